```python
import math
import jax, jax.numpy as jnp
from jax import lax
import numpy as np

D_MODEL = 2048
BATCH = 4
SEQ = 2048
DEPTH = 2

D_MIX = D_MODEL
GLA_W = D_MIX // 4
GDN_W = D_MIX // 4
DIFF_W = D_MIX // 2
GLA_HEADS = 4
GLA_DV = GLA_W // GLA_HEADS
GLA_DK = GLA_DV // 2
GLA_RANK = 16
GLA_TAU = 16.0
GDN_HEADS = 4
GDN_D = GDN_W // GDN_HEADS
CONV_K = 4
DIFF_HEADS = 4
DIFF_DV = DIFF_W // DIFF_HEADS
DIFF_D = DIFF_DV // 2
CHUNK = 64
Q_BLOCK = 128
ROPE_THETA = 10000.0
EPS = 1e-6

IN_SPLITS = (
    GLA_HEADS * GLA_DK, GLA_HEADS * GLA_DK, GLA_W, GLA_RANK, GLA_W,
    GDN_W, GDN_W, GDN_W, GDN_HEADS, GDN_HEADS, GDN_W,
    DIFF_HEADS * 2 * DIFF_D, DIFF_HEADS * 2 * DIFF_D, DIFF_W, DIFF_W,
)
D_IN = sum(IN_SPLITS)

kernel_name = "hybrid_gla_gdn_diffattn_parallel_heads"


def rms_norm(x, w):
    xf = x.astype(jnp.float32)
    y = xf * lax.rsqrt(jnp.mean(xf * xf, axis=-1, keepdims=True) + EPS)
    return (y * w.astype(jnp.float32)).astype(x.dtype)


def l2_norm(x):
    xf = x.astype(jnp.float32)
    return xf * lax.rsqrt(jnp.sum(xf * xf, axis=-1, keepdims=True) + EPS)


def rope_tables(positions, dim):
    half = dim // 2
    inv_freq = ROPE_THETA ** (-jnp.arange(half, dtype=jnp.float32) / half)
    ang = positions.astype(jnp.float32)[..., None] * inv_freq
    return jnp.cos(ang), jnp.sin(ang)


def apply_rope(x, cos, sin):
    xf = x.astype(jnp.float32)
    x1, x2 = jnp.split(xf, 2, axis=-1)
    return jnp.concatenate([x1 * cos - x2 * sin, x2 * cos + x1 * sin], axis=-1).astype(x.dtype)


def causal_depthwise_conv(x, w):
    C = x.shape[-1]
    return lax.conv_general_dilated(x, w[:, None, :].astype(x.dtype), window_strides=(1,),
                                    padding=[(CONV_K - 1, 0)], dimension_numbers=('NWC', 'WIO', 'NWC'),
                                    feature_group_count=C)


def _to_chunks(t, B, N, H):
    t = t.astype(jnp.float32)
    return jnp.moveaxis(t.reshape((B, N, CHUNK, H) + t.shape[3:]), 3, 1)


def gla_chunked(q, k, v, log_a):
    B, S, H, DK = q.shape
    DV = v.shape[-1]
    N = S // CHUNK
    q = _to_chunks(q, B, N, H) * DK ** -0.5
    k = _to_chunks(k, B, N, H)
    v = _to_chunks(v, B, N, H)
    b = jnp.cumsum(_to_chunks(log_a, B, N, H), axis=3)
    q_e = q * jnp.exp(b)
    k_e = k * jnp.exp(-b)
    causal = jnp.tril(jnp.ones((CHUNK, CHUNK), dtype=bool))
    att = jnp.where(causal, jnp.einsum('bhncd,bhnsd->bhncs', q_e, k_e), 0.0)
    o_intra = jnp.einsum('bhncs,bhnsv->bhncv', att, v)
    b_last = b[:, :, :, -1:, :]
    d_state = jnp.einsum('bhncd,bhncv->bhndv', k * jnp.exp(b_last - b), v)
    decay = jnp.exp(b_last[:, :, :, 0, :])

    def step(state, inp):
        ds_n, dec_n = inp
        return dec_n[..., None] * state + ds_n, state

    s0 = jnp.zeros((B, H, DK, DV), jnp.float32)
    _, s_prev = lax.scan(step, s0, (jnp.moveaxis(d_state, 2, 0), jnp.moveaxis(decay, 2, 0)))
    s_prev = jnp.moveaxis(s_prev, 0, 2)
    o = o_intra + jnp.einsum('bhncd,bhndv->bhncv', q_e, s_prev)
    return jnp.moveaxis(o, 1, 3).reshape(B, S, H, DV)


def gated_delta_chunked(q, k, v, g, beta):
    B, S, H, DK = q.shape
    DV = v.shape[-1]
    N = S // CHUNK
    q = _to_chunks(q, B, N, H) * DK ** -0.5
    k = _to_chunks(k, B, N, H)
    v = _to_chunks(v, B, N, H)
    gc = jnp.cumsum(_to_chunks(g, B, N, H), axis=-1)
    beta = _to_chunks(beta, B, N, H)
    incl = jnp.tril(jnp.ones((CHUNK, CHUNK), dtype=bool))
    strict = jnp.tril(jnp.ones((CHUNK, CHUNK), dtype=bool), k=-1)
    decay = jnp.exp(jnp.where(incl, gc[..., :, None] - gc[..., None, :], -jnp.inf))
    k_beta = k * beta[..., None]
    v_beta = v * beta[..., None]
    lower = jnp.where(strict, jnp.einsum('bhncd,bhnsd->bhncs', k_beta, k) * decay, 0.0)
    eye = jnp.eye(CHUNK, dtype=jnp.float32)
    rhs = jnp.concatenate([v_beta, k_beta * jnp.exp(gc)[..., None]], axis=-1)
    sol = lax.linalg.triangular_solve(eye + lower, rhs, left_side=True, lower=True, unit_diagonal=True)
    u, w = sol[..., :DV], sol[..., DV:]
    qk = jnp.einsum('bhncd,bhnsd->bhncs', q, k) * decay
    q_e = q * jnp.exp(gc)[..., None]
    k_tail = k * jnp.exp(gc[..., -1:] - gc)[..., None]
    chunk_decay = jnp.exp(gc[..., -1])

    def step(state, inp):
        u_n, w_n, qe_n, qk_n, kt_n, cd_n = inp
        v_new = u_n - jnp.einsum('bhck,bhkv->bhcv', w_n, state)
        o_n = jnp.einsum('bhck,bhkv->bhcv', qe_n, state) + jnp.einsum('bhcs,bhsv->bhcv', qk_n, v_new)
        state = state * cd_n[..., None, None] + jnp.einsum('bhck,bhcv->bhkv', kt_n, v_new)
        return state, o_n

    xs = tuple(jnp.moveaxis(t, 2, 0) for t in (u, w, q_e, qk, k_tail, chunk_decay))
    s0 = jnp.zeros((B, H, DK, DV), jnp.float32)
    _, o = lax.scan(step, s0, xs)
    o = jnp.moveaxis(o, 0, 2)
    return jnp.moveaxis(o, 1, 3).reshape(B, S, H, DV)


def diff_attention(q, k, v, lam):
    B, H, _, S, D = q.shape
    nb = S // Q_BLOCK
    scale = D ** -0.5
    q_blocks = jnp.moveaxis(q.reshape(B, H, 2, nb, Q_BLOCK, D), 3, 0)
    key_idx = jnp.arange(S)

    def block(args):
        qb, i = args
        s = jnp.einsum('bhmqd,bhmkd->bhmqk', qb, k).astype(jnp.float32) * scale
        q_idx = i * Q_BLOCK + jnp.arange(Q_BLOCK)
        s = jnp.where(key_idx[None, :] <= q_idx[:, None], s, -jnp.inf)
        p = jax.nn.softmax(s, axis=-1)
        a = p[:, :, 0] - lam * p[:, :, 1]
        return jnp.einsum('bhqk,bhkv->bhqv', a.astype(v.dtype), v)

    out = lax.map(block, (q_blocks, jnp.arange(nb)))
    return jnp.moveaxis(out, 0, 2).reshape(B, H, S, v.shape[-1])


def setup_inputs(seed: int = 0) -> dict:
    key = jax.random.key(seed)
    ks = jax.random.split(key, 20)
    f32 = jnp.float32
    x = jax.random.normal(ks[0], (BATCH, SEQ, D_MODEL), f32)
    c = jax.random.normal(ks[1], (BATCH, D_MODEL), f32)
    offsets = jax.random.randint(ks[2], (BATCH, 1), 0, 4096, dtype=jnp.int32)
    positions = offsets + jnp.arange(SEQ, dtype=jnp.int32)[None, :]
    norm_w = 1.0 + 0.02 * jax.random.normal(ks[3], (DEPTH, D_MODEL), f32)
    w_ada = jax.random.normal(ks[4], (DEPTH, D_MODEL, 3 * D_MODEL), f32) * D_MODEL ** -0.5
    b_ada = 0.01 * jax.random.normal(ks[5], (DEPTH, 3 * D_MODEL), f32)
    w_in = jax.random.normal(ks[6], (DEPTH, D_MODEL, D_IN), f32) * D_MODEL ** -0.5
    gla_w_lr = jax.random.normal(ks[7], (DEPTH, GLA_RANK, GLA_HEADS * GLA_DK), f32) * GLA_RANK ** -0.5
    gla_b_lr = 0.01 * jax.random.normal(ks[8], (DEPTH, GLA_HEADS * GLA_DK), f32)
    gla_norm_w = 1.0 + 0.02 * jax.random.normal(ks[9], (DEPTH, GLA_DV), f32)
    gdn_conv_w = jax.random.normal(ks[10], (DEPTH, CONV_K, 3 * GDN_W), f32) * CONV_K ** -0.5
    gdn_a_log = jnp.log(jax.random.uniform(ks[11], (DEPTH, GDN_HEADS), f32, 1.0, 16.0))
    dt = jnp.exp(jax.random.uniform(ks[12], (DEPTH, GDN_HEADS), f32, math.log(1e-3), math.log(1e-1)))
    gdn_dt_bias = dt + jnp.log(-jnp.expm1(-dt))
    gdn_norm_w = 1.0 + 0.02 * jax.random.normal(ks[13], (DEPTH, GDN_D), f32)
    diff_q_norm_w = 1.0 + 0.02 * jax.random.normal(ks[14], (DEPTH, DIFF_D), f32)
    diff_k_norm_w = 1.0 + 0.02 * jax.random.normal(ks[15], (DEPTH, DIFF_D), f32)
    diff_lambda = 0.1 * jax.random.normal(ks[16], (DEPTH, 4, DIFF_D), f32)
    diff_norm_w = 1.0 + 0.02 * jax.random.normal(ks[17], (DEPTH, DIFF_DV), f32)
    w_out = jax.random.normal(ks[18], (DEPTH, D_MIX, D_MODEL), f32) * D_MIX ** -0.5
    return {"x": x, "c": c, "positions": positions, "norm_w": norm_w, "w_ada": w_ada, "b_ada": b_ada,
            "w_in": w_in, "gla_w_lr": gla_w_lr, "gla_b_lr": gla_b_lr, "gla_norm_w": gla_norm_w,
            "gdn_conv_w": gdn_conv_w, "gdn_a_log": gdn_a_log, "gdn_dt_bias": gdn_dt_bias, "gdn_norm_w": gdn_norm_w,
            "diff_q_norm_w": diff_q_norm_w, "diff_k_norm_w": diff_k_norm_w, "diff_lambda": diff_lambda,
            "diff_norm_w": diff_norm_w, "w_out": w_out}


def reference(x, c, positions, norm_w, w_ada, b_ada, w_in, gla_w_lr, gla_b_lr, gla_norm_w,
              gdn_conv_w, gdn_a_log, gdn_dt_bias, gdn_norm_w, diff_q_norm_w, diff_k_norm_w, diff_lambda,
              diff_norm_w, w_out):
    B, S, _ = x.shape
    cos, sin = rope_tables(positions, DIFF_D)
    cos_b, sin_b = cos[:, None, None], sin[:, None, None]
    bounds = np.cumsum(IN_SPLITS)[:-1].tolist()
    c_act = jax.nn.silu(c)

    for l in range(DEPTH):
        shift, scale, gate = jnp.split(c_act @ w_ada[l] + b_ada[l], 3, axis=-1)
        h = rms_norm(x, norm_w[l]) * (1.0 + scale[:, None, :]) + shift[:, None, :]
        (gq, gk, gv, glr, gz, dq, dk, dv, da, db, dz, aq, ak, av, az) = jnp.split(h @ w_in[l], bounds, axis=-1)

        log_a = jax.nn.log_sigmoid((glr @ gla_w_lr[l] + gla_b_lr[l]).astype(jnp.float32)) / GLA_TAU
        o_gla = gla_chunked(gq.reshape(B, S, GLA_HEADS, GLA_DK), gk.reshape(B, S, GLA_HEADS, GLA_DK),
                            gv.reshape(B, S, GLA_HEADS, GLA_DV), log_a.reshape(B, S, GLA_HEADS, GLA_DK))
        o_gla = (rms_norm(o_gla, gla_norm_w[l]).reshape(B, S, GLA_W) * jax.nn.silu(gz)).astype(x.dtype)

        qkv = jax.nn.silu(causal_depthwise_conv(jnp.concatenate([dq, dk, dv], axis=-1), gdn_conv_w[l]))
        cq, ck, cv = jnp.split(qkv, 3, axis=-1)
        g = -jnp.exp(gdn_a_log[l].astype(jnp.float32)) * jax.nn.softplus(
            (da + gdn_dt_bias[l]).astype(jnp.float32))
        beta = jax.nn.sigmoid(db.astype(jnp.float32))
        o_gdn = gated_delta_chunked(l2_norm(cq.reshape(B, S, GDN_HEADS, GDN_D)),
                                    l2_norm(ck.reshape(B, S, GDN_HEADS, GDN_D)),
                                    cv.reshape(B, S, GDN_HEADS, GDN_D), g, beta)
        o_gdn = (rms_norm(o_gdn, gdn_norm_w[l]).reshape(B, S, GDN_W) * jax.nn.silu(dz)).astype(x.dtype)

        q_d = rms_norm(aq.reshape(B, S, DIFF_HEADS, 2, DIFF_D), diff_q_norm_w[l]).transpose(0, 2, 3, 1, 4)
        k_d = rms_norm(ak.reshape(B, S, DIFF_HEADS, 2, DIFF_D), diff_k_norm_w[l]).transpose(0, 2, 3, 1, 4)
        q_d = apply_rope(q_d, cos_b, sin_b)
        k_d = apply_rope(k_d, cos_b, sin_b)
        v_d = av.reshape(B, S, DIFF_HEADS, DIFF_DV).transpose(0, 2, 1, 3)
        lam_init = 0.8 - 0.6 * math.exp(-0.3 * l)
        lv = diff_lambda[l].astype(jnp.float32)
        lam = jnp.exp(jnp.sum(lv[0] * lv[1])) - jnp.exp(jnp.sum(lv[2] * lv[3])) + lam_init
        o_diff = diff_attention(q_d, k_d, v_d, lam)
        o_diff = rms_norm(o_diff, diff_norm_w[l]) * (1.0 - lam_init)
        o_diff = (o_diff.transpose(0, 2, 1, 3).reshape(B, S, DIFF_W) * jax.nn.silu(az)).astype(x.dtype)

        y = jnp.concatenate([o_gla, o_gdn, o_diff], axis=-1) @ w_out[l]
        x = x + gate[:, None, :] * y
    return x
```

```python
import functools
import math

import jax
import jax.numpy as jnp
from jax import lax
from jax.experimental import pallas as pl
from jax.experimental.pallas import tpu as pltpu

F32 = jnp.float32
BF16 = jnp.bfloat16

D_MODEL = 2048
BATCH = 4
SEQ = 2048
DEPTH = 2
TOKENS = BATCH * SEQ

GLA_HEADS = 4
GLA_DK = 64
GLA_DV = 128
GLA_W = GLA_HEADS * GLA_DV
GLA_RANK = 16
GLA_TAU = 16.0
GDN_HEADS = 4
GDN_D = 128
GDN_W = GDN_HEADS * GDN_D
CONV_K = 4
DIFF_HEADS = 4
DIFF_D = 128
DIFF_DV = 256
DIFF_W = DIFF_HEADS * DIFF_DV
CHUNK = 64
N_CHUNKS = SEQ // CHUNK
ROPE_THETA = 10000.0
EPS = 1e-6

LANES = 128
CONV_HALO = 8

_IN_SPLITS = (
    ("gq", GLA_HEADS * GLA_DK), ("gk", GLA_HEADS * GLA_DK), ("gv", GLA_W), ("glr", GLA_RANK), ("gz", GLA_W),
    ("dq", GDN_W), ("dk", GDN_W), ("dv", GDN_W), ("da", GDN_HEADS), ("db", GDN_HEADS), ("dz", GDN_W),
    ("aq", DIFF_HEADS * 2 * DIFF_D), ("ak", DIFF_HEADS * 2 * DIFF_D), ("av", DIFF_W), ("az", DIFF_W),
)
_IN_OFFSETS = {}
_off = 0
for _name, _width in _IN_SPLITS:
    _IN_OFFSETS[_name] = (_off, _off + _width)
    _off += _width
D_IN = _off

_BF_ORDER = ("aq", "ak", "av", "gq", "gk", "gv", "dq", "dk", "dv")
_F32_ORDER = ("gz", "dz", "az", "glr", "da", "db")
N_BF = 5632
N_F32 = 2560
PROJ_TN = 512
NB_BF = N_BF // PROJ_TN
NB_F32 = N_F32 // PROJ_TN
SMALL_COL = 2048
GLR_LANE = 0
DA_LANE = GLA_RANK
DB_LANE = GLA_RANK + GDN_HEADS

VMEM_LIMIT = 48 * 1024 * 1024


def _sigmoid(x):
    return 1.0 / (1.0 + jnp.exp(-x))


def _silu(x):
    return x * _sigmoid(x)


def _softplus(x):
    return jnp.maximum(x, 0.0) + jnp.log1p(jnp.exp(-jnp.abs(x)))


def _dot(a, b):
    return jnp.dot(a, b, preferred_element_type=F32)


def _dot_nt(a, b):
    return lax.dot_general(a, b, (((1,), (1,)), ((), ())), preferred_element_type=F32)


def _dot_tn(a, b):
    return lax.dot_general(a, b, (((0,), (0,)), ((), ())), preferred_element_type=F32)


def _dot_f32(a, b):
    return jnp.dot(a, b, precision=lax.Precision.HIGHEST, preferred_element_type=F32)


def _rms(x, w):
    return x * lax.rsqrt(jnp.mean(x * x, axis=-1, keepdims=True) + EPS) * w


def _params(*semantics):
    return pltpu.CompilerParams(dimension_semantics=semantics, vmem_limit_bytes=VMEM_LIMIT)


ADA_TN = 1024


def _ada_kernel(c_ref, w_ref, b_ref, o_ref):
    c_act = _silu(c_ref[...])
    o_ref[0] = _dot(c_act.astype(BF16), w_ref[0].astype(BF16)) + b_ref[0]


def _ada_mod(c_pad, w_ada, b_ada):
    n3 = 3 * D_MODEL
    return pl.pallas_call(
        _ada_kernel,
        grid=(DEPTH, n3 // ADA_TN),
        in_specs=[
            pl.BlockSpec((8, D_MODEL), lambda l, n: (0, 0)),
            pl.BlockSpec((1, D_MODEL, ADA_TN), lambda l, n: (l, 0, n)),
            pl.BlockSpec((1, 1, ADA_TN), lambda l, n: (l, 0, n)),
        ],
        out_specs=pl.BlockSpec((1, 8, ADA_TN), lambda l, n: (l, 0, n)),
        out_shape=jax.ShapeDtypeStruct((DEPTH, 8, n3), F32),
        compiler_params=_params("arbitrary", "arbitrary"),
        name="ada_mod",
    )(c_pad, w_ada, b_ada.reshape(DEPTH, 1, n3))


def _rope_kernel(pos_ref, freq_ref, sign_ref, cos_ref, sin_ref):
    ang = pos_ref[0].astype(F32) * freq_ref[...]
    cos_ref[0] = jnp.cos(ang)
    sin_ref[0] = jnp.sin(ang) * sign_ref[...]


def _rope_tables(positions):
    half = DIFF_D // 2
    inv_freq = ROPE_THETA ** (-jnp.arange(half, dtype=F32) / half)
    freq2 = jnp.concatenate([inv_freq, inv_freq]).reshape(1, DIFF_D)
    sign = jnp.concatenate([-jnp.ones((half,), F32), jnp.ones((half,), F32)]).reshape(1, DIFF_D)
    tab = jax.ShapeDtypeStruct((BATCH, SEQ, DIFF_D), F32)
    return pl.pallas_call(
        _rope_kernel,
        grid=(BATCH,),
        in_specs=[
            pl.BlockSpec((1, SEQ, 1), lambda b: (b, 0, 0)),
            pl.BlockSpec((1, DIFF_D), lambda b: (0, 0)),
            pl.BlockSpec((1, DIFF_D), lambda b: (0, 0)),
        ],
        out_specs=[pl.BlockSpec((1, SEQ, DIFF_D), lambda b: (b, 0, 0))] * 2,
        out_shape=[tab, tab],
        compiler_params=_params("arbitrary"),
        name="rope_tables",
    )(positions.reshape(BATCH, SEQ, 1), freq2, sign)


PROJ_TM = 1024
PROJ_ROWS = 256


def _inproj_kernel(x_ref, nw_ref, shift_ref, scale_ref, w_ref, ob_ref, of_ref, h_ref):
    n = pl.program_id(1)

    @pl.when(n == 0)
    def _():
        nw = nw_ref[...]
        scale1 = 1.0 + scale_ref[0]
        shift = shift_ref[0]
        for i in range(PROJ_TM // PROJ_ROWS):
            rows = slice(i * PROJ_ROWS, (i + 1) * PROJ_ROWS)
            h = _rms(x_ref[rows, :], nw) * scale1 + shift
            h_ref[rows, :] = h.astype(BF16)

    acc = _dot(h_ref[...], w_ref[...])

    @pl.when(n < NB_BF)
    def _():
        ob_ref[...] = acc.astype(BF16)

    @pl.when(n >= NB_BF)
    def _():
        of_ref[...] = acc


def _in_proj(x2d, norm_w, mods3, w_perm):
    per_batch = SEQ // PROJ_TM
    return pl.pallas_call(
        _inproj_kernel,
        grid=(TOKENS // PROJ_TM, NB_BF + NB_F32),
        in_specs=[
            pl.BlockSpec((PROJ_TM, D_MODEL), lambda m, n: (m, 0)),
            pl.BlockSpec((1, D_MODEL), lambda m, n: (0, 0)),
            pl.BlockSpec((1, 1, D_MODEL), lambda m, n: (m // per_batch, 0, 0)),
            pl.BlockSpec((1, 1, D_MODEL), lambda m, n: (m // per_batch, 0, 1)),
            pl.BlockSpec((D_MODEL, PROJ_TN), lambda m, n: (0, n)),
        ],
        out_specs=[
            pl.BlockSpec((PROJ_TM, PROJ_TN), lambda m, n: (m, jnp.minimum(n, NB_BF - 1))),
            pl.BlockSpec((PROJ_TM, PROJ_TN), lambda m, n: (m, jnp.maximum(n - NB_BF, 0))),
        ],
        out_shape=[
            jax.ShapeDtypeStruct((TOKENS, N_BF), BF16),
            jax.ShapeDtypeStruct((TOKENS, N_F32), F32),
        ],
        scratch_shapes=[pltpu.VMEM((PROJ_TM, D_MODEL), BF16)],
        compiler_params=_params("arbitrary", "arbitrary"),
        name="in_proj",
    )(x2d, norm_w, mods3, mods3, w_perm)


def _permute_w_in(w):
    cols = [w[:, _IN_OFFSETS[k][0]:_IN_OFFSETS[k][1]] for k in _BF_ORDER + _F32_ORDER]
    used = sum(c.shape[1] for c in cols)
    cols.append(jnp.zeros((D_MODEL, N_BF + N_F32 - used), w.dtype))
    return jnp.concatenate(cols, axis=1).astype(BF16)


def _gla_kernel(gq_ref, gk_ref, gv_ref, gz_ref, sm_ref, wlr_ref, blr_ref, nw_ref, o_ref, la_ref):
    z = _dot_f32(sm_ref[...], wlr_ref[...]) + blr_ref[...]
    la_ref[...] = -_softplus(-z) * (1.0 / GLA_TAU)

    row = lax.broadcasted_iota(jnp.int32, (CHUNK, CHUNK), 0)
    col = lax.broadcasted_iota(jnp.int32, (CHUNK, CHUNK), 1)
    incl = col <= row
    tril = jnp.where(incl, 1.0, 0.0)
    nw = nw_ref[...]

    def body(n, states):
        r = pl.ds(pl.multiple_of(n * CHUNK, CHUNK), CHUNK)
        bc = _dot_f32(tril, la_ref[r, :])
        b_last = bc[CHUNK - 1:CHUNK, :]
        q = gq_ref[r, :].astype(F32) * GLA_DK ** -0.5
        k = gk_ref[r, :].astype(F32)
        qe = (q * jnp.exp(bc)).astype(BF16)
        ke = (k * jnp.exp(-bc)).astype(BF16)
        kt = (k * jnp.exp(b_last - bc)).astype(BF16)
        dec = jnp.exp(b_last)
        v_all = gv_ref[r, :]
        gz = gz_ref[r, :]
        new_states = []
        outs = []
        for h in range(GLA_HEADS):
            ks = slice(h * GLA_DK, (h + 1) * GLA_DK)
            vs = slice(h * GLA_DV, (h + 1) * GLA_DV)
            st = states[h]
            att = jnp.where(incl, _dot_nt(qe[:, ks], ke[:, ks]), 0.0)
            v = v_all[:, vs]
            o = _dot(att.astype(BF16), v) + _dot_nt(qe[:, ks], st.astype(BF16))
            new_states.append(st * dec[:, ks] + _dot_tn(v, kt[:, ks]))
            outs.append(_rms(o, nw) * _silu(gz[:, vs]))
        o_ref[r, :] = jnp.concatenate(outs, axis=-1).astype(BF16)
        return tuple(new_states)

    lax.fori_loop(0, N_CHUNKS, body, tuple(jnp.zeros((GLA_DV, GLA_DK), F32) for _ in range(GLA_HEADS)))


def _gla(proj_bf, proj_f32, wlr_pad, blr, nw):
    return pl.pallas_call(
        _gla_kernel,
        grid=(BATCH,),
        in_specs=[
            pl.BlockSpec((SEQ, 256), lambda b: (b, 12)),
            pl.BlockSpec((SEQ, 256), lambda b: (b, 13)),
            pl.BlockSpec((SEQ, GLA_W), lambda b: (b, 7)),
            pl.BlockSpec((SEQ, GLA_W), lambda b: (b, 0)),
            pl.BlockSpec((SEQ, LANES), lambda b: (b, SMALL_COL // LANES)),
            pl.BlockSpec((LANES, GLA_HEADS * GLA_DK), lambda b: (0, 0)),
            pl.BlockSpec((1, GLA_HEADS * GLA_DK), lambda b: (0, 0)),
            pl.BlockSpec((1, GLA_DV), lambda b: (0, 0)),
        ],
        out_specs=pl.BlockSpec((SEQ, GLA_W), lambda b: (b, 0)),
        out_shape=jax.ShapeDtypeStruct((TOKENS, GLA_W), BF16),
        scratch_shapes=[pltpu.VMEM((SEQ, GLA_HEADS * GLA_DK), F32)],
        compiler_params=_params("arbitrary"),
        name="gla",
    )(proj_bf, proj_bf, proj_bf, proj_f32, proj_f32, wlr_pad, blr, nw)


def _gdn_kernel(dq_ref, dk_ref, dv_ref, dz_ref, sm_ref, cw_ref, alog_ref, dtb_ref, nw_ref, o_ref,
                xq_ref, xk_ref, xv_ref, g_ref, beta_ref, st_ref):
    for src, dst in ((dq_ref, xq_ref), (dk_ref, xk_ref), (dv_ref, xv_ref)):
        dst[0:CONV_HALO, :] = jnp.zeros((CONV_HALO, GDN_W), F32)
        dst[CONV_HALO:CONV_HALO + SEQ, :] = src[...].astype(F32)
    sm = sm_ref[...]
    g_ref[...] = -jnp.exp(alog_ref[...]) * _softplus(sm + dtb_ref[...])
    beta_ref[...] = _sigmoid(sm)
    st_ref[...] = jnp.zeros((GDN_HEADS, GDN_D, GDN_D), F32)

    row = lax.broadcasted_iota(jnp.int32, (CHUNK, CHUNK), 0)
    col = lax.broadcasted_iota(jnp.int32, (CHUNK, CHUNK), 1)
    incl = col <= row
    strict = col < row
    tril = jnp.where(incl, 1.0, 0.0)
    rrow = lax.broadcasted_iota(jnp.int32, (CHUNK, 2 * LANES), 0)
    rcol = lax.broadcasted_iota(jnp.int32, (CHUNK, 2 * LANES), 1)
    rc = jnp.where(rcol < CHUNK, jnp.where(rrow > rcol, 1.0, 0.0), jnp.where(rcol < CHUNK + GDN_D, 1.0, 0.0))
    nw = nw_ref[...]
    cw = cw_ref[...]

    def conv(x_ref, base, w):
        win = x_ref[pl.ds(base, CHUNK + CONV_HALO), :]
        first = CONV_HALO - (CONV_K - 1)
        acc = win[first:first + CHUNK] * w[0:1]
        for j in range(1, CONV_K):
            acc = acc + win[first + j:first + j + CHUNK] * w[j:j + 1]
        return _silu(acc)

    def body(n, carry):
        base = pl.multiple_of(n * CHUNK, CHUNK)
        r = pl.ds(base, CHUNK)
        cq = conv(xq_ref, base, cw[:, 0:GDN_W])
        ck = conv(xk_ref, base, cw[:, GDN_W:2 * GDN_W])
        cv = conv(xv_ref, base, cw[:, 2 * GDN_W:3 * GDN_W])
        g_all = g_ref[r, :]
        beta_all = beta_ref[r, :]
        dz = dz_ref[r, :]
        outs = []
        for h in range(GDN_HEADS):
            hs = slice(h * GDN_D, (h + 1) * GDN_D)
            q = cq[:, hs]
            k = ck[:, hs]
            v = cv[:, hs]
            q = q * lax.rsqrt(jnp.sum(q * q, axis=-1, keepdims=True) + EPS) * GDN_D ** -0.5
            k = k * lax.rsqrt(jnp.sum(k * k, axis=-1, keepdims=True) + EPS)
            g_col = g_all[:, DA_LANE + h:DA_LANE + h + 1]
            b_col = beta_all[:, DB_LANE + h:DB_LANE + h + 1]
            e = _dot_f32(tril, g_col * rc)
            gc = e[:, CHUNK:CHUNK + GDN_D]
            decay = jnp.where(incl, jnp.exp(e[:, 0:CHUNK]), 0.0)
            g_last = gc[CHUNK - 1:CHUNK, :]
            eg = jnp.exp(gc)
            k_beta = k * b_col
            v_beta = v * b_col
            kb = k.astype(BF16)
            lower = jnp.where(strict, _dot_nt(k_beta.astype(BF16), kb) * decay, 0.0)
            pw = -lower
            tq = pw
            for _ in range(5):
                pw = _dot_f32(pw, pw)
                tq = tq + pw + _dot_f32(tq, pw)
            rhs = jnp.concatenate([v_beta, k_beta * eg], axis=-1)
            sol = rhs + _dot(tq.astype(BF16), rhs.astype(BF16))
            u = sol[:, 0:GDN_D]
            w = sol[:, GDN_D:2 * GDN_D]
            qk = _dot_nt(q.astype(BF16), kb) * decay
            qe = (q * eg).astype(BF16)
            k_tail = (k * jnp.exp(g_last - gc)).astype(BF16)
            state = st_ref[h]
            sb = state.astype(BF16)
            v_new = (u - _dot(w.astype(BF16), sb)).astype(BF16)
            o = _dot(qe, sb) + _dot(qk.astype(BF16), v_new)
            st_ref[h] = state * jnp.exp(g_last) + _dot_tn(k_tail, v_new)
            outs.append(_rms(o, nw) * _silu(dz[:, hs]))
        o_ref[r, :] = jnp.concatenate(outs, axis=-1).astype(BF16)
        return carry

    lax.fori_loop(0, N_CHUNKS, body, 0)


def _gdn(proj_bf, proj_f32, conv_w, alog_pad, dtb_pad, nw):
    pad = pltpu.VMEM((SEQ + CONV_HALO, GDN_W), F32)
    return pl.pallas_call(
        _gdn_kernel,
        grid=(BATCH,),
        in_specs=[
            pl.BlockSpec((SEQ, GDN_W), lambda b: (b, 8)),
            pl.BlockSpec((SEQ, GDN_W), lambda b: (b, 9)),
            pl.BlockSpec((SEQ, GDN_W), lambda b: (b, 10)),
            pl.BlockSpec((SEQ, GDN_W), lambda b: (b, 1)),
            pl.BlockSpec((SEQ, LANES), lambda b: (b, SMALL_COL // LANES)),
            pl.BlockSpec((CONV_K, 3 * GDN_W), lambda b: (0, 0)),
            pl.BlockSpec((1, LANES), lambda b: (0, 0)),
            pl.BlockSpec((1, LANES), lambda b: (0, 0)),
            pl.BlockSpec((1, GDN_D), lambda b: (0, 0)),
        ],
        out_specs=pl.BlockSpec((SEQ, GDN_W), lambda b: (b, 0)),
        out_shape=jax.ShapeDtypeStruct((TOKENS, GDN_W), BF16),
        scratch_shapes=[pad, pad, pad,
                        pltpu.VMEM((SEQ, LANES), F32), pltpu.VMEM((SEQ, LANES), F32),
                        pltpu.VMEM((GDN_HEADS, GDN_D, GDN_D), F32)],
        compiler_params=_params("arbitrary"),
        name="gdn",
    )(proj_bf, proj_bf, proj_bf, proj_f32, proj_f32, conv_w, alog_pad, dtb_pad, nw)


PREP_TS = 512
ATT_T = 512


def _diff_prep_kernel(aq_ref, ak_ref, cos_ref, sin_ref, qw_ref, kw_ref, q_ref, k_ref):
    cos2 = cos_ref[0]
    sin2 = sin_ref[0]
    for src, w_ref, dst, scale in ((aq_ref, qw_ref, q_ref, DIFF_D ** -0.5), (ak_ref, kw_ref, k_ref, 1.0)):
        w = w_ref[...]
        for g in range(2 * DIFF_HEADS):
            cols = slice(g * DIFF_D, (g + 1) * DIFF_D)
            y = _rms(src[:, cols].astype(F32), w)
            y = y * cos2 + pltpu.roll(y, DIFF_D // 2, 1) * sin2
            dst[:, cols] = (y * scale).astype(BF16)


def _diff_prep(proj_bf, cos2, sin2, qw, kw):
    per_batch = SEQ // PREP_TS
    width = 2 * DIFF_HEADS * DIFF_D
    tab_spec = pl.BlockSpec((1, PREP_TS, DIFF_D), lambda i: (i // per_batch, i % per_batch, 0))
    out = jax.ShapeDtypeStruct((TOKENS, width), BF16)
    return pl.pallas_call(
        _diff_prep_kernel,
        grid=(TOKENS // PREP_TS,),
        in_specs=[
            pl.BlockSpec((PREP_TS, width), lambda i: (i, 0)),
            pl.BlockSpec((PREP_TS, width), lambda i: (i, 1)),
            tab_spec, tab_spec,
            pl.BlockSpec((1, DIFF_D), lambda i: (0, 0)),
            pl.BlockSpec((1, DIFF_D), lambda i: (0, 0)),
        ],
        out_specs=[pl.BlockSpec((PREP_TS, width), lambda i: (i, 0))] * 2,
        out_shape=[out, out],
        compiler_params=_params("arbitrary"),
        name="diff_prep",
    )(proj_bf, proj_bf, cos2, sin2, qw, kw)


def _diff_attn_kernel(q1_ref, q2_ref, k1_ref, k2_ref, v_ref, az_ref, lam_ref, nw_ref, o_ref,
                      m_ref, l_ref, acc_ref, *, lam_init):
    i = pl.program_id(2)
    m_ref[...] = jnp.full(m_ref.shape, -jnp.inf, F32)
    l_ref[...] = jnp.zeros(l_ref.shape, F32)
    acc_ref[...] = jnp.zeros(acc_ref.shape, F32)
    qs = (q1_ref[...], q2_ref[...])
    ks = (k1_ref, k2_ref)
    row = lax.broadcasted_iota(jnp.int32, (ATT_T, ATT_T), 0)
    col = lax.broadcasted_iota(jnp.int32, (ATT_T, ATT_T), 1)
    causal = col <= row

    def block(j, masked):
        kr = pl.ds(pl.multiple_of(j * ATT_T, ATT_T), ATT_T)
        v = v_ref[kr, :]
        for m in range(2):
            s = _dot_nt(qs[m], ks[m][kr, :])
            if masked:
                s = jnp.where(causal, s, -jnp.inf)
            m_prev = m_ref[m]
            m_new = jnp.maximum(m_prev, jnp.max(s, axis=-1, keepdims=True))
            alpha = jnp.exp(m_prev - m_new)
            p = jnp.exp(s - m_new)
            l_ref[m] = alpha * l_ref[m] + jnp.sum(p, axis=-1, keepdims=True)
            acc_ref[m] = alpha * acc_ref[m] + _dot(p.astype(BF16), v)
            m_ref[m] = m_new

    def body(j, carry):
        block(j, False)
        return carry

    lax.fori_loop(0, i, body, 0)
    block(i, True)

    lv = lam_ref[...]
    lam = (jnp.exp(jnp.sum(lv[0:1] * lv[1:2], axis=-1, keepdims=True))
           - jnp.exp(jnp.sum(lv[2:3] * lv[3:4], axis=-1, keepdims=True)) + lam_init)
    o = acc_ref[0] / l_ref[0] - lam * (acc_ref[1] / l_ref[1])
    o = _rms(o, nw_ref[...]) * (1.0 - lam_init)
    o_ref[...] = (o * _silu(az_ref[...])).astype(BF16)


def _diff_attn(q_d, k_d, proj_bf, proj_f32, lam, nw, lam_init):
    nq = SEQ // ATT_T
    return pl.pallas_call(
        functools.partial(_diff_attn_kernel, lam_init=lam_init),
        grid=(BATCH, DIFF_HEADS, nq),
        in_specs=[
            pl.BlockSpec((ATT_T, DIFF_D), lambda b, h, i: (b * nq + i, 2 * h)),
            pl.BlockSpec((ATT_T, DIFF_D), lambda b, h, i: (b * nq + i, 2 * h + 1)),
            pl.BlockSpec((SEQ, DIFF_D), lambda b, h, i: (b, 2 * h)),
            pl.BlockSpec((SEQ, DIFF_D), lambda b, h, i: (b, 2 * h + 1)),
            pl.BlockSpec((SEQ, DIFF_DV), lambda b, h, i: (b, 8 + h)),
            pl.BlockSpec((ATT_T, DIFF_DV), lambda b, h, i: (b * nq + i, 4 + h)),
            pl.BlockSpec((4, DIFF_D), lambda b, h, i: (0, 0)),
            pl.BlockSpec((1, DIFF_DV), lambda b, h, i: (0, 0)),
        ],
        out_specs=pl.BlockSpec((ATT_T, DIFF_DV), lambda b, h, i: (b * nq + i, h)),
        out_shape=jax.ShapeDtypeStruct((TOKENS, DIFF_W), BF16),
        scratch_shapes=[pltpu.VMEM((2, ATT_T, 1), F32), pltpu.VMEM((2, ATT_T, 1), F32),
                        pltpu.VMEM((2, ATT_T, DIFF_DV), F32)],
        compiler_params=_params("arbitrary", "arbitrary", "arbitrary"),
        name="diff_attn",
    )(q_d, q_d, k_d, k_d, proj_bf, proj_f32, lam, nw)


OUT_TM = 512


def _outproj_kernel(og_ref, od_ref, oa_ref, w_ref, x_ref, gate_ref, o_ref):
    y = _dot(og_ref[...], w_ref[0:GLA_W, :])
    y = y + _dot(od_ref[...], w_ref[GLA_W:GLA_W + GDN_W, :])
    y = y + _dot(oa_ref[...], w_ref[GLA_W + GDN_W:, :])
    o_ref[...] = x_ref[...] + gate_ref[0] * y


def _out_proj(o_gla, o_gdn, o_diff, w_out_bf, x2d, mods3):
    per_batch = SEQ // OUT_TM
    return pl.pallas_call(
        _outproj_kernel,
        grid=(TOKENS // OUT_TM,),
        in_specs=[
            pl.BlockSpec((OUT_TM, GLA_W), lambda m: (m, 0)),
            pl.BlockSpec((OUT_TM, GDN_W), lambda m: (m, 0)),
            pl.BlockSpec((OUT_TM, DIFF_W), lambda m: (m, 0)),
            pl.BlockSpec((D_MODEL, D_MODEL), lambda m: (0, 0)),
            pl.BlockSpec((OUT_TM, D_MODEL), lambda m: (m, 0)),
            pl.BlockSpec((1, 1, D_MODEL), lambda m: (m // per_batch, 0, 2)),
        ],
        out_specs=pl.BlockSpec((OUT_TM, D_MODEL), lambda m: (m, 0)),
        out_shape=jax.ShapeDtypeStruct((TOKENS, D_MODEL), F32),
        compiler_params=_params("arbitrary"),
        name="out_proj",
    )(o_gla, o_gdn, o_diff, w_out_bf, x2d, mods3)


def _lane_pad(vec, lane):
    return jnp.zeros((1, LANES), F32).at[0, lane:lane + vec.shape[0]].set(vec.astype(F32))


def kernel(x, c, positions, norm_w, w_ada, b_ada, w_in, gla_w_lr, gla_b_lr, gla_norm_w, gdn_conv_w, gdn_a_log,
           gdn_dt_bias, gdn_norm_w, diff_q_norm_w, diff_k_norm_w, diff_lambda, diff_norm_w, w_out):
    c_pad = jnp.zeros((8, D_MODEL), F32).at[:BATCH].set(c)
    mods = _ada_mod(c_pad, w_ada, b_ada)
    cos2, sin2 = _rope_tables(positions)
    x2d = x.reshape(TOKENS, D_MODEL)
    for l in range(DEPTH):
        mods3 = mods[l].reshape(8, 1, 3 * D_MODEL)
        proj_bf, proj_f32 = _in_proj(x2d, norm_w[l].reshape(1, D_MODEL), mods3, _permute_w_in(w_in[l]))
        wlr_pad = jnp.zeros((LANES, GLA_HEADS * GLA_DK), F32).at[GLR_LANE:GLR_LANE + GLA_RANK].set(gla_w_lr[l])
        o_gla = _gla(proj_bf, proj_f32, wlr_pad, gla_b_lr[l].reshape(1, -1), gla_norm_w[l].reshape(1, -1))
        o_gdn = _gdn(proj_bf, proj_f32, gdn_conv_w[l], _lane_pad(gdn_a_log[l], DA_LANE),
                     _lane_pad(gdn_dt_bias[l], DA_LANE), gdn_norm_w[l].reshape(1, -1))
        q_d, k_d = _diff_prep(proj_bf, cos2, sin2, diff_q_norm_w[l].reshape(1, -1), diff_k_norm_w[l].reshape(1, -1))
        lam_init = 0.8 - 0.6 * math.exp(-0.3 * l)
        o_diff = _diff_attn(q_d, k_d, proj_bf, proj_f32, diff_lambda[l], diff_norm_w[l].reshape(1, -1), lam_init)
        x2d = _out_proj(o_gla, o_gdn, o_diff, w_out[l].astype(BF16), x2d, mods3)
    return x2d.reshape(BATCH, SEQ, D_MODEL)
```

```python
import functools
import math

import jax
import jax.numpy as jnp
from jax import lax
from jax.experimental import pallas as pl
from jax.experimental.pallas import tpu as pltpu

F32 = jnp.float32
BF16 = jnp.bfloat16

D_MODEL = 2048
BATCH = 4
SEQ = 2048
DEPTH = 2
TOKENS = BATCH * SEQ

GLA_HEADS = 4
GLA_DK = 64
GLA_DV = 128
GLA_W = GLA_HEADS * GLA_DV
GLA_RANK = 16
GLA_TAU = 16.0
GDN_HEADS = 4
GDN_D = 128
GDN_W = GDN_HEADS * GDN_D
CONV_K = 4
DIFF_HEADS = 4
DIFF_D = 128
DIFF_DV = 256
DIFF_W = DIFF_HEADS * DIFF_DV
CHUNK = 64
N_CHUNKS = SEQ // CHUNK
ROPE_THETA = 10000.0
EPS = 1e-6
LOG2E = math.log2(math.e)

LANES = 128

_IN_SPLITS = (
    ("gq", GLA_HEADS * GLA_DK), ("gk", GLA_HEADS * GLA_DK), ("gv", GLA_W), ("glr", GLA_RANK), ("gz", GLA_W),
    ("dq", GDN_W), ("dk", GDN_W), ("dv", GDN_W), ("da", GDN_HEADS), ("db", GDN_HEADS), ("dz", GDN_W),
    ("aq", DIFF_HEADS * 2 * DIFF_D), ("ak", DIFF_HEADS * 2 * DIFF_D), ("av", DIFF_W), ("az", DIFF_W),
)
_IN_OFFSETS = {}
_off = 0
for _name, _width in _IN_SPLITS:
    _IN_OFFSETS[_name] = (_off, _off + _width)
    _off += _width
D_IN = _off

_BF_ORDER = ("aq", "ak", "av", "gq", "gk", "gv", "dq", "dk", "dv")
_F32_ORDER = ("gz", "dz", "az", "glr", "da", "db")
N_BF = 5632
N_F32 = 2560
PROJ_TN = 512
NB_BF = N_BF // PROJ_TN
NB_F32 = N_F32 // PROJ_TN
SMALL_COL = 2048
GLR_LANE = 0
DA_LANE = GLA_RANK
DB_LANE = GLA_RANK + GDN_HEADS

VMEM_LIMIT = 48 * 1024 * 1024


def _sigmoid(x):
    return 1.0 / (1.0 + jnp.exp(-x))


def _silu(x):
    return x * _sigmoid(x)


def _softplus(x):
    return jnp.maximum(x, 0.0) + jnp.log1p(jnp.exp(-jnp.abs(x)))


def _dot(a, b):
    return jnp.dot(a, b, preferred_element_type=F32)


def _dot_nt(a, b):
    return lax.dot_general(a, b, (((1,), (1,)), ((), ())), preferred_element_type=F32)


def _dot_tn(a, b):
    return lax.dot_general(a, b, (((0,), (0,)), ((), ())), preferred_element_type=F32)


def _dot_f32(a, b):
    return jnp.dot(a, b, precision=lax.Precision.HIGHEST, preferred_element_type=F32)


def _rms(x, w):
    return x * lax.rsqrt(jnp.mean(x * x, axis=-1, keepdims=True) + EPS) * w


def _params(*semantics):
    return pltpu.CompilerParams(dimension_semantics=semantics, vmem_limit_bytes=VMEM_LIMIT)


ADA_TN = 1024


def _ada_kernel(c_ref, w_ref, b_ref, o_ref):
    c_act = _silu(c_ref[...])
    o_ref[0] = _dot(c_act.astype(BF16), w_ref[0].astype(BF16)) + b_ref[0]


def _ada_mod(c_pad, w_ada, b_ada):
    n3 = 3 * D_MODEL
    return pl.pallas_call(
        _ada_kernel,
        grid=(DEPTH, n3 // ADA_TN),
        in_specs=[
            pl.BlockSpec((8, D_MODEL), lambda l, n: (0, 0)),
            pl.BlockSpec((1, D_MODEL, ADA_TN), lambda l, n: (l, 0, n)),
            pl.BlockSpec((1, 1, ADA_TN), lambda l, n: (l, 0, n)),
        ],
        out_specs=pl.BlockSpec((1, 8, ADA_TN), lambda l, n: (l, 0, n)),
        out_shape=jax.ShapeDtypeStruct((DEPTH, 8, n3), F32),
        compiler_params=_params("arbitrary", "arbitrary"),
        name="ada_mod",
    )(c_pad, w_ada, b_ada.reshape(DEPTH, 1, n3))


def _rope_kernel(pos_ref, freq_ref, sign_ref, cos_ref, sin_ref):
    ang = pos_ref[0].astype(F32) * freq_ref[...]
    cos_ref[0] = jnp.cos(ang)
    sin_ref[0] = jnp.sin(ang) * sign_ref[...]


def _rope_tables(positions):
    half = DIFF_D // 2
    inv_freq = ROPE_THETA ** (-jnp.arange(half, dtype=F32) / half)
    freq2 = jnp.concatenate([inv_freq, inv_freq]).reshape(1, DIFF_D)
    sign = jnp.concatenate([-jnp.ones((half,), F32), jnp.ones((half,), F32)]).reshape(1, DIFF_D)
    tab = jax.ShapeDtypeStruct((BATCH, SEQ, DIFF_D), F32)
    return pl.pallas_call(
        _rope_kernel,
        grid=(BATCH,),
        in_specs=[
            pl.BlockSpec((1, SEQ, 1), lambda b: (b, 0, 0)),
            pl.BlockSpec((1, DIFF_D), lambda b: (0, 0)),
            pl.BlockSpec((1, DIFF_D), lambda b: (0, 0)),
        ],
        out_specs=[pl.BlockSpec((1, SEQ, DIFF_D), lambda b: (b, 0, 0))] * 2,
        out_shape=[tab, tab],
        compiler_params=_params("arbitrary"),
        name="rope_tables",
    )(positions.reshape(BATCH, SEQ, 1), freq2, sign)


PROJ_TM = 1024
PROJ_ROWS = 256


def _inproj_kernel(x_ref, nw_ref, shift_ref, scale_ref, w_ref, ob_ref, of_ref, h_ref):
    n = pl.program_id(1)

    @pl.when(n == 0)
    def _():
        nw = nw_ref[...]
        scale1 = 1.0 + scale_ref[0]
        shift = shift_ref[0]
        for i in range(PROJ_TM // PROJ_ROWS):
            rows = slice(i * PROJ_ROWS, (i + 1) * PROJ_ROWS)
            h = _rms(x_ref[rows, :], nw) * scale1 + shift
            h_ref[rows, :] = h.astype(BF16)

    acc = _dot(h_ref[...], w_ref[...])

    @pl.when(n < NB_BF)
    def _():
        ob_ref[...] = acc.astype(BF16)

    @pl.when(n >= NB_BF)
    def _():
        of_ref[...] = acc


def _in_proj(x2d, norm_w, mods3, w_perm):
    per_batch = SEQ // PROJ_TM
    return pl.pallas_call(
        _inproj_kernel,
        grid=(TOKENS // PROJ_TM, NB_BF + NB_F32),
        in_specs=[
            pl.BlockSpec((PROJ_TM, D_MODEL), lambda m, n: (m, 0)),
            pl.BlockSpec((1, D_MODEL), lambda m, n: (0, 0)),
            pl.BlockSpec((1, 1, D_MODEL), lambda m, n: (m // per_batch, 0, 0)),
            pl.BlockSpec((1, 1, D_MODEL), lambda m, n: (m // per_batch, 0, 1)),
            pl.BlockSpec((D_MODEL, PROJ_TN), lambda m, n: (0, n)),
        ],
        out_specs=[
            pl.BlockSpec((PROJ_TM, PROJ_TN), lambda m, n: (m, jnp.minimum(n, NB_BF - 1))),
            pl.BlockSpec((PROJ_TM, PROJ_TN), lambda m, n: (m, jnp.maximum(n - NB_BF, 0))),
        ],
        out_shape=[
            jax.ShapeDtypeStruct((TOKENS, N_BF), BF16),
            jax.ShapeDtypeStruct((TOKENS, N_F32), F32),
        ],
        scratch_shapes=[pltpu.VMEM((PROJ_TM, D_MODEL), BF16)],
        compiler_params=_params("arbitrary", "arbitrary"),
        name="in_proj",
    )(x2d, norm_w, mods3, mods3, w_perm)


def _permute_w_in(w):
    cols = [w[:, _IN_OFFSETS[k][0]:_IN_OFFSETS[k][1]] for k in _BF_ORDER + _F32_ORDER]
    used = sum(c.shape[1] for c in cols)
    cols.append(jnp.zeros((D_MODEL, N_BF + N_F32 - used), w.dtype))
    return jnp.concatenate(cols, axis=1).astype(BF16)


def _gla_kernel(gq_ref, gk_ref, gv_ref, gz_ref, sm_ref, wlr_ref, blr_ref, nw_ref, o_ref, la_ref):
    z = _dot_f32(sm_ref[...], wlr_ref[...]) + blr_ref[...]
    la_ref[...] = -_softplus(-z) * (1.0 / GLA_TAU)

    row = lax.broadcasted_iota(jnp.int32, (CHUNK, CHUNK), 0)
    col = lax.broadcasted_iota(jnp.int32, (CHUNK, CHUNK), 1)
    incl = col <= row
    tril = jnp.where(incl, 1.0, 0.0)
    nw = nw_ref[...]

    def body(n, states):
        r = pl.ds(pl.multiple_of(n * CHUNK, CHUNK), CHUNK)
        bc = _dot_f32(tril, la_ref[r, :])
        b_last = bc[CHUNK - 1:CHUNK, :]
        q = gq_ref[r, :].astype(F32) * GLA_DK ** -0.5
        k = gk_ref[r, :].astype(F32)
        qe = (q * jnp.exp(bc)).astype(BF16)
        ke = (k * jnp.exp(-bc)).astype(BF16)
        kt = (k * jnp.exp(b_last - bc)).astype(BF16)
        dec = jnp.exp(b_last)
        v_all = gv_ref[r, :]
        gz = gz_ref[r, :]
        new_states = []
        outs = []
        for h in range(GLA_HEADS):
            ks = slice(h * GLA_DK, (h + 1) * GLA_DK)
            vs = slice(h * GLA_DV, (h + 1) * GLA_DV)
            st = states[h]
            att = jnp.where(incl, _dot_nt(qe[:, ks], ke[:, ks]), 0.0)
            v = v_all[:, vs]
            o = _dot(att.astype(BF16), v) + _dot_nt(qe[:, ks], st.astype(BF16))
            new_states.append(st * dec[:, ks] + _dot_tn(v, kt[:, ks]))
            outs.append(_rms(o, nw) * _silu(gz[:, vs]))
        o_ref[r, :] = jnp.concatenate(outs, axis=-1).astype(BF16)
        return tuple(new_states)

    lax.fori_loop(0, N_CHUNKS, body, tuple(jnp.zeros((GLA_DV, GLA_DK), F32) for _ in range(GLA_HEADS)))


def _gla(proj_bf, proj_f32, wlr_pad, blr, nw):
    return pl.pallas_call(
        _gla_kernel,
        grid=(BATCH,),
        in_specs=[
            pl.BlockSpec((SEQ, 256), lambda b: (b, 12)),
            pl.BlockSpec((SEQ, 256), lambda b: (b, 13)),
            pl.BlockSpec((SEQ, GLA_W), lambda b: (b, 7)),
            pl.BlockSpec((SEQ, GLA_W), lambda b: (b, 0)),
            pl.BlockSpec((SEQ, LANES), lambda b: (b, SMALL_COL // LANES)),
            pl.BlockSpec((LANES, GLA_HEADS * GLA_DK), lambda b: (0, 0)),
            pl.BlockSpec((1, GLA_HEADS * GLA_DK), lambda b: (0, 0)),
            pl.BlockSpec((1, GLA_DV), lambda b: (0, 0)),
        ],
        out_specs=pl.BlockSpec((SEQ, GLA_W), lambda b: (b, 0)),
        out_shape=jax.ShapeDtypeStruct((TOKENS, GLA_W), BF16),
        scratch_shapes=[pltpu.VMEM((SEQ, GLA_HEADS * GLA_DK), F32)],
        compiler_params=_params("arbitrary"),
        name="gla",
    )(proj_bf, proj_bf, proj_bf, proj_f32, proj_f32, wlr_pad, blr, nw)


GDN_RB = 256
GDN_HALO = 16


def _heads(x, h):
    return x[:, h * GDN_D:(h + 1) * GDN_D]


def _gdn_kernel(dq_ref, dk_ref, dv_ref, dz_ref, sm_ref, cw_ref, alog_ref, dtb_ref, nw_ref, o_ref,
                q_s, k_s, vu_s, w_s, qe_s, kt_s, qk_s, cd_s, g_s, beta_s, st_s):
    heads = range(GDN_HEADS)
    row = lax.broadcasted_iota(jnp.int32, (CHUNK, CHUNK), 0)
    col = lax.broadcasted_iota(jnp.int32, (CHUNK, CHUNK), 1)
    incl = col <= row
    strict = col < row
    tril = jnp.where(incl, 1.0, 0.0)
    nw = nw_ref[...]
    cw = cw_ref[...]
    neg_a = -jnp.exp(alog_ref[...])
    dtb = dtb_ref[...]

    def conv(x_ref, base, w):
        cur = x_ref[pl.ds(base, GDN_RB), :].astype(F32)
        prev = x_ref[pl.ds(pl.multiple_of(jnp.maximum(base - GDN_HALO, 0), GDN_HALO), GDN_HALO), :].astype(F32)
        prev = jnp.where(base > 0, prev, 0.0)
        win = jnp.concatenate([prev, cur], axis=0)
        first = GDN_HALO - (CONV_K - 1)
        acc = win[first:first + GDN_RB] * w[0:1]
        for j in range(1, CONV_K):
            acc = acc + win[first + j:first + j + GDN_RB] * w[j:j + 1]
        return _silu(acc)

    def l2n(x, scale):
        parts = []
        for h in heads:
            xh = _heads(x, h)
            parts.append(xh * (lax.rsqrt(jnp.sum(xh * xh, axis=-1, keepdims=True) + EPS) * scale))
        return jnp.concatenate(parts, axis=-1)

    def phase0(i, carry):
        base = pl.multiple_of(i * GDN_RB, GDN_RB)
        rows = pl.ds(base, GDN_RB)
        q_s[rows, :] = l2n(conv(dq_ref, base, cw[:, 0:GDN_W]), GDN_D ** -0.5)
        k_s[rows, :] = l2n(conv(dk_ref, base, cw[:, GDN_W:2 * GDN_W]), 1.0)
        vu_s[rows, :] = conv(dv_ref, base, cw[:, 2 * GDN_W:3 * GDN_W])
        sm = sm_ref[rows, :]
        g_s[rows, :] = neg_a * _softplus(sm + dtb)
        beta_s[rows, :] = _sigmoid(sm)
        return carry

    lax.fori_loop(0, SEQ // GDN_RB, phase0, 0)

    def phase_a(n, carry):
        r = pl.ds(pl.multiple_of(n * CHUNK, CHUNK), CHUNK)
        gcs = _dot_f32(tril, g_s[r, :])
        gcs_t = gcs.T
        beta_all = beta_s[r, :]
        qn = q_s[r, :]
        kn = k_s[r, :]
        v = vu_s[r, :]
        decay, eg, ekt, cd, k_beta, v_beta, kb, pw = [], [], [], [], [], [], [], []
        for h in heads:
            gc_col = gcs[:, DA_LANE + h:DA_LANE + h + 1]
            gc_row = gcs_t[DA_LANE + h:DA_LANE + h + 1, :]
            decay.append(jnp.exp(jnp.where(incl, gc_col - gc_row, -jnp.inf)))
            gcb = jnp.broadcast_to(gc_col, (CHUNK, GDN_D))
            g_last = gcb[CHUNK - 1:CHUNK, :]
            eg.append(jnp.exp(gcb))
            ekt.append(jnp.exp(g_last - gcb))
            cd.append(jnp.broadcast_to(jnp.exp(g_last), (8, GDN_D)))
            b_col = beta_all[:, DB_LANE + h:DB_LANE + h + 1]
            k_beta.append(_heads(kn, h) * b_col)
            v_beta.append(_heads(v, h) * b_col)
            kb.append(_heads(kn, h).astype(BF16))
        for h in heads:
            lower = jnp.where(strict, _dot_nt(k_beta[h].astype(BF16), kb[h]) * decay[h], 0.0)
            pw.append(-lower)
        tq = list(pw)
        pw = [_dot_f32(p, p) for p in pw]
        for _ in range(4):
            both = [_dot_f32(jnp.concatenate([tq[h], pw[h]], axis=0), pw[h]) for h in heads]
            tq = [tq[h] + pw[h] + both[h][0:CHUNK] for h in heads]
            pw = [both[h][CHUNK:2 * CHUNK] for h in heads]
        tq = [tq[h] + pw[h] + _dot_f32(tq[h], pw[h]) for h in heads]
        rhs = [jnp.concatenate([v_beta[h], k_beta[h] * eg[h]], axis=-1) for h in heads]
        sol = [rhs[h] + _dot(tq[h].astype(BF16), rhs[h].astype(BF16)) for h in heads]
        qk = [_dot_nt(_heads(qn, h).astype(BF16), kb[h]) * decay[h] for h in heads]
        vu_s[r, :] = jnp.concatenate([s[:, 0:GDN_D] for s in sol], axis=-1)
        w_s[r, :] = jnp.concatenate([s[:, GDN_D:2 * GDN_D] for s in sol], axis=-1).astype(BF16)
        qe_s[r, :] = jnp.concatenate([_heads(qn, h) * eg[h] for h in heads], axis=-1).astype(BF16)
        kt_s[r, :] = jnp.concatenate([_heads(kn, h) * ekt[h] for h in heads], axis=-1).astype(BF16)
        qk_s[r, :] = jnp.concatenate(qk, axis=-1).astype(BF16)
        cd_s[n] = jnp.concatenate(cd, axis=-1)
        return carry

    lax.fori_loop(0, N_CHUNKS, phase_a, 0)

    st_s[...] = jnp.zeros((GDN_HEADS, GDN_D, GDN_D), F32)

    def phase_b(n, carry):
        r = pl.ds(pl.multiple_of(n * CHUNK, CHUNK), CHUNK)
        u = vu_s[r, :]
        w = w_s[r, :]
        qe = qe_s[r, :]
        kt = kt_s[r, :]
        qk = qk_s[r, :]
        cd = cd_s[n]
        state = [st_s[h] for h in heads]
        sb = [s.astype(BF16) for s in state]
        ws = [_dot(_heads(w, h), sb[h]) for h in heads]
        qs = [_dot(_heads(qe, h), sb[h]) for h in heads]
        v_new = [(_heads(u, h) - ws[h]).astype(BF16) for h in heads]
        o = [qs[h] + _dot(qk[:, h * CHUNK:(h + 1) * CHUNK], v_new[h]) for h in heads]
        for h in heads:
            st_s[h] = state[h] * _heads(cd, h)[0:1] + _dot_tn(_heads(kt, h), v_new[h])
        vu_s[r, :] = jnp.concatenate(o, axis=-1)
        return carry

    lax.fori_loop(0, N_CHUNKS, phase_b, 0)

    def phase_c(i, carry):
        rows = pl.ds(pl.multiple_of(i * GDN_RB, GDN_RB), GDN_RB)
        o = vu_s[rows, :]
        o = jnp.concatenate([_rms(_heads(o, h), nw) for h in heads], axis=-1)
        o_ref[rows, :] = (o * _silu(dz_ref[rows, :])).astype(BF16)
        return carry

    lax.fori_loop(0, SEQ // GDN_RB, phase_c, 0)


def _gdn(proj_bf, proj_f32, conv_w, alog_pad, dtb_pad, nw):
    once = pl.Buffered(1)
    full_f32 = pltpu.VMEM((SEQ, GDN_W), F32)
    full_bf = pltpu.VMEM((SEQ, GDN_W), BF16)
    return pl.pallas_call(
        _gdn_kernel,
        grid=(BATCH,),
        in_specs=[
            pl.BlockSpec((SEQ, GDN_W), lambda b: (b, 8), pipeline_mode=once),
            pl.BlockSpec((SEQ, GDN_W), lambda b: (b, 9), pipeline_mode=once),
            pl.BlockSpec((SEQ, GDN_W), lambda b: (b, 10), pipeline_mode=once),
            pl.BlockSpec((SEQ, GDN_W), lambda b: (b, 1), pipeline_mode=once),
            pl.BlockSpec((SEQ, LANES), lambda b: (b, SMALL_COL // LANES), pipeline_mode=once),
            pl.BlockSpec((CONV_K, 3 * GDN_W), lambda b: (0, 0)),
            pl.BlockSpec((1, LANES), lambda b: (0, 0)),
            pl.BlockSpec((1, LANES), lambda b: (0, 0)),
            pl.BlockSpec((1, GDN_D), lambda b: (0, 0)),
        ],
        out_specs=pl.BlockSpec((SEQ, GDN_W), lambda b: (b, 0)),
        out_shape=jax.ShapeDtypeStruct((TOKENS, GDN_W), BF16),
        scratch_shapes=[full_f32, full_f32, full_f32,
                        full_bf, full_bf, full_bf,
                        pltpu.VMEM((SEQ, GDN_HEADS * CHUNK), BF16),
                        pltpu.VMEM((N_CHUNKS, 8, GDN_W), F32),
                        pltpu.VMEM((SEQ, LANES), F32), pltpu.VMEM((SEQ, LANES), F32),
                        pltpu.VMEM((GDN_HEADS, GDN_D, GDN_D), F32)],
        compiler_params=_params("arbitrary"),
        name="gdn",
    )(proj_bf, proj_bf, proj_bf, proj_f32, proj_f32, conv_w, alog_pad, dtb_pad, nw)


PREP_TS = 512
ATT_TQ = 256


def _diff_prep_kernel(aq_ref, ak_ref, cos_ref, sin_ref, qw_ref, kw_ref, q_ref, k_ref):
    cos2 = cos_ref[0]
    sin2 = sin_ref[0]
    for src, w_ref, dst, scale in ((aq_ref, qw_ref, q_ref, DIFF_D ** -0.5 * LOG2E), (ak_ref, kw_ref, k_ref, 1.0)):
        w = w_ref[...]
        for g in range(2 * DIFF_HEADS):
            cols = slice(g * DIFF_D, (g + 1) * DIFF_D)
            y = _rms(src[:, cols].astype(F32), w)
            y = y * cos2 + pltpu.roll(y, DIFF_D // 2, 1) * sin2
            dst[:, cols] = (y * scale).astype(BF16)


def _diff_prep(proj_bf, cos2, sin2, qw, kw):
    per_batch = SEQ // PREP_TS
    width = 2 * DIFF_HEADS * DIFF_D
    tab_spec = pl.BlockSpec((1, PREP_TS, DIFF_D), lambda i: (i // per_batch, i % per_batch, 0))
    out = jax.ShapeDtypeStruct((TOKENS, width), BF16)
    return pl.pallas_call(
        _diff_prep_kernel,
        grid=(TOKENS // PREP_TS,),
        in_specs=[
            pl.BlockSpec((PREP_TS, width), lambda i: (i, 0)),
            pl.BlockSpec((PREP_TS, width), lambda i: (i, 1)),
            tab_spec, tab_spec,
            pl.BlockSpec((1, DIFF_D), lambda i: (0, 0)),
            pl.BlockSpec((1, DIFF_D), lambda i: (0, 0)),
        ],
        out_specs=[pl.BlockSpec((PREP_TS, width), lambda i: (i, 0))] * 2,
        out_shape=[out, out],
        compiler_params=_params("arbitrary"),
        name="diff_prep",
    )(proj_bf, proj_bf, cos2, sin2, qw, kw)


def _diff_attn_kernel(q1_ref, q2_ref, k1_ref, k2_ref, v_ref, az_ref, lam_ref, nw_ref, o_ref, *, lam_init):
    lv = lam_ref[...]
    lam = (jnp.exp(jnp.sum(lv[0:1] * lv[1:2], axis=-1, keepdims=True))
           - jnp.exp(jnp.sum(lv[2:3] * lv[3:4], axis=-1, keepdims=True)) + lam_init)
    nw = nw_ref[...]
    row = lax.broadcasted_iota(jnp.int32, (ATT_TQ, ATT_TQ), 0)
    col = lax.broadcasted_iota(jnp.int32, (ATT_TQ, ATT_TQ), 1)
    causal = col <= row
    q_refs = (q1_ref, q2_ref)
    k_refs = (k1_ref, k2_ref)
    for i in range(SEQ // ATT_TQ):
        start = i * ATT_TQ
        rows = slice(start, start + ATT_TQ)
        p_past, p_diag, inv_l = [], [], []
        for m in range(2):
            q = q_refs[m][rows, :]
            s_diag = jnp.where(causal, _dot_nt(q, k_refs[m][rows, :]), -jnp.inf)
            mx = jnp.max(s_diag, axis=-1, keepdims=True)
            if i > 0:
                s_past = _dot_nt(q, k_refs[m][0:start, :])
                mx = jnp.maximum(mx, jnp.max(s_past, axis=-1, keepdims=True))
                p_past.append(jnp.exp2(s_past - mx))
            p_diag.append(jnp.exp2(s_diag - mx))
            l = jnp.sum(p_diag[m], axis=-1, keepdims=True)
            if i > 0:
                l = l + jnp.sum(p_past[m], axis=-1, keepdims=True)
            inv_l.append(1.0 / l)
        c1 = inv_l[0]
        c2 = lam * inv_l[1]
        o = _dot((p_diag[0] * c1 - p_diag[1] * c2).astype(BF16), v_ref[rows, :])
        if i > 0:
            o = o + _dot((p_past[0] * c1 - p_past[1] * c2).astype(BF16), v_ref[0:start, :])
        o = _rms(o, nw) * (1.0 - lam_init)
        o_ref[rows, :] = (o * _silu(az_ref[rows, :])).astype(BF16)


def _diff_attn(q_d, k_d, proj_bf, proj_f32, lam, nw, lam_init):
    return pl.pallas_call(
        functools.partial(_diff_attn_kernel, lam_init=lam_init),
        grid=(BATCH, DIFF_HEADS),
        in_specs=[
            pl.BlockSpec((SEQ, DIFF_D), lambda b, h: (b, 2 * h)),
            pl.BlockSpec((SEQ, DIFF_D), lambda b, h: (b, 2 * h + 1)),
            pl.BlockSpec((SEQ, DIFF_D), lambda b, h: (b, 2 * h)),
            pl.BlockSpec((SEQ, DIFF_D), lambda b, h: (b, 2 * h + 1)),
            pl.BlockSpec((SEQ, DIFF_DV), lambda b, h: (b, 8 + h)),
            pl.BlockSpec((SEQ, DIFF_DV), lambda b, h: (b, 4 + h)),
            pl.BlockSpec((4, DIFF_D), lambda b, h: (0, 0)),
            pl.BlockSpec((1, DIFF_DV), lambda b, h: (0, 0)),
        ],
        out_specs=pl.BlockSpec((SEQ, DIFF_DV), lambda b, h: (b, h)),
        out_shape=jax.ShapeDtypeStruct((TOKENS, DIFF_W), BF16),
        compiler_params=_params("arbitrary", "arbitrary"),
        name="diff_attn",
    )(q_d, q_d, k_d, k_d, proj_bf, proj_f32, lam, nw)


OUT_TM = 512


def _outproj_kernel(og_ref, od_ref, oa_ref, w_ref, x_ref, gate_ref, o_ref):
    y = _dot(og_ref[...], w_ref[0:GLA_W, :])
    y = y + _dot(od_ref[...], w_ref[GLA_W:GLA_W + GDN_W, :])
    y = y + _dot(oa_ref[...], w_ref[GLA_W + GDN_W:, :])
    o_ref[...] = x_ref[...] + gate_ref[0] * y


def _out_proj(o_gla, o_gdn, o_diff, w_out_bf, x2d, mods3):
    per_batch = SEQ // OUT_TM
    return pl.pallas_call(
        _outproj_kernel,
        grid=(TOKENS // OUT_TM,),
        in_specs=[
            pl.BlockSpec((OUT_TM, GLA_W), lambda m: (m, 0)),
            pl.BlockSpec((OUT_TM, GDN_W), lambda m: (m, 0)),
            pl.BlockSpec((OUT_TM, DIFF_W), lambda m: (m, 0)),
            pl.BlockSpec((D_MODEL, D_MODEL), lambda m: (0, 0)),
            pl.BlockSpec((OUT_TM, D_MODEL), lambda m: (m, 0)),
            pl.BlockSpec((1, 1, D_MODEL), lambda m: (m // per_batch, 0, 2)),
        ],
        out_specs=pl.BlockSpec((OUT_TM, D_MODEL), lambda m: (m, 0)),
        out_shape=jax.ShapeDtypeStruct((TOKENS, D_MODEL), F32),
        compiler_params=_params("arbitrary"),
        name="out_proj",
    )(o_gla, o_gdn, o_diff, w_out_bf, x2d, mods3)


def _lane_pad(vec, lane):
    return jnp.zeros((1, LANES), F32).at[0, lane:lane + vec.shape[0]].set(vec.astype(F32))


def kernel(x, c, positions, norm_w, w_ada, b_ada, w_in, gla_w_lr, gla_b_lr, gla_norm_w, gdn_conv_w, gdn_a_log,
           gdn_dt_bias, gdn_norm_w, diff_q_norm_w, diff_k_norm_w, diff_lambda, diff_norm_w, w_out):
    c_pad = jnp.zeros((8, D_MODEL), F32).at[:BATCH].set(c)
    mods = _ada_mod(c_pad, w_ada, b_ada)
    cos2, sin2 = _rope_tables(positions)
    x2d = x.reshape(TOKENS, D_MODEL)
    for l in range(DEPTH):
        mods3 = mods[l].reshape(8, 1, 3 * D_MODEL)
        proj_bf, proj_f32 = _in_proj(x2d, norm_w[l].reshape(1, D_MODEL), mods3, _permute_w_in(w_in[l]))
        wlr_pad = jnp.zeros((LANES, GLA_HEADS * GLA_DK), F32).at[GLR_LANE:GLR_LANE + GLA_RANK].set(gla_w_lr[l])
        o_gla = _gla(proj_bf, proj_f32, wlr_pad, gla_b_lr[l].reshape(1, -1), gla_norm_w[l].reshape(1, -1))
        o_gdn = _gdn(proj_bf, proj_f32, gdn_conv_w[l], _lane_pad(gdn_a_log[l], DA_LANE),
                     _lane_pad(gdn_dt_bias[l], DA_LANE), gdn_norm_w[l].reshape(1, -1))
        q_d, k_d = _diff_prep(proj_bf, cos2, sin2, diff_q_norm_w[l].reshape(1, -1), diff_k_norm_w[l].reshape(1, -1))
        lam_init = 0.8 - 0.6 * math.exp(-0.3 * l)
        o_diff = _diff_attn(q_d, k_d, proj_bf, proj_f32, diff_lambda[l], diff_norm_w[l].reshape(1, -1), lam_init)
        x2d = _out_proj(o_gla, o_gdn, o_diff, w_out[l].astype(BF16), x2d, mods3)
    return x2d.reshape(BATCH, SEQ, D_MODEL)
```

```python
import functools
import math

import jax
import jax.numpy as jnp
from jax import lax
from jax.experimental import pallas as pl
from jax.experimental.pallas import tpu as pltpu

F32 = jnp.float32
BF16 = jnp.bfloat16

D_MODEL = 2048
BATCH = 4
SEQ = 2048
DEPTH = 2
TOKENS = BATCH * SEQ

GLA_HEADS = 4
GLA_DK = 64
GLA_DV = 128
GLA_W = GLA_HEADS * GLA_DV
GLA_RANK = 16
GLA_TAU = 16.0
GDN_HEADS = 4
GDN_D = 128
GDN_W = GDN_HEADS * GDN_D
CONV_K = 4
DIFF_HEADS = 4
DIFF_D = 128
DIFF_DV = 256
DIFF_W = DIFF_HEADS * DIFF_DV
CHUNK = 64
N_CHUNKS = SEQ // CHUNK
ROPE_THETA = 10000.0
EPS = 1e-6
LOG2E = math.log2(math.e)

LANES = 128

_IN_SPLITS = (
    ("gq", GLA_HEADS * GLA_DK), ("gk", GLA_HEADS * GLA_DK), ("gv", GLA_W), ("glr", GLA_RANK), ("gz", GLA_W),
    ("dq", GDN_W), ("dk", GDN_W), ("dv", GDN_W), ("da", GDN_HEADS), ("db", GDN_HEADS), ("dz", GDN_W),
    ("aq", DIFF_HEADS * 2 * DIFF_D), ("ak", DIFF_HEADS * 2 * DIFF_D), ("av", DIFF_W), ("az", DIFF_W),
)
_IN_OFFSETS = {}
_off = 0
for _name, _width in _IN_SPLITS:
    _IN_OFFSETS[_name] = (_off, _off + _width)
    _off += _width
D_IN = _off

_BF_ORDER = ("aq", "ak", "av", "gq", "gk", "gv", "dq", "dk", "dv")
_F32_ORDER = ("gz", "dz", "az")
N_BF = 5632
N_F32 = 2560
PROJ_TN = 512
NB_BF = N_BF // PROJ_TN
NB_F32 = N_F32 // PROJ_TN
GLR_SRC = (_IN_OFFSETS["glr"][0] // LANES) * LANES
DAB_SRC = (_IN_OFFSETS["da"][0] // LANES) * LANES
GLR_COL = 2048
DAB_COL = GLR_COL + LANES
GLR_LANE = _IN_OFFSETS["glr"][0] - GLR_SRC
DA_LANE = _IN_OFFSETS["da"][0] - DAB_SRC
DB_LANE = _IN_OFFSETS["db"][0] - DAB_SRC
assert GLR_LANE + GLA_RANK <= LANES and DB_LANE + GDN_HEADS <= LANES

VMEM_LIMIT = 48 * 1024 * 1024


def _sigmoid(x):
    return 1.0 / (1.0 + jnp.exp(-x))


def _silu(x):
    return x * _sigmoid(x)


def _softplus(x):
    return jnp.maximum(x, 0.0) + jnp.log1p(jnp.exp(-jnp.abs(x)))


def _dot(a, b):
    return jnp.dot(a, b, preferred_element_type=F32)


def _dot_nt(a, b):
    return lax.dot_general(a, b, (((1,), (1,)), ((), ())), preferred_element_type=F32)


def _dot_tn(a, b):
    return lax.dot_general(a, b, (((0,), (0,)), ((), ())), preferred_element_type=F32)


def _dot_f32(a, b):
    return jnp.dot(a, b, precision=lax.Precision.HIGHEST, preferred_element_type=F32)


def _rms(x, w):
    return x * lax.rsqrt(jnp.mean(x * x, axis=-1, keepdims=True) + EPS) * w


def _params(*semantics, vmem=VMEM_LIMIT):
    return pltpu.CompilerParams(dimension_semantics=semantics, vmem_limit_bytes=vmem)


ADA_TN = 1024


def _ada_kernel(c_ref, w_ref, b_ref, o_ref):
    c_act = _silu(c_ref[...])
    o_ref[0] = _dot(c_act.astype(BF16), w_ref[0].astype(BF16)) + b_ref[0]


def _ada_mod(c_pad, w_ada, b_ada):
    n3 = 3 * D_MODEL
    return pl.pallas_call(
        _ada_kernel,
        grid=(DEPTH, n3 // ADA_TN),
        in_specs=[
            pl.BlockSpec((8, D_MODEL), lambda l, n: (0, 0)),
            pl.BlockSpec((1, D_MODEL, ADA_TN), lambda l, n: (l, 0, n)),
            pl.BlockSpec((1, 1, ADA_TN), lambda l, n: (l, 0, n)),
        ],
        out_specs=pl.BlockSpec((1, 8, ADA_TN), lambda l, n: (l, 0, n)),
        out_shape=jax.ShapeDtypeStruct((DEPTH, 8, n3), F32),
        compiler_params=_params("arbitrary", "arbitrary"),
        name="ada_mod",
    )(c_pad, w_ada, b_ada.reshape(DEPTH, 1, n3))


def _rope_kernel(pos_ref, freq_ref, sign_ref, cos_ref, sin_ref):
    ang = pos_ref[0].astype(F32) * freq_ref[...]
    cos_ref[0] = jnp.cos(ang)
    sin_ref[0] = jnp.sin(ang) * sign_ref[...]


def _rope_tables(positions):
    half = DIFF_D // 2
    inv_freq = ROPE_THETA ** (-jnp.arange(half, dtype=F32) / half)
    freq2 = jnp.concatenate([inv_freq, inv_freq]).reshape(1, DIFF_D)
    sign = jnp.concatenate([-jnp.ones((half,), F32), jnp.ones((half,), F32)]).reshape(1, DIFF_D)
    tab = jax.ShapeDtypeStruct((BATCH, SEQ, DIFF_D), F32)
    return pl.pallas_call(
        _rope_kernel,
        grid=(BATCH,),
        in_specs=[
            pl.BlockSpec((1, SEQ, 1), lambda b: (b, 0, 0)),
            pl.BlockSpec((1, DIFF_D), lambda b: (0, 0)),
            pl.BlockSpec((1, DIFF_D), lambda b: (0, 0)),
        ],
        out_specs=[pl.BlockSpec((1, SEQ, DIFF_D), lambda b: (b, 0, 0))] * 2,
        out_shape=[tab, tab],
        compiler_params=_params("arbitrary"),
        name="rope_tables",
    )(positions.reshape(BATCH, SEQ, 1), freq2, sign)


PROJ_TM = 1024
PROJ_ROWS = 256


def _inproj_kernel(x_ref, nw_ref, shift_ref, scale_ref, w_ref, ob_ref, of_ref, h_ref):
    n = pl.program_id(1)

    @pl.when(n == 0)
    def _():
        nw = nw_ref[0]
        scale1 = 1.0 + scale_ref[0, 0]
        shift = shift_ref[0, 0]
        for i in range(PROJ_TM // PROJ_ROWS):
            rows = slice(i * PROJ_ROWS, (i + 1) * PROJ_ROWS)
            h = _rms(x_ref[rows, :], nw) * scale1 + shift
            h_ref[rows, :] = h.astype(BF16)

    acc = _dot(h_ref[...], w_ref[0])

    @pl.when(n < NB_BF)
    def _():
        ob_ref[...] = acc.astype(BF16)

    @pl.when(n >= NB_BF)
    def _():
        of_ref[...] = acc


def _in_proj(l, x2d, norm_w3, mods4, w_perm):
    per_batch = SEQ // PROJ_TM
    return pl.pallas_call(
        _inproj_kernel,
        grid=(TOKENS // PROJ_TM, NB_BF + NB_F32),
        in_specs=[
            pl.BlockSpec((PROJ_TM, D_MODEL), lambda m, n: (m, 0)),
            pl.BlockSpec((1, 1, D_MODEL), lambda m, n: (l, 0, 0)),
            pl.BlockSpec((1, 1, 1, D_MODEL), lambda m, n: (l, m // per_batch, 0, 0)),
            pl.BlockSpec((1, 1, 1, D_MODEL), lambda m, n: (l, m // per_batch, 0, 1)),
            pl.BlockSpec((1, D_MODEL, PROJ_TN), lambda m, n: (l, 0, n)),
        ],
        out_specs=[
            pl.BlockSpec((PROJ_TM, PROJ_TN), lambda m, n: (m, jnp.minimum(n, NB_BF - 1))),
            pl.BlockSpec((PROJ_TM, PROJ_TN), lambda m, n: (m, jnp.maximum(n - NB_BF, 0))),
        ],
        out_shape=[
            jax.ShapeDtypeStruct((TOKENS, N_BF), BF16),
            jax.ShapeDtypeStruct((TOKENS, N_F32), F32),
        ],
        scratch_shapes=[pltpu.VMEM((PROJ_TM, D_MODEL), BF16)],
        compiler_params=_params("arbitrary", "arbitrary"),
        name="in_proj",
    )(x2d, norm_w3, mods4, mods4, w_perm)


WPREP_ROWS = 256


def _wprep_kernel(w_ref, o_ref):
    dst = 0
    for name in _BF_ORDER + _F32_ORDER:
        lo, hi = _IN_OFFSETS[name]
        lo_al = (lo // LANES) * LANES
        hi_al = min(-(-hi // LANES) * LANES, D_IN)
        window = w_ref[0, :, lo_al:hi_al]
        o_ref[0, :, dst:dst + hi - lo] = window[:, lo - lo_al:hi - lo_al].astype(BF16)
        dst += hi - lo
    assert dst == N_BF + GLR_COL
    o_ref[0, :, dst:dst + LANES] = w_ref[0, :, GLR_SRC:GLR_SRC + LANES].astype(BF16)
    o_ref[0, :, dst + LANES:dst + 2 * LANES] = w_ref[0, :, DAB_SRC:DAB_SRC + LANES].astype(BF16)
    o_ref[0, :, dst + 2 * LANES:] = jnp.zeros((WPREP_ROWS, N_F32 - DAB_COL - LANES), BF16)


def _prep_w_in(w_in):
    return pl.pallas_call(
        _wprep_kernel,
        grid=(DEPTH, D_MODEL // WPREP_ROWS),
        in_specs=[pl.BlockSpec((1, WPREP_ROWS, D_IN), lambda l, i: (l, i, 0))],
        out_specs=pl.BlockSpec((1, WPREP_ROWS, N_BF + N_F32), lambda l, i: (l, i, 0)),
        out_shape=jax.ShapeDtypeStruct((DEPTH, D_MODEL, N_BF + N_F32), BF16),
        compiler_params=_params("arbitrary", "arbitrary"),
        name="w_in_layout",
    )(w_in)


GLA_GROUP = 4


def _gla_kernel(gq_ref, gk_ref, gv_ref, gz_ref, sm_ref, wlr_ref, blr_ref, nw_ref, o_ref, la_ref):
    z = _dot_f32(sm_ref[...], wlr_ref[0]) + blr_ref[0]
    la_ref[...] = -_softplus(-z) * (1.0 / GLA_TAU)

    row = lax.broadcasted_iota(jnp.int32, (CHUNK, CHUNK), 0)
    col = lax.broadcasted_iota(jnp.int32, (CHUNK, CHUNK), 1)
    incl = col <= row
    tril = jnp.where(incl, 1.0, 0.0)
    nw = nw_ref[0]
    heads = range(GLA_HEADS)
    group = range(GLA_GROUP)

    def kcols(x, h):
        return x[:, h * GLA_DK:(h + 1) * GLA_DK]

    def vcols(x, h):
        return x[:, h * GLA_DV:(h + 1) * GLA_DV]

    def body(gi, states):
        rs = [pl.ds(pl.multiple_of((gi * GLA_GROUP + c) * CHUNK, CHUNK), CHUNK) for c in group]
        bc = [_dot_f32(tril, la_ref[r, :]) for r in rs]
        qe, ke, kt, dec, v = [], [], [], [], []
        for c in group:
            b_last = bc[c][CHUNK - 1:CHUNK, :]
            q = gq_ref[rs[c], :].astype(F32) * GLA_DK ** -0.5
            k = gk_ref[rs[c], :].astype(F32)
            qe.append((q * jnp.exp(bc[c])).astype(BF16))
            ke.append((k * jnp.exp(-bc[c])).astype(BF16))
            kt.append((k * jnp.exp(b_last - bc[c])).astype(BF16))
            dec.append(jnp.exp(b_last))
            v.append(gv_ref[rs[c], :])
        att = [[jnp.where(incl, _dot_nt(kcols(qe[c], h), kcols(ke[c], h)), 0.0).astype(BF16) for h in heads]
               for c in group]
        o_intra = [[_dot(att[c][h], vcols(v[c], h)) for h in heads] for c in group]
        d_state = [[_dot_tn(vcols(v[c], h), kcols(kt[c], h)) for h in heads] for c in group]
        st = list(states)
        for c in group:
            gz = gz_ref[rs[c], :]
            outs = []
            for h in heads:
                o = o_intra[c][h] + _dot_nt(kcols(qe[c], h), st[h].astype(BF16))
                st[h] = st[h] * kcols(dec[c], h) + d_state[c][h]
                outs.append(_rms(o, nw) * _silu(vcols(gz, h)))
            o_ref[rs[c], :] = jnp.concatenate(outs, axis=-1).astype(BF16)
        return tuple(st)

    lax.fori_loop(0, N_CHUNKS // GLA_GROUP, body,
                  tuple(jnp.zeros((GLA_DV, GLA_DK), F32) for _ in heads))


def _gla(l, proj_bf, proj_f32, wlr_pad, blr, nw):
    return pl.pallas_call(
        _gla_kernel,
        grid=(BATCH,),
        in_specs=[
            pl.BlockSpec((SEQ, 256), lambda b: (b, 12)),
            pl.BlockSpec((SEQ, 256), lambda b: (b, 13)),
            pl.BlockSpec((SEQ, GLA_W), lambda b: (b, 7)),
            pl.BlockSpec((SEQ, GLA_W), lambda b: (b, 0)),
            pl.BlockSpec((SEQ, LANES), lambda b: (b, GLR_COL // LANES)),
            pl.BlockSpec((1, LANES, GLA_HEADS * GLA_DK), lambda b: (l, 0, 0)),
            pl.BlockSpec((1, 1, GLA_HEADS * GLA_DK), lambda b: (l, 0, 0)),
            pl.BlockSpec((1, 1, GLA_DV), lambda b: (l, 0, 0)),
        ],
        out_specs=pl.BlockSpec((SEQ, GLA_W), lambda b: (b, 0)),
        out_shape=jax.ShapeDtypeStruct((TOKENS, GLA_W), BF16),
        scratch_shapes=[pltpu.VMEM((SEQ, GLA_HEADS * GLA_DK), F32)],
        compiler_params=_params("arbitrary"),
        name="gla",
    )(proj_bf, proj_bf, proj_bf, proj_f32, proj_f32, wlr_pad, blr, nw)


GDN_RB = 256
GDN_HALO = 16


def _heads(x, h):
    return x[:, h * GDN_D:(h + 1) * GDN_D]


def _gdn_kernel(dq_ref, dk_ref, dv_ref, dz_ref, sm_ref, cw_ref, alog_ref, dtb_ref, nw_ref, o_ref,
                q_s, k_s, vu_s, w_s, qe_s, kt_s, qk_s, cd_s, g_s, beta_s, st_s):
    heads = range(GDN_HEADS)
    row = lax.broadcasted_iota(jnp.int32, (CHUNK, CHUNK), 0)
    col = lax.broadcasted_iota(jnp.int32, (CHUNK, CHUNK), 1)
    incl = col <= row
    strict = col < row
    tril = jnp.where(incl, 1.0, 0.0)
    nw = nw_ref[0]
    cw = cw_ref[0]
    neg_a = -jnp.exp(alog_ref[0])
    dtb = dtb_ref[0]

    def conv(x_ref, base, w):
        cur = x_ref[pl.ds(base, GDN_RB), :].astype(F32)
        prev = x_ref[pl.ds(pl.multiple_of(jnp.maximum(base - GDN_HALO, 0), GDN_HALO), GDN_HALO), :].astype(F32)
        prev = jnp.where(base > 0, prev, 0.0)
        win = jnp.concatenate([prev, cur], axis=0)
        first = GDN_HALO - (CONV_K - 1)
        acc = win[first:first + GDN_RB] * w[0:1]
        for j in range(1, CONV_K):
            acc = acc + win[first + j:first + j + GDN_RB] * w[j:j + 1]
        return _silu(acc)

    def l2n(x, scale):
        parts = []
        for h in heads:
            xh = _heads(x, h)
            parts.append(xh * (lax.rsqrt(jnp.sum(xh * xh, axis=-1, keepdims=True) + EPS) * scale))
        return jnp.concatenate(parts, axis=-1)

    def phase0(i, carry):
        base = pl.multiple_of(i * GDN_RB, GDN_RB)
        rows = pl.ds(base, GDN_RB)
        q_s[rows, :] = l2n(conv(dq_ref, base, cw[:, 0:GDN_W]), GDN_D ** -0.5)
        k_s[rows, :] = l2n(conv(dk_ref, base, cw[:, GDN_W:2 * GDN_W]), 1.0)
        vu_s[rows, :] = conv(dv_ref, base, cw[:, 2 * GDN_W:3 * GDN_W])
        sm = sm_ref[rows, :]
        g_s[rows, :] = neg_a * _softplus(sm + dtb)
        beta_s[rows, :] = _sigmoid(sm)
        return carry

    lax.fori_loop(0, SEQ // GDN_RB, phase0, 0)

    def phase_a(n, carry):
        r = pl.ds(pl.multiple_of(n * CHUNK, CHUNK), CHUNK)
        gcs = _dot_f32(tril, g_s[r, :])
        gcs_t = gcs.T
        beta_all = beta_s[r, :]
        qn = q_s[r, :]
        kn = k_s[r, :]
        v = vu_s[r, :]
        decay, eg, ekt, cd, k_beta, v_beta, kb, pw = [], [], [], [], [], [], [], []
        for h in heads:
            gc_col = gcs[:, DA_LANE + h:DA_LANE + h + 1]
            gc_row = gcs_t[DA_LANE + h:DA_LANE + h + 1, :]
            decay.append(jnp.exp(jnp.where(incl, gc_col - gc_row, -jnp.inf)))
            gcb = jnp.broadcast_to(gc_col, (CHUNK, GDN_D))
            g_last = gcb[CHUNK - 1:CHUNK, :]
            eg.append(jnp.exp(gcb))
            ekt.append(jnp.exp(g_last - gcb))
            cd.append(jnp.broadcast_to(jnp.exp(g_last), (8, GDN_D)))
            b_col = beta_all[:, DB_LANE + h:DB_LANE + h + 1]
            k_beta.append(_heads(kn, h) * b_col)
            v_beta.append(_heads(v, h) * b_col)
            kb.append(_heads(kn, h).astype(BF16))
        for h in heads:
            lower = jnp.where(strict, _dot_nt(k_beta[h].astype(BF16), kb[h]) * decay[h], 0.0)
            pw.append(-lower)
        tq = list(pw)
        pw = [_dot_f32(p, p) for p in pw]
        for _ in range(4):
            both = [_dot_f32(jnp.concatenate([tq[h], pw[h]], axis=0), pw[h]) for h in heads]
            tq = [tq[h] + pw[h] + both[h][0:CHUNK] for h in heads]
            pw = [both[h][CHUNK:2 * CHUNK] for h in heads]
        tq = [tq[h] + pw[h] + _dot_f32(tq[h], pw[h]) for h in heads]
        rhs = [jnp.concatenate([v_beta[h], k_beta[h] * eg[h]], axis=-1) for h in heads]
        sol = [rhs[h] + _dot(tq[h].astype(BF16), rhs[h].astype(BF16)) for h in heads]
        qk = [_dot_nt(_heads(qn, h).astype(BF16), kb[h]) * decay[h] for h in heads]
        vu_s[r, :] = jnp.concatenate([s[:, 0:GDN_D] for s in sol], axis=-1)
        w_s[r, :] = jnp.concatenate([s[:, GDN_D:2 * GDN_D] for s in sol], axis=-1).astype(BF16)
        qe_s[r, :] = jnp.concatenate([_heads(qn, h) * eg[h] for h in heads], axis=-1).astype(BF16)
        kt_s[r, :] = jnp.concatenate([_heads(kn, h) * ekt[h] for h in heads], axis=-1).astype(BF16)
        qk_s[r, :] = jnp.concatenate(qk, axis=-1).astype(BF16)
        cd_s[n] = jnp.concatenate(cd, axis=-1)
        return carry

    lax.fori_loop(0, N_CHUNKS, phase_a, 0)

    st_s[...] = jnp.zeros((GDN_HEADS, GDN_D, GDN_D), F32)

    def phase_b(n, carry):
        r = pl.ds(pl.multiple_of(n * CHUNK, CHUNK), CHUNK)
        u = vu_s[r, :]
        w = w_s[r, :]
        qe = qe_s[r, :]
        kt = kt_s[r, :]
        qk = qk_s[r, :]
        cd = cd_s[n]
        state = [st_s[h] for h in heads]
        sb = [s.astype(BF16) for s in state]
        ws = [_dot(_heads(w, h), sb[h]) for h in heads]
        qs = [_dot(_heads(qe, h), sb[h]) for h in heads]
        v_new = [(_heads(u, h) - ws[h]).astype(BF16) for h in heads]
        o = [qs[h] + _dot(qk[:, h * CHUNK:(h + 1) * CHUNK], v_new[h]) for h in heads]
        for h in heads:
            st_s[h] = state[h] * _heads(cd, h)[0:1] + _dot_tn(_heads(kt, h), v_new[h])
        vu_s[r, :] = jnp.concatenate(o, axis=-1)
        return carry

    lax.fori_loop(0, N_CHUNKS, phase_b, 0)

    def phase_c(i, carry):
        rows = pl.ds(pl.multiple_of(i * GDN_RB, GDN_RB), GDN_RB)
        o = vu_s[rows, :]
        o = jnp.concatenate([_rms(_heads(o, h), nw) for h in heads], axis=-1)
        o_ref[rows, :] = (o * _silu(dz_ref[rows, :])).astype(BF16)
        return carry

    lax.fori_loop(0, SEQ // GDN_RB, phase_c, 0)


def _gdn(l, proj_bf, proj_f32, conv_w, alog_pad, dtb_pad, nw):
    once = pl.Buffered(1)
    full_f32 = pltpu.VMEM((SEQ, GDN_W), F32)
    full_bf = pltpu.VMEM((SEQ, GDN_W), BF16)
    return pl.pallas_call(
        _gdn_kernel,
        grid=(BATCH,),
        in_specs=[
            pl.BlockSpec((SEQ, GDN_W), lambda b: (b, 8), pipeline_mode=once),
            pl.BlockSpec((SEQ, GDN_W), lambda b: (b, 9), pipeline_mode=once),
            pl.BlockSpec((SEQ, GDN_W), lambda b: (b, 10), pipeline_mode=once),
            pl.BlockSpec((SEQ, GDN_W), lambda b: (b, 1), pipeline_mode=once),
            pl.BlockSpec((SEQ, LANES), lambda b: (b, DAB_COL // LANES), pipeline_mode=once),
            pl.BlockSpec((1, CONV_K, 3 * GDN_W), lambda b: (l, 0, 0)),
            pl.BlockSpec((1, 1, LANES), lambda b: (l, 0, 0)),
            pl.BlockSpec((1, 1, LANES), lambda b: (l, 0, 0)),
            pl.BlockSpec((1, 1, GDN_D), lambda b: (l, 0, 0)),
        ],
        out_specs=pl.BlockSpec((SEQ, GDN_W), lambda b: (b, 0)),
        out_shape=jax.ShapeDtypeStruct((TOKENS, GDN_W), BF16),
        scratch_shapes=[full_f32, full_f32, full_f32,
                        full_bf, full_bf, full_bf,
                        pltpu.VMEM((SEQ, GDN_HEADS * CHUNK), BF16),
                        pltpu.VMEM((N_CHUNKS, 8, GDN_W), F32),
                        pltpu.VMEM((SEQ, LANES), F32), pltpu.VMEM((SEQ, LANES), F32),
                        pltpu.VMEM((GDN_HEADS, GDN_D, GDN_D), F32)],
        compiler_params=_params("arbitrary"),
        name="gdn",
    )(proj_bf, proj_bf, proj_bf, proj_f32, proj_f32, conv_w, alog_pad, dtb_pad, nw)


PREP_TS = 512
ATT_TQ = 256


def _diff_prep_kernel(aq_ref, ak_ref, cos_ref, sin_ref, qw_ref, kw_ref, q_ref, k_ref):
    cos2 = cos_ref[0]
    sin2 = sin_ref[0]
    for src, w_ref, dst, scale in ((aq_ref, qw_ref, q_ref, DIFF_D ** -0.5 * LOG2E), (ak_ref, kw_ref, k_ref, 1.0)):
        w = w_ref[0]
        for g in range(2 * DIFF_HEADS):
            cols = slice(g * DIFF_D, (g + 1) * DIFF_D)
            y = _rms(src[:, cols].astype(F32), w)
            y = y * cos2 + pltpu.roll(y, DIFF_D // 2, 1) * sin2
            dst[:, cols] = (y * scale).astype(BF16)


def _diff_prep(l, proj_bf, cos2, sin2, qw, kw):
    per_batch = SEQ // PREP_TS
    width = 2 * DIFF_HEADS * DIFF_D
    tab_spec = pl.BlockSpec((1, PREP_TS, DIFF_D), lambda i: (i // per_batch, i % per_batch, 0))
    out = jax.ShapeDtypeStruct((TOKENS, width), BF16)
    return pl.pallas_call(
        _diff_prep_kernel,
        grid=(TOKENS // PREP_TS,),
        in_specs=[
            pl.BlockSpec((PREP_TS, width), lambda i: (i, 0)),
            pl.BlockSpec((PREP_TS, width), lambda i: (i, 1)),
            tab_spec, tab_spec,
            pl.BlockSpec((1, 1, DIFF_D), lambda i: (l, 0, 0)),
            pl.BlockSpec((1, 1, DIFF_D), lambda i: (l, 0, 0)),
        ],
        out_specs=[pl.BlockSpec((PREP_TS, width), lambda i: (i, 0))] * 2,
        out_shape=[out, out],
        compiler_params=_params("arbitrary"),
        name="diff_prep",
    )(proj_bf, proj_bf, cos2, sin2, qw, kw)


def _diff_attn_kernel(q1_ref, q2_ref, k1_ref, k2_ref, v_ref, az_ref, lam_ref, nw_ref, o_ref, *, lam_init):
    lv = lam_ref[0]
    lam = (jnp.exp(jnp.sum(lv[0:1] * lv[1:2], axis=-1, keepdims=True))
           - jnp.exp(jnp.sum(lv[2:3] * lv[3:4], axis=-1, keepdims=True)) + lam_init)
    nw = nw_ref[0]
    row = lax.broadcasted_iota(jnp.int32, (ATT_TQ, ATT_TQ), 0)
    col = lax.broadcasted_iota(jnp.int32, (ATT_TQ, ATT_TQ), 1)
    causal = col <= row
    q_refs = (q1_ref, q2_ref)
    k_refs = (k1_ref, k2_ref)
    for i in range(SEQ // ATT_TQ):
        start = i * ATT_TQ
        rows = slice(start, start + ATT_TQ)
        p_past, p_diag, inv_l = [], [], []
        for m in range(2):
            q = q_refs[m][rows, :]
            s_diag = jnp.where(causal, _dot_nt(q, k_refs[m][rows, :]), -jnp.inf)
            mx = jnp.max(s_diag, axis=-1, keepdims=True)
            if i > 0:
                s_past = _dot_nt(q, k_refs[m][0:start, :])
                mx = jnp.maximum(mx, jnp.max(s_past, axis=-1, keepdims=True))
                p_past.append(jnp.exp2(s_past - mx))
            p_diag.append(jnp.exp2(s_diag - mx))
            l = jnp.sum(p_diag[m], axis=-1, keepdims=True)
            if i > 0:
                l = l + jnp.sum(p_past[m], axis=-1, keepdims=True)
            inv_l.append(1.0 / l)
        c1 = inv_l[0]
        c2 = lam * inv_l[1]
        o = _dot((p_diag[0] * c1 - p_diag[1] * c2).astype(BF16), v_ref[rows, :])
        if i > 0:
            o = o + _dot((p_past[0] * c1 - p_past[1] * c2).astype(BF16), v_ref[0:start, :])
        o = _rms(o, nw) * (1.0 - lam_init)
        o_ref[rows, :] = (o * _silu(az_ref[rows, :])).astype(BF16)


def _diff_attn(l, q_d, k_d, proj_bf, proj_f32, lam, nw):
    lam_init = 0.8 - 0.6 * math.exp(-0.3 * l)
    return pl.pallas_call(
        functools.partial(_diff_attn_kernel, lam_init=lam_init),
        grid=(BATCH, DIFF_HEADS),
        in_specs=[
            pl.BlockSpec((SEQ, DIFF_D), lambda b, h: (b, 2 * h)),
            pl.BlockSpec((SEQ, DIFF_D), lambda b, h: (b, 2 * h + 1)),
            pl.BlockSpec((SEQ, DIFF_D), lambda b, h: (b, 2 * h)),
            pl.BlockSpec((SEQ, DIFF_D), lambda b, h: (b, 2 * h + 1)),
            pl.BlockSpec((SEQ, DIFF_DV), lambda b, h: (b, 8 + h)),
            pl.BlockSpec((SEQ, DIFF_DV), lambda b, h: (b, 4 + h)),
            pl.BlockSpec((1, 4, DIFF_D), lambda b, h: (l, 0, 0)),
            pl.BlockSpec((1, 1, DIFF_DV), lambda b, h: (l, 0, 0)),
        ],
        out_specs=pl.BlockSpec((SEQ, DIFF_DV), lambda b, h: (b, h)),
        out_shape=jax.ShapeDtypeStruct((TOKENS, DIFF_W), BF16),
        compiler_params=_params("arbitrary", "arbitrary"),
        name="diff_attn",
    )(q_d, q_d, k_d, k_d, proj_bf, proj_f32, lam, nw)


OUT_TM = 512


OUT_WROWS = 256


def _outproj_kernel(og_ref, od_ref, oa_ref, w_ref, x_ref, gate_ref, o_ref, wb_ref):
    @pl.when(pl.program_id(0) == 0)
    def _():
        def cast(i, carry):
            rows = pl.ds(pl.multiple_of(i * OUT_WROWS, OUT_WROWS), OUT_WROWS)
            wb_ref[rows, :] = w_ref[0, rows, :].astype(BF16)
            return carry

        lax.fori_loop(0, D_MODEL // OUT_WROWS, cast, 0)

    y = _dot(og_ref[...], wb_ref[0:GLA_W, :])
    y = y + _dot(od_ref[...], wb_ref[GLA_W:GLA_W + GDN_W, :])
    y = y + _dot(oa_ref[...], wb_ref[GLA_W + GDN_W:, :])
    o_ref[...] = x_ref[...] + gate_ref[0, 0] * y


def _out_proj(l, o_gla, o_gdn, o_diff, w_out, x2d, mods4):
    per_batch = SEQ // OUT_TM
    return pl.pallas_call(
        _outproj_kernel,
        grid=(TOKENS // OUT_TM,),
        in_specs=[
            pl.BlockSpec((OUT_TM, GLA_W), lambda m: (m, 0)),
            pl.BlockSpec((OUT_TM, GDN_W), lambda m: (m, 0)),
            pl.BlockSpec((OUT_TM, DIFF_W), lambda m: (m, 0)),
            pl.BlockSpec((1, D_MODEL, D_MODEL), lambda m: (l, 0, 0), pipeline_mode=pl.Buffered(1)),
            pl.BlockSpec((OUT_TM, D_MODEL), lambda m: (m, 0)),
            pl.BlockSpec((1, 1, 1, D_MODEL), lambda m: (l, m // per_batch, 0, 2)),
        ],
        out_specs=pl.BlockSpec((OUT_TM, D_MODEL), lambda m: (m, 0)),
        out_shape=jax.ShapeDtypeStruct((TOKENS, D_MODEL), F32),
        scratch_shapes=[pltpu.VMEM((D_MODEL, D_MODEL), BF16)],
        compiler_params=_params("arbitrary", vmem=52 * 1024 * 1024),
        name="out_proj",
    )(o_gla, o_gdn, o_diff, w_out, x2d, mods4)


def _lane_place(vecs, lane):
    n = vecs.shape[1]
    return jnp.pad(vecs.astype(F32), ((0, 0), (lane, LANES - lane - n))).reshape(DEPTH, 1, LANES)


def kernel(x, c, positions, norm_w, w_ada, b_ada, w_in, gla_w_lr, gla_b_lr, gla_norm_w, gdn_conv_w, gdn_a_log,
           gdn_dt_bias, gdn_norm_w, diff_q_norm_w, diff_k_norm_w, diff_lambda, diff_norm_w, w_out):
    c_pad = jnp.pad(c, ((0, 8 - BATCH), (0, 0)))
    mods4 = _ada_mod(c_pad, w_ada, b_ada).reshape(DEPTH, 8, 1, 3 * D_MODEL)
    cos2, sin2 = _rope_tables(positions)
    w_perm = _prep_w_in(w_in)
    row3 = lambda p: p.reshape(DEPTH, 1, p.shape[-1])
    norm_w3 = row3(norm_w)
    wlr_pad = jnp.pad(gla_w_lr, ((0, 0), (GLR_LANE, LANES - GLR_LANE - GLA_RANK), (0, 0)))
    alog_pad = _lane_place(gdn_a_log, DA_LANE)
    dtb_pad = _lane_place(gdn_dt_bias, DA_LANE)
    x2d = x.reshape(TOKENS, D_MODEL)
    for l in range(DEPTH):
        proj_bf, proj_f32 = _in_proj(l, x2d, norm_w3, mods4, w_perm)
        o_gla = _gla(l, proj_bf, proj_f32, wlr_pad, row3(gla_b_lr), row3(gla_norm_w))
        o_gdn = _gdn(l, proj_bf, proj_f32, gdn_conv_w, alog_pad, dtb_pad, row3(gdn_norm_w))
        q_d, k_d = _diff_prep(l, proj_bf, cos2, sin2, row3(diff_q_norm_w), row3(diff_k_norm_w))
        o_diff = _diff_attn(l, q_d, k_d, proj_bf, proj_f32, diff_lambda, row3(diff_norm_w))
        x2d = _out_proj(l, o_gla, o_gdn, o_diff, w_out, x2d, mods4)
    return x2d.reshape(BATCH, SEQ, D_MODEL)
```

```python
import functools
import math

import jax
import jax.numpy as jnp
from jax import lax
from jax.experimental import pallas as pl
from jax.experimental.pallas import tpu as pltpu

F32 = jnp.float32
BF16 = jnp.bfloat16

D_MODEL = 2048
BATCH = 4
SEQ = 2048
DEPTH = 2
TOKENS = BATCH * SEQ

GLA_HEADS = 4
GLA_DK = 64
GLA_DV = 128
GLA_W = GLA_HEADS * GLA_DV
GLA_RANK = 16
GLA_TAU = 16.0
GDN_HEADS = 4
GDN_D = 128
GDN_W = GDN_HEADS * GDN_D
CONV_K = 4
DIFF_HEADS = 4
DIFF_D = 128
DIFF_DV = 256
DIFF_W = DIFF_HEADS * DIFF_DV
CHUNK = 64
N_CHUNKS = SEQ // CHUNK
ROPE_THETA = 10000.0
EPS = 1e-6
LOG2E = math.log2(math.e)

LANES = 128

_IN_SPLITS = (
    ("gq", GLA_HEADS * GLA_DK), ("gk", GLA_HEADS * GLA_DK), ("gv", GLA_W), ("glr", GLA_RANK), ("gz", GLA_W),
    ("dq", GDN_W), ("dk", GDN_W), ("dv", GDN_W), ("da", GDN_HEADS), ("db", GDN_HEADS), ("dz", GDN_W),
    ("aq", DIFF_HEADS * 2 * DIFF_D), ("ak", DIFF_HEADS * 2 * DIFF_D), ("av", DIFF_W), ("az", DIFF_W),
)
_IN_OFFSETS = {}
_off = 0
for _name, _width in _IN_SPLITS:
    _IN_OFFSETS[_name] = (_off, _off + _width)
    _off += _width
D_IN = _off

_BF_ORDER = ("aq", "ak", "av", "gq", "gk", "gv", "dq", "dk", "dv")
_F32_ORDER = ("gz", "dz", "az")
N_BF = 5632
N_F32 = 2560
PROJ_TN = 512
NB_BF = N_BF // PROJ_TN
NB_F32 = N_F32 // PROJ_TN
WIN = 256
GLR_SRC = (_IN_OFFSETS["glr"][0] // LANES) * LANES
DAB_SRC = (_IN_OFFSETS["da"][0] // LANES) * LANES
GLR_COL = 2048
DAB_COL = GLR_COL + WIN
GLR_LANE = _IN_OFFSETS["glr"][0] - GLR_SRC
DA_LANE = _IN_OFFSETS["da"][0] - DAB_SRC
DB_LANE = _IN_OFFSETS["db"][0] - DAB_SRC
assert GLR_LANE + GLA_RANK <= LANES and DB_LANE + GDN_HEADS <= LANES
assert DAB_COL + WIN == N_F32

VMEM_LIMIT = 48 * 1024 * 1024


def _sigmoid(x):
    return 1.0 / (1.0 + jnp.exp(-x))


def _silu(x):
    return x * _sigmoid(x)


def _softplus(x):
    return jnp.maximum(x, 0.0) + jnp.log1p(jnp.exp(-jnp.abs(x)))


def _dot(a, b):
    return jnp.dot(a, b, preferred_element_type=F32)


def _dot_nt(a, b):
    return lax.dot_general(a, b, (((1,), (1,)), ((), ())), preferred_element_type=F32)


def _dot_tn(a, b):
    return lax.dot_general(a, b, (((0,), (0,)), ((), ())), preferred_element_type=F32)


def _dot_f32(a, b):
    return jnp.dot(a, b, precision=lax.Precision.HIGHEST, preferred_element_type=F32)


def _rms(x, w):
    return x * lax.rsqrt(jnp.mean(x * x, axis=-1, keepdims=True) + EPS) * w


def _params(*semantics, vmem=VMEM_LIMIT):
    return pltpu.CompilerParams(dimension_semantics=semantics, vmem_limit_bytes=vmem)


ADA_TN = 1024


def _ada_kernel(c_ref, w_ref, b_ref, o_ref):
    c_act = _silu(c_ref[...])
    o_ref[0] = _dot(c_act.astype(BF16), w_ref[0].astype(BF16)) + b_ref[0]


def _ada_mod(c_pad, w_ada, b_ada):
    n3 = 3 * D_MODEL
    return pl.pallas_call(
        _ada_kernel,
        grid=(DEPTH, n3 // ADA_TN),
        in_specs=[
            pl.BlockSpec((8, D_MODEL), lambda l, n: (0, 0)),
            pl.BlockSpec((1, D_MODEL, ADA_TN), lambda l, n: (l, 0, n)),
            pl.BlockSpec((1, 1, ADA_TN), lambda l, n: (l, 0, n)),
        ],
        out_specs=pl.BlockSpec((1, 8, ADA_TN), lambda l, n: (l, 0, n)),
        out_shape=jax.ShapeDtypeStruct((DEPTH, 8, n3), F32),
        compiler_params=_params("arbitrary", "arbitrary"),
        name="ada_mod",
    )(c_pad, w_ada, b_ada.reshape(DEPTH, 1, n3))


def _rope_kernel(pos_ref, freq_ref, sign_ref, cos_ref, sin_ref):
    ang = pos_ref[0].astype(F32) * freq_ref[...]
    cos_ref[0] = jnp.cos(ang)
    sin_ref[0] = jnp.sin(ang) * sign_ref[...]


def _rope_tables(positions):
    half = DIFF_D // 2
    inv_freq = ROPE_THETA ** (-jnp.arange(half, dtype=F32) / half)
    freq2 = jnp.concatenate([inv_freq, inv_freq]).reshape(1, DIFF_D)
    sign = jnp.concatenate([-jnp.ones((half,), F32), jnp.ones((half,), F32)]).reshape(1, DIFF_D)
    tab = jax.ShapeDtypeStruct((BATCH, SEQ, DIFF_D), F32)
    return pl.pallas_call(
        _rope_kernel,
        grid=(BATCH,),
        in_specs=[
            pl.BlockSpec((1, SEQ, 1), lambda b: (b, 0, 0)),
            pl.BlockSpec((1, DIFF_D), lambda b: (0, 0)),
            pl.BlockSpec((1, DIFF_D), lambda b: (0, 0)),
        ],
        out_specs=[pl.BlockSpec((1, SEQ, DIFF_D), lambda b: (b, 0, 0))] * 2,
        out_shape=[tab, tab],
        compiler_params=_params("arbitrary"),
        name="rope_tables",
    )(positions.reshape(BATCH, SEQ, 1), freq2, sign)


PROJ_TM = 1024
PROJ_ROWS = 256


def _inproj_kernel(x_ref, nw_ref, shift_ref, scale_ref, w_ref, ob_ref, of_ref, h_ref):
    n = pl.program_id(1)

    @pl.when(n == 0)
    def _():
        nw = nw_ref[0]
        scale1 = 1.0 + scale_ref[0, 0]
        shift = shift_ref[0, 0]
        for i in range(PROJ_TM // PROJ_ROWS):
            rows = slice(i * PROJ_ROWS, (i + 1) * PROJ_ROWS)
            h = _rms(x_ref[rows, :], nw) * scale1 + shift
            h_ref[rows, :] = h.astype(BF16)

    acc = _dot_nt(h_ref[...], w_ref[0])

    @pl.when(n < NB_BF)
    def _():
        ob_ref[...] = acc.astype(BF16)

    @pl.when(n >= NB_BF)
    def _():
        of_ref[...] = acc


def _in_proj(l, x2d, norm_w3, mods4, w_perm):
    per_batch = SEQ // PROJ_TM
    return pl.pallas_call(
        _inproj_kernel,
        grid=(TOKENS // PROJ_TM, NB_BF + NB_F32),
        in_specs=[
            pl.BlockSpec((PROJ_TM, D_MODEL), lambda m, n: (m, 0)),
            pl.BlockSpec((1, 1, D_MODEL), lambda m, n: (l, 0, 0)),
            pl.BlockSpec((1, 1, 1, D_MODEL), lambda m, n: (l, m // per_batch, 0, 0)),
            pl.BlockSpec((1, 1, 1, D_MODEL), lambda m, n: (l, m // per_batch, 0, 1)),
            pl.BlockSpec((1, PROJ_TN, D_MODEL), lambda m, n: (l, n, 0)),
        ],
        out_specs=[
            pl.BlockSpec((PROJ_TM, PROJ_TN), lambda m, n: (m, jnp.minimum(n, NB_BF - 1))),
            pl.BlockSpec((PROJ_TM, PROJ_TN), lambda m, n: (m, jnp.maximum(n - NB_BF, 0))),
        ],
        out_shape=[
            jax.ShapeDtypeStruct((TOKENS, N_BF), BF16),
            jax.ShapeDtypeStruct((TOKENS, N_F32), F32),
        ],
        scratch_shapes=[pltpu.VMEM((PROJ_TM, D_MODEL), BF16)],
        compiler_params=_params("arbitrary", "arbitrary"),
        name="in_proj",
    )(x2d, norm_w3, mods4, mods4, w_perm)


def _w_in_sources():
    starts = []
    for name in _BF_ORDER + _F32_ORDER:
        lo, hi = _IN_OFFSETS[name]
        assert (hi - lo) % WIN == 0 and lo % 8 == 0
        starts += list(range(lo, hi, WIN))
    starts += [GLR_SRC, DAB_SRC]
    assert len(starts) * WIN == N_BF + N_F32
    return starts


def _wprep_kernel(src_ref, w_ref, o_ref):
    del src_ref
    o_ref[0] = w_ref[...].astype(BF16)


def _prep_w_in(w_in_t):
    sublanes = 8
    starts = jnp.asarray([s // sublanes for s in _w_in_sources()], jnp.int32)
    grid_spec = pltpu.PrefetchScalarGridSpec(
        num_scalar_prefetch=1,
        grid=(DEPTH, (N_BF + N_F32) // WIN),
        in_specs=[pl.BlockSpec((pl.Squeezed(), pl.Element(WIN), pl.Element(D_MODEL)),
                               lambda l, i, src: (l, src[i] * sublanes, 0))],
        out_specs=pl.BlockSpec((1, WIN, D_MODEL), lambda l, i, src: (l, i, 0)),
    )
    return pl.pallas_call(
        _wprep_kernel,
        grid_spec=grid_spec,
        out_shape=jax.ShapeDtypeStruct((DEPTH, N_BF + N_F32, D_MODEL), BF16),
        compiler_params=_params("arbitrary", "arbitrary"),
        name="w_in_layout",
    )(starts, w_in_t)


GLA_GROUP = 4


def _gla_kernel(gq_ref, gk_ref, gv_ref, gz_ref, sm_ref, wlr_ref, blr_ref, nw_ref, o_ref, la_ref):
    z = _dot_f32(sm_ref[...], wlr_ref[0]) + blr_ref[0]
    la_ref[...] = -_softplus(-z) * (1.0 / GLA_TAU)

    row = lax.broadcasted_iota(jnp.int32, (CHUNK, CHUNK), 0)
    col = lax.broadcasted_iota(jnp.int32, (CHUNK, CHUNK), 1)
    incl = col <= row
    tril = jnp.where(incl, 1.0, 0.0)
    nw = nw_ref[0]
    heads = range(GLA_HEADS)
    group = range(GLA_GROUP)

    def kcols(x, h):
        return x[:, h * GLA_DK:(h + 1) * GLA_DK]

    def vcols(x, h):
        return x[:, h * GLA_DV:(h + 1) * GLA_DV]

    def body(gi, states):
        rs = [pl.ds(pl.multiple_of((gi * GLA_GROUP + c) * CHUNK, CHUNK), CHUNK) for c in group]
        bc = [_dot_f32(tril, la_ref[r, :]) for r in rs]
        qe, ke, kt, dec, v = [], [], [], [], []
        for c in group:
            b_last = bc[c][CHUNK - 1:CHUNK, :]
            q = gq_ref[rs[c], :].astype(F32) * GLA_DK ** -0.5
            k = gk_ref[rs[c], :].astype(F32)
            qe.append((q * jnp.exp(bc[c])).astype(BF16))
            ke.append((k * jnp.exp(-bc[c])).astype(BF16))
            kt.append((k * jnp.exp(b_last - bc[c])).astype(BF16))
            dec.append(jnp.exp(b_last))
            v.append(gv_ref[rs[c], :])
        att = [[jnp.where(incl, _dot_nt(kcols(qe[c], h), kcols(ke[c], h)), 0.0).astype(BF16) for h in heads]
               for c in group]
        o_intra = [[_dot(att[c][h], vcols(v[c], h)) for h in heads] for c in group]
        d_state = [[_dot_tn(vcols(v[c], h), kcols(kt[c], h)) for h in heads] for c in group]
        st = list(states)
        for c in group:
            gz = gz_ref[rs[c], :]
            outs = []
            for h in heads:
                o = o_intra[c][h] + _dot_nt(kcols(qe[c], h), st[h].astype(BF16))
                st[h] = st[h] * kcols(dec[c], h) + d_state[c][h]
                outs.append(_rms(o, nw) * _silu(vcols(gz, h)))
            o_ref[rs[c], :] = jnp.concatenate(outs, axis=-1).astype(BF16)
        return tuple(st)

    lax.fori_loop(0, N_CHUNKS // GLA_GROUP, body,
                  tuple(jnp.zeros((GLA_DV, GLA_DK), F32) for _ in heads))


def _gla(l, proj_bf, proj_f32, wlr_pad, blr, nw):
    return pl.pallas_call(
        _gla_kernel,
        grid=(BATCH,),
        in_specs=[
            pl.BlockSpec((SEQ, 256), lambda b: (b, 12)),
            pl.BlockSpec((SEQ, 256), lambda b: (b, 13)),
            pl.BlockSpec((SEQ, GLA_W), lambda b: (b, 7)),
            pl.BlockSpec((SEQ, GLA_W), lambda b: (b, 0)),
            pl.BlockSpec((SEQ, LANES), lambda b: (b, GLR_COL // LANES)),
            pl.BlockSpec((1, LANES, GLA_HEADS * GLA_DK), lambda b: (l, 0, 0)),
            pl.BlockSpec((1, 1, GLA_HEADS * GLA_DK), lambda b: (l, 0, 0)),
            pl.BlockSpec((1, 1, GLA_DV), lambda b: (l, 0, 0)),
        ],
        out_specs=pl.BlockSpec((SEQ, GLA_W), lambda b: (b, 0)),
        out_shape=jax.ShapeDtypeStruct((TOKENS, GLA_W), BF16),
        scratch_shapes=[pltpu.VMEM((SEQ, GLA_HEADS * GLA_DK), F32)],
        compiler_params=_params("arbitrary"),
        name="gla",
    )(proj_bf, proj_bf, proj_bf, proj_f32, proj_f32, wlr_pad, blr, nw)


GDN_RB = 256
GDN_HALO = 16
GDN_GROUP = 2


def _heads(x, h):
    return x[:, h * GDN_D:(h + 1) * GDN_D]


def _gdn_kernel(dq_ref, dk_ref, dv_ref, dz_ref, sm_ref, cw_ref, alog_ref, dtb_ref, nw_ref, o_ref,
                q_s, k_s, vu_s, w_s, qe_s, kt_s, qk_s, cd_s, g_s, beta_s, st_s):
    heads = range(GDN_HEADS)
    row = lax.broadcasted_iota(jnp.int32, (CHUNK, CHUNK), 0)
    col = lax.broadcasted_iota(jnp.int32, (CHUNK, CHUNK), 1)
    incl = col <= row
    strict = col < row
    tril = jnp.where(incl, 1.0, 0.0)
    nw = nw_ref[0]
    cw = cw_ref[0]
    neg_a = -jnp.exp(alog_ref[0])
    dtb = dtb_ref[0]

    def conv(x_ref, base, w):
        cur = x_ref[pl.ds(base, GDN_RB), :].astype(F32)
        prev = x_ref[pl.ds(pl.multiple_of(jnp.maximum(base - GDN_HALO, 0), GDN_HALO), GDN_HALO), :].astype(F32)
        prev = jnp.where(base > 0, prev, 0.0)
        win = jnp.concatenate([prev, cur], axis=0)
        first = GDN_HALO - (CONV_K - 1)
        acc = win[first:first + GDN_RB] * w[0:1]
        for j in range(1, CONV_K):
            acc = acc + win[first + j:first + j + GDN_RB] * w[j:j + 1]
        return _silu(acc)

    def l2n(x, scale):
        parts = []
        for h in heads:
            xh = _heads(x, h)
            parts.append(xh * (lax.rsqrt(jnp.sum(xh * xh, axis=-1, keepdims=True) + EPS) * scale))
        return jnp.concatenate(parts, axis=-1)

    def phase0(i, carry):
        base = pl.multiple_of(i * GDN_RB, GDN_RB)
        rows = pl.ds(base, GDN_RB)
        q_s[rows, :] = l2n(conv(dq_ref, base, cw[:, 0:GDN_W]), GDN_D ** -0.5)
        k_s[rows, :] = l2n(conv(dk_ref, base, cw[:, GDN_W:2 * GDN_W]), 1.0)
        vu_s[rows, :] = conv(dv_ref, base, cw[:, 2 * GDN_W:3 * GDN_W])
        sm = sm_ref[rows, :]
        g_s[rows, :] = neg_a * _softplus(sm + dtb)
        beta_s[rows, :] = _sigmoid(sm)
        return carry

    lax.fori_loop(0, SEQ // GDN_RB, phase0, 0)

    def phase_a(gi, carry):
        group = range(GDN_GROUP)
        units = [(c, h) for c in group for h in heads]
        rs = [pl.ds(pl.multiple_of((gi * GDN_GROUP + c) * CHUNK, CHUNK), CHUNK) for c in group]
        gcs = [_dot_f32(tril, g_s[r, :]) for r in rs]
        gcs_t = [g.T for g in gcs]
        beta_all = [beta_s[r, :] for r in rs]
        qn = [q_s[r, :] for r in rs]
        kn = [k_s[r, :] for r in rs]
        v = [vu_s[r, :] for r in rs]
        decay, eg, ekt, cd, k_beta, v_beta, kb = {}, {}, {}, {}, {}, {}, {}
        for c, h in units:
            gc_col = gcs[c][:, DA_LANE + h:DA_LANE + h + 1]
            gc_row = gcs_t[c][DA_LANE + h:DA_LANE + h + 1, :]
            decay[c, h] = jnp.exp(jnp.where(incl, gc_col - gc_row, -jnp.inf))
            gcb = jnp.broadcast_to(gc_col, (CHUNK, GDN_D))
            g_last = gcb[CHUNK - 1:CHUNK, :]
            eg[c, h] = jnp.exp(gcb)
            ekt[c, h] = jnp.exp(g_last - gcb)
            cd[c, h] = jnp.broadcast_to(jnp.exp(g_last), (8, GDN_D))
            b_col = beta_all[c][:, DB_LANE + h:DB_LANE + h + 1]
            k_beta[c, h] = _heads(kn[c], h) * b_col
            v_beta[c, h] = _heads(v[c], h) * b_col
            kb[c, h] = _heads(kn[c], h).astype(BF16)
        pw = {u: -jnp.where(strict, _dot_nt(k_beta[u].astype(BF16), kb[u]) * decay[u], 0.0) for u in units}
        tq = dict(pw)
        pw = {u: _dot_f32(pw[u], pw[u]) for u in units}
        for _ in range(4):
            both = {u: _dot_f32(jnp.concatenate([tq[u], pw[u]], axis=0), pw[u]) for u in units}
            tq = {u: tq[u] + pw[u] + both[u][0:CHUNK] for u in units}
            pw = {u: both[u][CHUNK:2 * CHUNK] for u in units}
        tq = {u: tq[u] + pw[u] + _dot_f32(tq[u], pw[u]) for u in units}
        rhs = {u: jnp.concatenate([v_beta[u], k_beta[u] * eg[u]], axis=-1) for u in units}
        sol = {u: rhs[u] + _dot(tq[u].astype(BF16), rhs[u].astype(BF16)) for u in units}
        qk = {(c, h): _dot_nt(_heads(qn[c], h).astype(BF16), kb[c, h]) * decay[c, h] for c, h in units}
        for c in group:
            r = rs[c]
            vu_s[r, :] = jnp.concatenate([sol[c, h][:, 0:GDN_D] for h in heads], axis=-1)
            w_s[r, :] = jnp.concatenate([sol[c, h][:, GDN_D:2 * GDN_D] for h in heads], axis=-1).astype(BF16)
            qe_s[r, :] = jnp.concatenate([_heads(qn[c], h) * eg[c, h] for h in heads], axis=-1).astype(BF16)
            kt_s[r, :] = jnp.concatenate([_heads(kn[c], h) * ekt[c, h] for h in heads], axis=-1).astype(BF16)
            qk_s[r, :] = jnp.concatenate([qk[c, h] for h in heads], axis=-1).astype(BF16)
            cd_s[gi * GDN_GROUP + c] = jnp.concatenate([cd[c, h] for h in heads], axis=-1)
        return carry

    lax.fori_loop(0, N_CHUNKS // GDN_GROUP, phase_a, 0)

    st_s[...] = jnp.zeros((GDN_HEADS, GDN_D, GDN_D), F32)

    def phase_b(n, carry):
        r = pl.ds(pl.multiple_of(n * CHUNK, CHUNK), CHUNK)
        u = vu_s[r, :]
        w = w_s[r, :]
        qe = qe_s[r, :]
        kt = kt_s[r, :]
        qk = qk_s[r, :]
        cd = cd_s[n]
        state = [st_s[h] for h in heads]
        sb = [s.astype(BF16) for s in state]
        ws = [_dot(_heads(w, h), sb[h]) for h in heads]
        qs = [_dot(_heads(qe, h), sb[h]) for h in heads]
        v_new = [(_heads(u, h) - ws[h]).astype(BF16) for h in heads]
        o = [qs[h] + _dot(qk[:, h * CHUNK:(h + 1) * CHUNK], v_new[h]) for h in heads]
        for h in heads:
            st_s[h] = state[h] * _heads(cd, h)[0:1] + _dot_tn(_heads(kt, h), v_new[h])
        vu_s[r, :] = jnp.concatenate(o, axis=-1)
        return carry

    lax.fori_loop(0, N_CHUNKS, phase_b, 0)

    def phase_c(i, carry):
        rows = pl.ds(pl.multiple_of(i * GDN_RB, GDN_RB), GDN_RB)
        o = vu_s[rows, :]
        o = jnp.concatenate([_rms(_heads(o, h), nw) for h in heads], axis=-1)
        o_ref[rows, :] = (o * _silu(dz_ref[rows, :])).astype(BF16)
        return carry

    lax.fori_loop(0, SEQ // GDN_RB, phase_c, 0)


def _gdn(l, proj_bf, proj_f32, conv_w, alog_pad, dtb_pad, nw):
    once = pl.Buffered(1)
    full_f32 = pltpu.VMEM((SEQ, GDN_W), F32)
    full_bf = pltpu.VMEM((SEQ, GDN_W), BF16)
    return pl.pallas_call(
        _gdn_kernel,
        grid=(BATCH,),
        in_specs=[
            pl.BlockSpec((SEQ, GDN_W), lambda b: (b, 8), pipeline_mode=once),
            pl.BlockSpec((SEQ, GDN_W), lambda b: (b, 9), pipeline_mode=once),
            pl.BlockSpec((SEQ, GDN_W), lambda b: (b, 10), pipeline_mode=once),
            pl.BlockSpec((SEQ, GDN_W), lambda b: (b, 1), pipeline_mode=once),
            pl.BlockSpec((SEQ, LANES), lambda b: (b, DAB_COL // LANES), pipeline_mode=once),
            pl.BlockSpec((1, CONV_K, 3 * GDN_W), lambda b: (l, 0, 0)),
            pl.BlockSpec((1, 1, LANES), lambda b: (l, 0, 0)),
            pl.BlockSpec((1, 1, LANES), lambda b: (l, 0, 0)),
            pl.BlockSpec((1, 1, GDN_D), lambda b: (l, 0, 0)),
        ],
        out_specs=pl.BlockSpec((SEQ, GDN_W), lambda b: (b, 0)),
        out_shape=jax.ShapeDtypeStruct((TOKENS, GDN_W), BF16),
        scratch_shapes=[full_f32, full_f32, full_f32,
                        full_bf, full_bf, full_bf,
                        pltpu.VMEM((SEQ, GDN_HEADS * CHUNK), BF16),
                        pltpu.VMEM((N_CHUNKS, 8, GDN_W), F32),
                        pltpu.VMEM((SEQ, LANES), F32), pltpu.VMEM((SEQ, LANES), F32),
                        pltpu.VMEM((GDN_HEADS, GDN_D, GDN_D), F32)],
        compiler_params=_params("arbitrary"),
        name="gdn",
    )(proj_bf, proj_bf, proj_bf, proj_f32, proj_f32, conv_w, alog_pad, dtb_pad, nw)


PREP_TS = 512
ATT_TQ = 256


def _diff_prep_kernel(aq_ref, ak_ref, cos_ref, sin_ref, qw_ref, kw_ref, q_ref, k_ref):
    cos2 = cos_ref[0]
    sin2 = sin_ref[0]
    for src, w_ref, dst, scale in ((aq_ref, qw_ref, q_ref, DIFF_D ** -0.5 * LOG2E), (ak_ref, kw_ref, k_ref, 1.0)):
        w = w_ref[0]
        for g in range(2 * DIFF_HEADS):
            cols = slice(g * DIFF_D, (g + 1) * DIFF_D)
            y = _rms(src[:, cols].astype(F32), w)
            y = y * cos2 + pltpu.roll(y, DIFF_D // 2, 1) * sin2
            dst[:, cols] = (y * scale).astype(BF16)


def _diff_prep(l, proj_bf, cos2, sin2, qw, kw):
    per_batch = SEQ // PREP_TS
    width = 2 * DIFF_HEADS * DIFF_D
    tab_spec = pl.BlockSpec((1, PREP_TS, DIFF_D), lambda i: (i // per_batch, i % per_batch, 0))
    out = jax.ShapeDtypeStruct((TOKENS, width), BF16)
    return pl.pallas_call(
        _diff_prep_kernel,
        grid=(TOKENS // PREP_TS,),
        in_specs=[
            pl.BlockSpec((PREP_TS, width), lambda i: (i, 0)),
            pl.BlockSpec((PREP_TS, width), lambda i: (i, 1)),
            tab_spec, tab_spec,
            pl.BlockSpec((1, 1, DIFF_D), lambda i: (l, 0, 0)),
            pl.BlockSpec((1, 1, DIFF_D), lambda i: (l, 0, 0)),
        ],
        out_specs=[pl.BlockSpec((PREP_TS, width), lambda i: (i, 0))] * 2,
        out_shape=[out, out],
        compiler_params=_params("arbitrary"),
        name="diff_prep",
    )(proj_bf, proj_bf, cos2, sin2, qw, kw)


def _diff_attn_kernel(q1_ref, q2_ref, k1_ref, k2_ref, v_ref, az_ref, lam_ref, nw_ref, o_ref, *, lam_init):
    lv = lam_ref[0]
    lam = (jnp.exp(jnp.sum(lv[0:1] * lv[1:2], axis=-1, keepdims=True))
           - jnp.exp(jnp.sum(lv[2:3] * lv[3:4], axis=-1, keepdims=True)) + lam_init)
    nw = nw_ref[0]
    row = lax.broadcasted_iota(jnp.int32, (ATT_TQ, ATT_TQ), 0)
    col = lax.broadcasted_iota(jnp.int32, (ATT_TQ, ATT_TQ), 1)
    causal = col <= row
    q_refs = (q1_ref, q2_ref)
    k_refs = (k1_ref, k2_ref)
    for i in range(SEQ // ATT_TQ):
        start = i * ATT_TQ
        rows = slice(start, start + ATT_TQ)
        p_past, p_diag, inv_l = [], [], []
        for m in range(2):
            q = q_refs[m][rows, :]
            s_diag = jnp.where(causal, _dot_nt(q, k_refs[m][rows, :]), -jnp.inf)
            mx = jnp.max(s_diag, axis=-1, keepdims=True)
            if i > 0:
                s_past = _dot_nt(q, k_refs[m][0:start, :])
                mx = jnp.maximum(mx, jnp.max(s_past, axis=-1, keepdims=True))
                p_past.append(jnp.exp2(s_past - mx))
            p_diag.append(jnp.exp2(s_diag - mx))
            l = jnp.sum(p_diag[m], axis=-1, keepdims=True)
            if i > 0:
                l = l + jnp.sum(p_past[m], axis=-1, keepdims=True)
            inv_l.append(1.0 / l)
        c1 = inv_l[0]
        c2 = lam * inv_l[1]
        o = _dot((p_diag[0] * c1 - p_diag[1] * c2).astype(BF16), v_ref[rows, :])
        if i > 0:
            o = o + _dot((p_past[0] * c1 - p_past[1] * c2).astype(BF16), v_ref[0:start, :])
        o = _rms(o, nw) * (1.0 - lam_init)
        o_ref[rows, :] = (o * _silu(az_ref[rows, :])).astype(BF16)


def _diff_attn(l, q_d, k_d, proj_bf, proj_f32, lam, nw):
    lam_init = 0.8 - 0.6 * math.exp(-0.3 * l)
    return pl.pallas_call(
        functools.partial(_diff_attn_kernel, lam_init=lam_init),
        grid=(BATCH, DIFF_HEADS),
        in_specs=[
            pl.BlockSpec((SEQ, DIFF_D), lambda b, h: (b, 2 * h)),
            pl.BlockSpec((SEQ, DIFF_D), lambda b, h: (b, 2 * h + 1)),
            pl.BlockSpec((SEQ, DIFF_D), lambda b, h: (b, 2 * h)),
            pl.BlockSpec((SEQ, DIFF_D), lambda b, h: (b, 2 * h + 1)),
            pl.BlockSpec((SEQ, DIFF_DV), lambda b, h: (b, 8 + h)),
            pl.BlockSpec((SEQ, DIFF_DV), lambda b, h: (b, 4 + h)),
            pl.BlockSpec((1, 4, DIFF_D), lambda b, h: (l, 0, 0)),
            pl.BlockSpec((1, 1, DIFF_DV), lambda b, h: (l, 0, 0)),
        ],
        out_specs=pl.BlockSpec((SEQ, DIFF_DV), lambda b, h: (b, h)),
        out_shape=jax.ShapeDtypeStruct((TOKENS, DIFF_W), BF16),
        compiler_params=_params("arbitrary", "arbitrary"),
        name="diff_attn",
    )(q_d, q_d, k_d, k_d, proj_bf, proj_f32, lam, nw)


OUT_TM = 512


OUT_WROWS = 256


def _outproj_kernel(og_ref, od_ref, oa_ref, w_ref, x_ref, gate_ref, o_ref, wb_ref):
    @pl.when(pl.program_id(0) == 0)
    def _():
        def cast(i, carry):
            rows = pl.ds(pl.multiple_of(i * OUT_WROWS, OUT_WROWS), OUT_WROWS)
            wb_ref[rows, :] = w_ref[0, rows, :].astype(BF16)
            return carry

        lax.fori_loop(0, D_MODEL // OUT_WROWS, cast, 0)

    y = _dot(og_ref[...], wb_ref[0:GLA_W, :])
    y = y + _dot(od_ref[...], wb_ref[GLA_W:GLA_W + GDN_W, :])
    y = y + _dot(oa_ref[...], wb_ref[GLA_W + GDN_W:, :])
    o_ref[...] = x_ref[...] + gate_ref[0, 0] * y


def _out_proj(l, o_gla, o_gdn, o_diff, w_out, x2d, mods4):
    per_batch = SEQ // OUT_TM
    return pl.pallas_call(
        _outproj_kernel,
        grid=(TOKENS // OUT_TM,),
        in_specs=[
            pl.BlockSpec((OUT_TM, GLA_W), lambda m: (m, 0)),
            pl.BlockSpec((OUT_TM, GDN_W), lambda m: (m, 0)),
            pl.BlockSpec((OUT_TM, DIFF_W), lambda m: (m, 0)),
            pl.BlockSpec((1, D_MODEL, D_MODEL), lambda m: (l, 0, 0), pipeline_mode=pl.Buffered(1)),
            pl.BlockSpec((OUT_TM, D_MODEL), lambda m: (m, 0)),
            pl.BlockSpec((1, 1, 1, D_MODEL), lambda m: (l, m // per_batch, 0, 2)),
        ],
        out_specs=pl.BlockSpec((OUT_TM, D_MODEL), lambda m: (m, 0)),
        out_shape=jax.ShapeDtypeStruct((TOKENS, D_MODEL), F32),
        scratch_shapes=[pltpu.VMEM((D_MODEL, D_MODEL), BF16)],
        compiler_params=_params("arbitrary", vmem=52 * 1024 * 1024),
        name="out_proj",
    )(o_gla, o_gdn, o_diff, w_out, x2d, mods4)


def _lane_place(vecs, lane):
    n = vecs.shape[1]
    return jnp.pad(vecs.astype(F32), ((0, 0), (lane, LANES - lane - n))).reshape(DEPTH, 1, LANES)


def kernel(x, c, positions, norm_w, w_ada, b_ada, w_in, gla_w_lr, gla_b_lr, gla_norm_w, gdn_conv_w, gdn_a_log,
           gdn_dt_bias, gdn_norm_w, diff_q_norm_w, diff_k_norm_w, diff_lambda, diff_norm_w, w_out):
    c_pad = jnp.pad(c, ((0, 8 - BATCH), (0, 0)))
    mods4 = _ada_mod(c_pad, w_ada, b_ada).reshape(DEPTH, 8, 1, 3 * D_MODEL)
    cos2, sin2 = _rope_tables(positions)
    w_perm = _prep_w_in(jnp.swapaxes(w_in, 1, 2))
    row3 = lambda p: p.reshape(DEPTH, 1, p.shape[-1])
    norm_w3 = row3(norm_w)
    wlr_pad = jnp.pad(gla_w_lr, ((0, 0), (GLR_LANE, LANES - GLR_LANE - GLA_RANK), (0, 0)))
    alog_pad = _lane_place(gdn_a_log, DA_LANE)
    dtb_pad = _lane_place(gdn_dt_bias, DA_LANE)
    x2d = x.reshape(TOKENS, D_MODEL)
    for l in range(DEPTH):
        proj_bf, proj_f32 = _in_proj(l, x2d, norm_w3, mods4, w_perm)
        o_gla = _gla(l, proj_bf, proj_f32, wlr_pad, row3(gla_b_lr), row3(gla_norm_w))
        o_gdn = _gdn(l, proj_bf, proj_f32, gdn_conv_w, alog_pad, dtb_pad, row3(gdn_norm_w))
        q_d, k_d = _diff_prep(l, proj_bf, cos2, sin2, row3(diff_q_norm_w), row3(diff_k_norm_w))
        o_diff = _diff_attn(l, q_d, k_d, proj_bf, proj_f32, diff_lambda, row3(diff_norm_w))
        x2d = _out_proj(l, o_gla, o_gdn, o_diff, w_out, x2d, mods4)
    return x2d.reshape(BATCH, SEQ, D_MODEL)
```

```python
import functools
import math

import jax
import jax.numpy as jnp
from jax import lax
from jax.experimental import pallas as pl
from jax.experimental.pallas import tpu as pltpu

F32 = jnp.float32
BF16 = jnp.bfloat16

D_MODEL = 2048
BATCH = 4
SEQ = 2048
DEPTH = 2
TOKENS = BATCH * SEQ

GLA_HEADS = 4
GLA_DK = 64
GLA_DV = 128
GLA_W = GLA_HEADS * GLA_DV
GLA_RANK = 16
GLA_TAU = 16.0
GDN_HEADS = 4
GDN_D = 128
GDN_W = GDN_HEADS * GDN_D
CONV_K = 4
DIFF_HEADS = 4
DIFF_D = 128
DIFF_DV = 256
DIFF_W = DIFF_HEADS * DIFF_DV
CHUNK = 64
N_CHUNKS = SEQ // CHUNK
ROPE_THETA = 10000.0
EPS = 1e-6
LOG2E = math.log2(math.e)

LANES = 128

_IN_SPLITS = (
    ("gq", GLA_HEADS * GLA_DK), ("gk", GLA_HEADS * GLA_DK), ("gv", GLA_W), ("glr", GLA_RANK), ("gz", GLA_W),
    ("dq", GDN_W), ("dk", GDN_W), ("dv", GDN_W), ("da", GDN_HEADS), ("db", GDN_HEADS), ("dz", GDN_W),
    ("aq", DIFF_HEADS * 2 * DIFF_D), ("ak", DIFF_HEADS * 2 * DIFF_D), ("av", DIFF_W), ("az", DIFF_W),
)
_IN_OFFSETS = {}
_off = 0
for _name, _width in _IN_SPLITS:
    _IN_OFFSETS[_name] = (_off, _off + _width)
    _off += _width
D_IN = _off

_BF_ORDER = ("aq", "ak", "av", "gq", "gk", "gv", "dq", "dk", "dv")
_F32_ORDER = ("gz", "dz", "az")
N_BF = 5632
N_F32 = 2560
PROJ_TN = 512
NB_BF = N_BF // PROJ_TN
NB_F32 = N_F32 // PROJ_TN
WIN = 256
GLR_SRC = (_IN_OFFSETS["glr"][0] // LANES) * LANES
DAB_SRC = (_IN_OFFSETS["da"][0] // LANES) * LANES
GLR_COL = 2048
DAB_COL = GLR_COL + WIN
GLR_LANE = _IN_OFFSETS["glr"][0] - GLR_SRC
DA_LANE = _IN_OFFSETS["da"][0] - DAB_SRC
DB_LANE = _IN_OFFSETS["db"][0] - DAB_SRC
assert GLR_LANE + GLA_RANK <= LANES and DB_LANE + GDN_HEADS <= LANES
assert DAB_COL + WIN == N_F32

VMEM_LIMIT = 48 * 1024 * 1024


def _sigmoid(x):
    return 1.0 / (1.0 + jnp.exp(-x))


def _silu(x):
    return x * _sigmoid(x)


def _softplus(x):
    return jnp.maximum(x, 0.0) + jnp.log1p(jnp.exp(-jnp.abs(x)))


def _dot(a, b):
    return jnp.dot(a, b, preferred_element_type=F32)


def _dot_nt(a, b):
    return lax.dot_general(a, b, (((1,), (1,)), ((), ())), preferred_element_type=F32)


def _dot_tn(a, b):
    return lax.dot_general(a, b, (((0,), (0,)), ((), ())), preferred_element_type=F32)


def _dot_f32(a, b):
    return jnp.dot(a, b, precision=lax.Precision.HIGHEST, preferred_element_type=F32)


def _rms(x, w):
    return x * lax.rsqrt(jnp.mean(x * x, axis=-1, keepdims=True) + EPS) * w


def _params(*semantics, vmem=VMEM_LIMIT):
    return pltpu.CompilerParams(dimension_semantics=semantics, vmem_limit_bytes=vmem)


ADA_TN = 1024


def _ada_kernel(c_ref, w_ref, b_ref, o_ref):
    c_act = _silu(c_ref[...])
    o_ref[0] = _dot(c_act.astype(BF16), w_ref[0].astype(BF16)) + b_ref[0]


def _ada_mod(c_pad, w_ada, b_ada):
    n3 = 3 * D_MODEL
    return pl.pallas_call(
        _ada_kernel,
        grid=(DEPTH, n3 // ADA_TN),
        in_specs=[
            pl.BlockSpec((8, D_MODEL), lambda l, n: (0, 0)),
            pl.BlockSpec((1, D_MODEL, ADA_TN), lambda l, n: (l, 0, n)),
            pl.BlockSpec((1, 1, ADA_TN), lambda l, n: (l, 0, n)),
        ],
        out_specs=pl.BlockSpec((1, 8, ADA_TN), lambda l, n: (l, 0, n)),
        out_shape=jax.ShapeDtypeStruct((DEPTH, 8, n3), F32),
        compiler_params=_params("arbitrary", "arbitrary"),
        name="ada_mod",
    )(c_pad, w_ada, b_ada.reshape(DEPTH, 1, n3))


def _rope_kernel(pos_ref, freq_ref, sign_ref, cos_ref, sin_ref):
    ang = pos_ref[0].astype(F32) * freq_ref[...]
    cos_ref[0] = jnp.cos(ang)
    sin_ref[0] = jnp.sin(ang) * sign_ref[...]


def _rope_tables(positions):
    half = DIFF_D // 2
    inv_freq = ROPE_THETA ** (-jnp.arange(half, dtype=F32) / half)
    freq2 = jnp.concatenate([inv_freq, inv_freq]).reshape(1, DIFF_D)
    sign = jnp.concatenate([-jnp.ones((half,), F32), jnp.ones((half,), F32)]).reshape(1, DIFF_D)
    tab = jax.ShapeDtypeStruct((BATCH, SEQ, DIFF_D), F32)
    return pl.pallas_call(
        _rope_kernel,
        grid=(BATCH,),
        in_specs=[
            pl.BlockSpec((1, SEQ, 1), lambda b: (b, 0, 0)),
            pl.BlockSpec((1, DIFF_D), lambda b: (0, 0)),
            pl.BlockSpec((1, DIFF_D), lambda b: (0, 0)),
        ],
        out_specs=[pl.BlockSpec((1, SEQ, DIFF_D), lambda b: (b, 0, 0))] * 2,
        out_shape=[tab, tab],
        compiler_params=_params("arbitrary"),
        name="rope_tables",
    )(positions.reshape(BATCH, SEQ, 1), freq2, sign)


PROJ_TM = 1024
PROJ_ROWS = 256


def _inproj_kernel(x_ref, nw_ref, shift_ref, scale_ref, w_ref, ob_ref, of_ref, h_ref):
    n = pl.program_id(1)

    @pl.when(n == 0)
    def _():
        gain = nw_ref[0] * (1.0 + scale_ref[0, 0])
        shift = shift_ref[0, 0]
        w = w_ref[0]
        for i in range(PROJ_TM // PROJ_ROWS):
            rows = slice(i * PROJ_ROWS, (i + 1) * PROJ_ROWS)
            x = x_ref[rows, :]
            h = (x * lax.rsqrt(jnp.mean(x * x, axis=-1, keepdims=True) + EPS) * gain + shift).astype(BF16)
            h_ref[rows, :] = h
            ob_ref[rows, :] = _dot_nt(h, w).astype(BF16)

    @pl.when(jnp.logical_and(n > 0, n < NB_BF))
    def _():
        ob_ref[...] = _dot_nt(h_ref[...], w_ref[0]).astype(BF16)

    @pl.when(n >= NB_BF)
    def _():
        of_ref[...] = _dot_nt(h_ref[...], w_ref[0])


def _in_proj(l, x2d, norm_w3, mods4, w_perm):
    per_batch = SEQ // PROJ_TM
    return pl.pallas_call(
        _inproj_kernel,
        grid=(TOKENS // PROJ_TM, NB_BF + NB_F32),
        in_specs=[
            pl.BlockSpec((PROJ_TM, D_MODEL), lambda m, n: (m, 0)),
            pl.BlockSpec((1, 1, D_MODEL), lambda m, n: (l, 0, 0)),
            pl.BlockSpec((1, 1, 1, D_MODEL), lambda m, n: (l, m // per_batch, 0, 0)),
            pl.BlockSpec((1, 1, 1, D_MODEL), lambda m, n: (l, m // per_batch, 0, 1)),
            pl.BlockSpec((1, PROJ_TN, D_MODEL), lambda m, n: (l, n, 0)),
        ],
        out_specs=[
            pl.BlockSpec((PROJ_TM, PROJ_TN), lambda m, n: (m, jnp.minimum(n, NB_BF - 1))),
            pl.BlockSpec((PROJ_TM, PROJ_TN), lambda m, n: (m, jnp.maximum(n - NB_BF, 0))),
        ],
        out_shape=[
            jax.ShapeDtypeStruct((TOKENS, N_BF), BF16),
            jax.ShapeDtypeStruct((TOKENS, N_F32), F32),
        ],
        scratch_shapes=[pltpu.VMEM((PROJ_TM, D_MODEL), BF16)],
        compiler_params=_params("arbitrary", "arbitrary"),
        name="in_proj",
    )(x2d, norm_w3, mods4, mods4, w_perm)


def _w_in_sources():
    starts = []
    for name in _BF_ORDER + _F32_ORDER:
        lo, hi = _IN_OFFSETS[name]
        assert (hi - lo) % WIN == 0 and lo % 8 == 0
        starts += list(range(lo, hi, WIN))
    starts += [GLR_SRC, DAB_SRC]
    assert len(starts) * WIN == N_BF + N_F32
    return starts


def _wprep_kernel(src_ref, w_ref, o_ref):
    del src_ref
    o_ref[0] = w_ref[...].astype(BF16)


def _prep_w_in(w_in_t):
    sublanes = 8
    starts = jnp.asarray([s // sublanes for s in _w_in_sources()], jnp.int32)
    grid_spec = pltpu.PrefetchScalarGridSpec(
        num_scalar_prefetch=1,
        grid=(DEPTH, (N_BF + N_F32) // WIN),
        in_specs=[pl.BlockSpec((pl.Squeezed(), pl.Element(WIN), pl.Element(D_MODEL)),
                               lambda l, i, src: (l, src[i] * sublanes, 0))],
        out_specs=pl.BlockSpec((1, WIN, D_MODEL), lambda l, i, src: (l, i, 0)),
    )
    return pl.pallas_call(
        _wprep_kernel,
        grid_spec=grid_spec,
        out_shape=jax.ShapeDtypeStruct((DEPTH, N_BF + N_F32, D_MODEL), BF16),
        compiler_params=_params("arbitrary", "arbitrary"),
        name="w_in_layout",
    )(starts, w_in_t)


GLA_GROUP = 4


def _gla_kernel(gq_ref, gk_ref, gv_ref, gz_ref, sm_ref, wlr_ref, blr_ref, nw_ref, o_ref, la_ref):
    z = _dot_f32(sm_ref[...], wlr_ref[0]) + blr_ref[0]
    la_ref[...] = -_softplus(-z) * (1.0 / GLA_TAU)

    row = lax.broadcasted_iota(jnp.int32, (CHUNK, CHUNK), 0)
    col = lax.broadcasted_iota(jnp.int32, (CHUNK, CHUNK), 1)
    incl = col <= row
    tril = jnp.where(incl, 1.0, 0.0)
    nw = nw_ref[0]
    heads = range(GLA_HEADS)
    group = range(GLA_GROUP)

    def kcols(x, h):
        return x[:, h * GLA_DK:(h + 1) * GLA_DK]

    def vcols(x, h):
        return x[:, h * GLA_DV:(h + 1) * GLA_DV]

    def body(gi, states):
        rs = [pl.ds(pl.multiple_of((gi * GLA_GROUP + c) * CHUNK, CHUNK), CHUNK) for c in group]
        bc = [_dot_f32(tril, la_ref[r, :]) for r in rs]
        qe, ke, kt, dec, v = [], [], [], [], []
        for c in group:
            b_last = bc[c][CHUNK - 1:CHUNK, :]
            q = gq_ref[rs[c], :].astype(F32) * GLA_DK ** -0.5
            k = gk_ref[rs[c], :].astype(F32)
            qe.append((q * jnp.exp(bc[c])).astype(BF16))
            ke.append((k * jnp.exp(-bc[c])).astype(BF16))
            kt.append((k * jnp.exp(b_last - bc[c])).astype(BF16))
            dec.append(jnp.exp(b_last))
            v.append(gv_ref[rs[c], :])
        att = [[jnp.where(incl, _dot_nt(kcols(qe[c], h), kcols(ke[c], h)), 0.0).astype(BF16) for h in heads]
               for c in group]
        o_intra = [[_dot(att[c][h], vcols(v[c], h)) for h in heads] for c in group]
        d_state = [[_dot_tn(vcols(v[c], h), kcols(kt[c], h)) for h in heads] for c in group]
        st = list(states)
        for c in group:
            gz = gz_ref[rs[c], :]
            outs = []
            for h in heads:
                o = o_intra[c][h] + _dot_nt(kcols(qe[c], h), st[h].astype(BF16))
                st[h] = st[h] * kcols(dec[c], h) + d_state[c][h]
                outs.append(_rms(o, nw) * _silu(vcols(gz, h)))
            o_ref[rs[c], :] = jnp.concatenate(outs, axis=-1).astype(BF16)
        return tuple(st)

    lax.fori_loop(0, N_CHUNKS // GLA_GROUP, body,
                  tuple(jnp.zeros((GLA_DV, GLA_DK), F32) for _ in heads))


def _gla(l, proj_bf, proj_f32, wlr_pad, blr, nw):
    return pl.pallas_call(
        _gla_kernel,
        grid=(BATCH,),
        in_specs=[
            pl.BlockSpec((SEQ, 256), lambda b: (b, 12)),
            pl.BlockSpec((SEQ, 256), lambda b: (b, 13)),
            pl.BlockSpec((SEQ, GLA_W), lambda b: (b, 7)),
            pl.BlockSpec((SEQ, GLA_W), lambda b: (b, 0)),
            pl.BlockSpec((SEQ, LANES), lambda b: (b, GLR_COL // LANES)),
            pl.BlockSpec((1, LANES, GLA_HEADS * GLA_DK), lambda b: (l, 0, 0)),
            pl.BlockSpec((1, 1, GLA_HEADS * GLA_DK), lambda b: (l, 0, 0)),
            pl.BlockSpec((1, 1, GLA_DV), lambda b: (l, 0, 0)),
        ],
        out_specs=pl.BlockSpec((SEQ, GLA_W), lambda b: (b, 0)),
        out_shape=jax.ShapeDtypeStruct((TOKENS, GLA_W), BF16),
        scratch_shapes=[pltpu.VMEM((SEQ, GLA_HEADS * GLA_DK), F32)],
        compiler_params=_params("arbitrary"),
        name="gla",
    )(proj_bf, proj_bf, proj_bf, proj_f32, proj_f32, wlr_pad, blr, nw)


GDN_RB = 256
GDN_HALO = 16
GDN_GROUP = 2


def _heads(x, h):
    return x[:, h * GDN_D:(h + 1) * GDN_D]


def _gdn_kernel(dq_ref, dk_ref, dv_ref, dz_ref, sm_ref, cw_ref, alog_ref, dtb_ref, nw_ref, o_ref,
                q_s, k_s, vu_s, w_s, qe_s, kt_s, qk_s, cd_s, g_s, beta_s, st_s):
    heads = range(GDN_HEADS)
    row = lax.broadcasted_iota(jnp.int32, (CHUNK, CHUNK), 0)
    col = lax.broadcasted_iota(jnp.int32, (CHUNK, CHUNK), 1)
    incl = col <= row
    strict = col < row
    tril = jnp.where(incl, 1.0, 0.0)
    nw = nw_ref[0]
    cw = cw_ref[0]
    neg_a = -jnp.exp(alog_ref[0])
    dtb = dtb_ref[0]

    def conv(x_ref, base, w):
        cur = x_ref[pl.ds(base, GDN_RB), :].astype(F32)
        prev = x_ref[pl.ds(pl.multiple_of(jnp.maximum(base - GDN_HALO, 0), GDN_HALO), GDN_HALO), :].astype(F32)
        prev = jnp.where(base > 0, prev, 0.0)
        win = jnp.concatenate([prev, cur], axis=0)
        first = GDN_HALO - (CONV_K - 1)
        acc = win[first:first + GDN_RB] * w[0:1]
        for j in range(1, CONV_K):
            acc = acc + win[first + j:first + j + GDN_RB] * w[j:j + 1]
        return _silu(acc)

    def l2n(x, scale):
        parts = []
        for h in heads:
            xh = _heads(x, h)
            parts.append(xh * (lax.rsqrt(jnp.sum(xh * xh, axis=-1, keepdims=True) + EPS) * scale))
        return jnp.concatenate(parts, axis=-1)

    def phase0(i, carry):
        base = pl.multiple_of(i * GDN_RB, GDN_RB)
        rows = pl.ds(base, GDN_RB)
        q_s[rows, :] = l2n(conv(dq_ref, base, cw[:, 0:GDN_W]), GDN_D ** -0.5)
        k_s[rows, :] = l2n(conv(dk_ref, base, cw[:, GDN_W:2 * GDN_W]), 1.0)
        vu_s[rows, :] = conv(dv_ref, base, cw[:, 2 * GDN_W:3 * GDN_W])
        sm = sm_ref[rows, :]
        g_s[rows, :] = neg_a * _softplus(sm + dtb)
        beta_s[rows, :] = _sigmoid(sm)
        return carry

    lax.fori_loop(0, SEQ // GDN_RB, phase0, 0)

    def phase_a(gi, carry):
        group = range(GDN_GROUP)
        units = [(c, h) for c in group for h in heads]
        rs = [pl.ds(pl.multiple_of((gi * GDN_GROUP + c) * CHUNK, CHUNK), CHUNK) for c in group]
        gcs = [_dot_f32(tril, g_s[r, :]) for r in rs]
        gcs_t = [g.T for g in gcs]
        beta_all = [beta_s[r, :] for r in rs]
        qn = [q_s[r, :] for r in rs]
        kn = [k_s[r, :] for r in rs]
        v = [vu_s[r, :] for r in rs]
        decay, eg, ekt, cd, k_beta, v_beta, kb = {}, {}, {}, {}, {}, {}, {}
        for c, h in units:
            gc_col = gcs[c][:, DA_LANE + h:DA_LANE + h + 1]
            gc_row = gcs_t[c][DA_LANE + h:DA_LANE + h + 1, :]
            decay[c, h] = jnp.exp(jnp.where(incl, gc_col - gc_row, -jnp.inf))
            gcb = jnp.broadcast_to(gc_col, (CHUNK, GDN_D))
            g_last = gcb[CHUNK - 1:CHUNK, :]
            eg[c, h] = jnp.exp(gcb)
            ekt[c, h] = jnp.exp(g_last - gcb)
            cd[c, h] = jnp.broadcast_to(jnp.exp(g_last), (8, GDN_D))
            b_col = beta_all[c][:, DB_LANE + h:DB_LANE + h + 1]
            k_beta[c, h] = _heads(kn[c], h) * b_col
            v_beta[c, h] = _heads(v[c], h) * b_col
            kb[c, h] = _heads(kn[c], h).astype(BF16)
        pw = {u: -jnp.where(strict, _dot_nt(k_beta[u].astype(BF16), kb[u]) * decay[u], 0.0) for u in units}
        tq = dict(pw)
        pw = {u: _dot_f32(pw[u], pw[u]) for u in units}
        for _ in range(4):
            both = {u: _dot_f32(jnp.concatenate([tq[u], pw[u]], axis=0), pw[u]) for u in units}
            tq = {u: tq[u] + pw[u] + both[u][0:CHUNK] for u in units}
            pw = {u: both[u][CHUNK:2 * CHUNK] for u in units}
        tq = {u: tq[u] + pw[u] + _dot_f32(tq[u], pw[u]) for u in units}
        rhs = {u: jnp.concatenate([v_beta[u], k_beta[u] * eg[u]], axis=-1) for u in units}
        sol = {u: rhs[u] + _dot(tq[u].astype(BF16), rhs[u].astype(BF16)) for u in units}
        qk = {(c, h): _dot_nt(_heads(qn[c], h).astype(BF16), kb[c, h]) * decay[c, h] for c, h in units}
        for c in group:
            r = rs[c]
            vu_s[r, :] = jnp.concatenate([sol[c, h][:, 0:GDN_D] for h in heads], axis=-1)
            w_s[r, :] = jnp.concatenate([sol[c, h][:, GDN_D:2 * GDN_D] for h in heads], axis=-1).astype(BF16)
            qe_s[r, :] = jnp.concatenate([_heads(qn[c], h) * eg[c, h] for h in heads], axis=-1).astype(BF16)
            kt_s[r, :] = jnp.concatenate([_heads(kn[c], h) * ekt[c, h] for h in heads], axis=-1).astype(BF16)
            qk_s[r, :] = jnp.concatenate([qk[c, h] for h in heads], axis=-1).astype(BF16)
            cd_s[gi * GDN_GROUP + c] = jnp.concatenate([cd[c, h] for h in heads], axis=-1)
        return carry

    lax.fori_loop(0, N_CHUNKS // GDN_GROUP, phase_a, 0)

    st_s[...] = jnp.zeros((GDN_HEADS, GDN_D, GDN_D), F32)

    def phase_b(n, carry):
        r = pl.ds(pl.multiple_of(n * CHUNK, CHUNK), CHUNK)
        u = vu_s[r, :]
        w = w_s[r, :]
        qe = qe_s[r, :]
        kt = kt_s[r, :]
        qk = qk_s[r, :]
        cd = cd_s[n]
        state = [st_s[h] for h in heads]
        sb = [s.astype(BF16) for s in state]
        ws = [_dot(_heads(w, h), sb[h]) for h in heads]
        qs = [_dot(_heads(qe, h), sb[h]) for h in heads]
        v_new = [(_heads(u, h) - ws[h]).astype(BF16) for h in heads]
        o = [qs[h] + _dot(qk[:, h * CHUNK:(h + 1) * CHUNK], v_new[h]) for h in heads]
        for h in heads:
            st_s[h] = state[h] * _heads(cd, h)[0:1] + _dot_tn(_heads(kt, h), v_new[h])
        vu_s[r, :] = jnp.concatenate(o, axis=-1)
        return carry

    lax.fori_loop(0, N_CHUNKS, phase_b, 0)

    def phase_c(i, carry):
        rows = pl.ds(pl.multiple_of(i * GDN_RB, GDN_RB), GDN_RB)
        o = vu_s[rows, :]
        o = jnp.concatenate([_rms(_heads(o, h), nw) for h in heads], axis=-1)
        o_ref[rows, :] = (o * _silu(dz_ref[rows, :])).astype(BF16)
        return carry

    lax.fori_loop(0, SEQ // GDN_RB, phase_c, 0)


def _gdn(l, proj_bf, proj_f32, conv_w, alog_pad, dtb_pad, nw):
    once = pl.Buffered(1)
    full_f32 = pltpu.VMEM((SEQ, GDN_W), F32)
    full_bf = pltpu.VMEM((SEQ, GDN_W), BF16)
    return pl.pallas_call(
        _gdn_kernel,
        grid=(BATCH,),
        in_specs=[
            pl.BlockSpec((SEQ, GDN_W), lambda b: (b, 8), pipeline_mode=once),
            pl.BlockSpec((SEQ, GDN_W), lambda b: (b, 9), pipeline_mode=once),
            pl.BlockSpec((SEQ, GDN_W), lambda b: (b, 10), pipeline_mode=once),
            pl.BlockSpec((SEQ, GDN_W), lambda b: (b, 1), pipeline_mode=once),
            pl.BlockSpec((SEQ, LANES), lambda b: (b, DAB_COL // LANES), pipeline_mode=once),
            pl.BlockSpec((1, CONV_K, 3 * GDN_W), lambda b: (l, 0, 0)),
            pl.BlockSpec((1, 1, LANES), lambda b: (l, 0, 0)),
            pl.BlockSpec((1, 1, LANES), lambda b: (l, 0, 0)),
            pl.BlockSpec((1, 1, GDN_D), lambda b: (l, 0, 0)),
        ],
        out_specs=pl.BlockSpec((SEQ, GDN_W), lambda b: (b, 0)),
        out_shape=jax.ShapeDtypeStruct((TOKENS, GDN_W), BF16),
        scratch_shapes=[full_f32, full_f32, full_f32,
                        full_bf, full_bf, full_bf,
                        pltpu.VMEM((SEQ, GDN_HEADS * CHUNK), BF16),
                        pltpu.VMEM((N_CHUNKS, 8, GDN_W), F32),
                        pltpu.VMEM((SEQ, LANES), F32), pltpu.VMEM((SEQ, LANES), F32),
                        pltpu.VMEM((GDN_HEADS, GDN_D, GDN_D), F32)],
        compiler_params=_params("arbitrary"),
        name="gdn",
    )(proj_bf, proj_bf, proj_bf, proj_f32, proj_f32, conv_w, alog_pad, dtb_pad, nw)


PREP_TS = 512
ATT_TQ = 256


def _diff_prep_kernel(aq_ref, ak_ref, cos_ref, sin_ref, qw_ref, kw_ref, q_ref, k_ref):
    cos2 = cos_ref[0]
    sin2 = sin_ref[0]
    for src, w_ref, dst, scale in ((aq_ref, qw_ref, q_ref, DIFF_D ** -0.5 * LOG2E), (ak_ref, kw_ref, k_ref, 1.0)):
        w = w_ref[0]
        for g in range(2 * DIFF_HEADS):
            cols = slice(g * DIFF_D, (g + 1) * DIFF_D)
            y = _rms(src[:, cols].astype(F32), w)
            y = y * cos2 + pltpu.roll(y, DIFF_D // 2, 1) * sin2
            dst[:, cols] = (y * scale).astype(BF16)


def _diff_prep(l, proj_bf, cos2, sin2, qw, kw):
    per_batch = SEQ // PREP_TS
    width = 2 * DIFF_HEADS * DIFF_D
    tab_spec = pl.BlockSpec((1, PREP_TS, DIFF_D), lambda i: (i // per_batch, i % per_batch, 0))
    out = jax.ShapeDtypeStruct((TOKENS, width), BF16)
    return pl.pallas_call(
        _diff_prep_kernel,
        grid=(TOKENS // PREP_TS,),
        in_specs=[
            pl.BlockSpec((PREP_TS, width), lambda i: (i, 0)),
            pl.BlockSpec((PREP_TS, width), lambda i: (i, 1)),
            tab_spec, tab_spec,
            pl.BlockSpec((1, 1, DIFF_D), lambda i: (l, 0, 0)),
            pl.BlockSpec((1, 1, DIFF_D), lambda i: (l, 0, 0)),
        ],
        out_specs=[pl.BlockSpec((PREP_TS, width), lambda i: (i, 0))] * 2,
        out_shape=[out, out],
        compiler_params=_params("arbitrary"),
        name="diff_prep",
    )(proj_bf, proj_bf, cos2, sin2, qw, kw)


def _diff_attn_kernel(q1_ref, q2_ref, k1_ref, k2_ref, v_ref, az_ref, lam_ref, nw_ref, o_ref, *, lam_init):
    lv = lam_ref[0]
    lam = (jnp.exp(jnp.sum(lv[0:1] * lv[1:2], axis=-1, keepdims=True))
           - jnp.exp(jnp.sum(lv[2:3] * lv[3:4], axis=-1, keepdims=True)) + lam_init)
    nw = nw_ref[0]
    row = lax.broadcasted_iota(jnp.int32, (ATT_TQ, ATT_TQ), 0)
    col = lax.broadcasted_iota(jnp.int32, (ATT_TQ, ATT_TQ), 1)
    causal = col <= row
    q_refs = (q1_ref, q2_ref)
    k_refs = (k1_ref, k2_ref)
    n_blocks = SEQ // ATT_TQ

    def scores(i):
        start = i * ATT_TQ
        rows = slice(start, start + ATT_TQ)
        out = []
        for m in range(2):
            q = q_refs[m][rows, :]
            s_diag = jnp.where(causal, _dot_nt(q, k_refs[m][rows, :]), -jnp.inf)
            s_past = _dot_nt(q, k_refs[m][0:start, :]) if i > 0 else None
            out.append((s_diag, s_past))
        return out

    s_next = scores(0)
    for i in range(n_blocks):
        start = i * ATT_TQ
        rows = slice(start, start + ATT_TQ)
        s_cur = s_next
        if i + 1 < n_blocks:
            s_next = scores(i + 1)
        ps, inv_l = [], []
        for s_diag, s_past in s_cur:
            mx = jnp.max(s_diag, axis=-1, keepdims=True)
            if i > 0:
                mx = jnp.maximum(mx, jnp.max(s_past, axis=-1, keepdims=True))
            p_diag = jnp.exp2(s_diag - mx)
            l = jnp.sum(p_diag, axis=-1, keepdims=True)
            p_past = None
            if i > 0:
                p_past = jnp.exp2(s_past - mx)
                l = l + jnp.sum(p_past, axis=-1, keepdims=True)
                p_past = p_past.astype(BF16)
            ps.append((p_diag.astype(BF16), p_past))
            inv_l.append(1.0 / l)
        pv = []
        for p_diag, p_past in ps:
            acc = _dot(p_diag, v_ref[rows, :])
            if i > 0:
                acc = acc + _dot(p_past, v_ref[0:start, :])
            pv.append(acc)
        o = pv[0] * inv_l[0] - pv[1] * (lam * inv_l[1])
        o = _rms(o, nw) * (1.0 - lam_init)
        o_ref[rows, :] = (o * _silu(az_ref[rows, :])).astype(BF16)


def _diff_attn(l, q_d, k_d, proj_bf, proj_f32, lam, nw):
    lam_init = 0.8 - 0.6 * math.exp(-0.3 * l)
    return pl.pallas_call(
        functools.partial(_diff_attn_kernel, lam_init=lam_init),
        grid=(BATCH, DIFF_HEADS),
        in_specs=[
            pl.BlockSpec((SEQ, DIFF_D), lambda b, h: (b, 2 * h)),
            pl.BlockSpec((SEQ, DIFF_D), lambda b, h: (b, 2 * h + 1)),
            pl.BlockSpec((SEQ, DIFF_D), lambda b, h: (b, 2 * h)),
            pl.BlockSpec((SEQ, DIFF_D), lambda b, h: (b, 2 * h + 1)),
            pl.BlockSpec((SEQ, DIFF_DV), lambda b, h: (b, 8 + h)),
            pl.BlockSpec((SEQ, DIFF_DV), lambda b, h: (b, 4 + h)),
            pl.BlockSpec((1, 4, DIFF_D), lambda b, h: (l, 0, 0)),
            pl.BlockSpec((1, 1, DIFF_DV), lambda b, h: (l, 0, 0)),
        ],
        out_specs=pl.BlockSpec((SEQ, DIFF_DV), lambda b, h: (b, h)),
        out_shape=jax.ShapeDtypeStruct((TOKENS, DIFF_W), BF16),
        compiler_params=_params("arbitrary", "arbitrary"),
        name="diff_attn",
    )(q_d, q_d, k_d, k_d, proj_bf, proj_f32, lam, nw)


OUT_TM = 512


OUT_WROWS = 256


def _outproj_kernel(og_ref, od_ref, oa_ref, w_ref, x_ref, gate_ref, o_ref, wb_ref):
    @pl.when(pl.program_id(0) == 0)
    def _():
        def cast(i, carry):
            rows = pl.ds(pl.multiple_of(i * OUT_WROWS, OUT_WROWS), OUT_WROWS)
            wb_ref[rows, :] = w_ref[0, rows, :].astype(BF16)
            return carry

        lax.fori_loop(0, D_MODEL // OUT_WROWS, cast, 0)

    y = _dot(og_ref[...], wb_ref[0:GLA_W, :])
    y = y + _dot(od_ref[...], wb_ref[GLA_W:GLA_W + GDN_W, :])
    y = y + _dot(oa_ref[...], wb_ref[GLA_W + GDN_W:, :])
    o_ref[...] = x_ref[...] + gate_ref[0, 0] * y


def _out_proj(l, o_gla, o_gdn, o_diff, w_out, x2d, mods4):
    per_batch = SEQ // OUT_TM
    return pl.pallas_call(
        _outproj_kernel,
        grid=(TOKENS // OUT_TM,),
        in_specs=[
            pl.BlockSpec((OUT_TM, GLA_W), lambda m: (m, 0)),
            pl.BlockSpec((OUT_TM, GDN_W), lambda m: (m, 0)),
            pl.BlockSpec((OUT_TM, DIFF_W), lambda m: (m, 0)),
            pl.BlockSpec((1, D_MODEL, D_MODEL), lambda m: (l, 0, 0), pipeline_mode=pl.Buffered(1)),
            pl.BlockSpec((OUT_TM, D_MODEL), lambda m: (m, 0)),
            pl.BlockSpec((1, 1, 1, D_MODEL), lambda m: (l, m // per_batch, 0, 2)),
        ],
        out_specs=pl.BlockSpec((OUT_TM, D_MODEL), lambda m: (m, 0)),
        out_shape=jax.ShapeDtypeStruct((TOKENS, D_MODEL), F32),
        scratch_shapes=[pltpu.VMEM((D_MODEL, D_MODEL), BF16)],
        compiler_params=_params("arbitrary", vmem=52 * 1024 * 1024),
        name="out_proj",
    )(o_gla, o_gdn, o_diff, w_out, x2d, mods4)


def _lane_place(vecs, lane):
    n = vecs.shape[1]
    return jnp.pad(vecs.astype(F32), ((0, 0), (lane, LANES - lane - n))).reshape(DEPTH, 1, LANES)


def kernel(x, c, positions, norm_w, w_ada, b_ada, w_in, gla_w_lr, gla_b_lr, gla_norm_w, gdn_conv_w, gdn_a_log,
           gdn_dt_bias, gdn_norm_w, diff_q_norm_w, diff_k_norm_w, diff_lambda, diff_norm_w, w_out):
    c_pad = jnp.pad(c, ((0, 8 - BATCH), (0, 0)))
    mods4 = _ada_mod(c_pad, w_ada, b_ada).reshape(DEPTH, 8, 1, 3 * D_MODEL)
    cos2, sin2 = _rope_tables(positions)
    w_perm = _prep_w_in(jnp.swapaxes(w_in, 1, 2))
    row3 = lambda p: p.reshape(DEPTH, 1, p.shape[-1])
    norm_w3 = row3(norm_w)
    wlr_pad = jnp.pad(gla_w_lr, ((0, 0), (GLR_LANE, LANES - GLR_LANE - GLA_RANK), (0, 0)))
    alog_pad = _lane_place(gdn_a_log, DA_LANE)
    dtb_pad = _lane_place(gdn_dt_bias, DA_LANE)
    x2d = x.reshape(TOKENS, D_MODEL)
    for l in range(DEPTH):
        proj_bf, proj_f32 = _in_proj(l, x2d, norm_w3, mods4, w_perm)
        o_gla = _gla(l, proj_bf, proj_f32, wlr_pad, row3(gla_b_lr), row3(gla_norm_w))
        o_gdn = _gdn(l, proj_bf, proj_f32, gdn_conv_w, alog_pad, dtb_pad, row3(gdn_norm_w))
        q_d, k_d = _diff_prep(l, proj_bf, cos2, sin2, row3(diff_q_norm_w), row3(diff_k_norm_w))
        o_diff = _diff_attn(l, q_d, k_d, proj_bf, proj_f32, diff_lambda, row3(diff_norm_w))
        x2d = _out_proj(l, o_gla, o_gdn, o_diff, w_out, x2d, mods4)
    return x2d.reshape(BATCH, SEQ, D_MODEL)
```

```python
import functools
import math

import jax
import jax.numpy as jnp
from jax import lax
from jax.experimental import pallas as pl
from jax.experimental.pallas import tpu as pltpu

F32 = jnp.float32
BF16 = jnp.bfloat16

D_MODEL = 2048
BATCH = 4
SEQ = 2048
DEPTH = 2
TOKENS = BATCH * SEQ

GLA_HEADS = 4
GLA_DK = 64
GLA_DV = 128
GLA_W = GLA_HEADS * GLA_DV
GLA_RANK = 16
GLA_TAU = 16.0
GDN_HEADS = 4
GDN_D = 128
GDN_W = GDN_HEADS * GDN_D
CONV_K = 4
DIFF_HEADS = 4
DIFF_D = 128
DIFF_DV = 256
DIFF_W = DIFF_HEADS * DIFF_DV
CHUNK = 64
N_CHUNKS = SEQ // CHUNK
ROPE_THETA = 10000.0
EPS = 1e-6
LOG2E = math.log2(math.e)

LANES = 128

_IN_SPLITS = (
    ("gq", GLA_HEADS * GLA_DK), ("gk", GLA_HEADS * GLA_DK), ("gv", GLA_W), ("glr", GLA_RANK), ("gz", GLA_W),
    ("dq", GDN_W), ("dk", GDN_W), ("dv", GDN_W), ("da", GDN_HEADS), ("db", GDN_HEADS), ("dz", GDN_W),
    ("aq", DIFF_HEADS * 2 * DIFF_D), ("ak", DIFF_HEADS * 2 * DIFF_D), ("av", DIFF_W), ("az", DIFF_W),
)
_IN_OFFSETS = {}
_off = 0
for _name, _width in _IN_SPLITS:
    _IN_OFFSETS[_name] = (_off, _off + _width)
    _off += _width
D_IN = _off

_BF_ORDER = ("aq", "ak", "av", "gq", "gk", "gv", "dq", "dk", "dv")
_F32_ORDER = ("gz", "dz", "az")
N_BF = 5632
N_F32 = 2048
PROJ_TN = 512
NB_BF = N_BF // PROJ_TN
NB_F32 = N_F32 // PROJ_TN
GLR_SRC = (_IN_OFFSETS["glr"][0] // LANES) * LANES
DAB_SRC = (_IN_OFFSETS["da"][0] // LANES) * LANES
GLR_WINDOW = 0
DAB_WINDOW = 1
GLR_LANE = _IN_OFFSETS["glr"][0] - GLR_SRC
DA_LANE = _IN_OFFSETS["da"][0] - DAB_SRC
DB_LANE = _IN_OFFSETS["db"][0] - DAB_SRC
assert GLR_LANE + GLA_RANK <= LANES and DB_LANE + GDN_HEADS <= LANES

VMEM_LIMIT = 48 * 1024 * 1024


def _sigmoid(x):
    return 1.0 / (1.0 + jnp.exp(-x))


def _silu(x):
    return x * _sigmoid(x)


def _softplus(x):
    return jnp.maximum(x, 0.0) + jnp.log1p(jnp.exp(-jnp.abs(x)))


def _dot(a, b):
    return jnp.dot(a, b, preferred_element_type=F32)


def _dot_nt(a, b):
    return lax.dot_general(a, b, (((1,), (1,)), ((), ())), preferred_element_type=F32)


def _dot_tn(a, b):
    return lax.dot_general(a, b, (((0,), (0,)), ((), ())), preferred_element_type=F32)


def _dot_f32(a, b):
    return jnp.dot(a, b, precision=lax.Precision.HIGHEST, preferred_element_type=F32)


def _rms(x, w):
    return x * lax.rsqrt(jnp.mean(x * x, axis=-1, keepdims=True) + EPS) * w


def _params(*semantics, vmem=VMEM_LIMIT):
    return pltpu.CompilerParams(dimension_semantics=semantics, vmem_limit_bytes=vmem)


ADA_TN = 1024


def _ada_kernel(c_ref, w_ref, b_ref, o_ref):
    c_act = _silu(c_ref[...])
    o_ref[0] = _dot(c_act.astype(BF16), w_ref[0].astype(BF16)) + b_ref[0]


def _ada_mod(c_pad, w_ada, b_ada):
    n3 = 3 * D_MODEL
    return pl.pallas_call(
        _ada_kernel,
        grid=(DEPTH, n3 // ADA_TN),
        in_specs=[
            pl.BlockSpec((8, D_MODEL), lambda l, n: (0, 0)),
            pl.BlockSpec((1, D_MODEL, ADA_TN), lambda l, n: (l, 0, n)),
            pl.BlockSpec((1, 1, ADA_TN), lambda l, n: (l, 0, n)),
        ],
        out_specs=pl.BlockSpec((1, 8, ADA_TN), lambda l, n: (l, 0, n)),
        out_shape=jax.ShapeDtypeStruct((DEPTH, 8, n3), F32),
        compiler_params=_params("arbitrary", "arbitrary"),
        name="ada_mod",
    )(c_pad, w_ada, b_ada.reshape(DEPTH, 1, n3))


def _rope_kernel(pos_ref, freq_ref, sign_ref, cos_ref, sin_ref):
    ang = pos_ref[0].astype(F32) * freq_ref[...]
    cos_ref[0] = jnp.cos(ang)
    sin_ref[0] = jnp.sin(ang) * sign_ref[...]


def _rope_tables(positions):
    half = DIFF_D // 2
    inv_freq = ROPE_THETA ** (-jnp.arange(half, dtype=F32) / half)
    freq2 = jnp.concatenate([inv_freq, inv_freq]).reshape(1, DIFF_D)
    sign = jnp.concatenate([-jnp.ones((half,), F32), jnp.ones((half,), F32)]).reshape(1, DIFF_D)
    tab = jax.ShapeDtypeStruct((BATCH, SEQ, DIFF_D), F32)
    return pl.pallas_call(
        _rope_kernel,
        grid=(BATCH,),
        in_specs=[
            pl.BlockSpec((1, SEQ, 1), lambda b: (b, 0, 0)),
            pl.BlockSpec((1, DIFF_D), lambda b: (0, 0)),
            pl.BlockSpec((1, DIFF_D), lambda b: (0, 0)),
        ],
        out_specs=[pl.BlockSpec((1, SEQ, DIFF_D), lambda b: (b, 0, 0))] * 2,
        out_shape=[tab, tab],
        compiler_params=_params("arbitrary"),
        name="rope_tables",
    )(positions.reshape(BATCH, SEQ, 1), freq2, sign)


PROJ_TM = 1024
PROJ_ROWS = 256


def _inproj_kernel(src_ref, x_ref, nw_ref, shift_ref, scale_ref, w_ref, wg_ref, wd_ref,
                   ob_ref, of_ref, os_ref, h_ref):
    del src_ref
    n = pl.program_id(1)

    @pl.when(n == 0)
    def _():
        gain = nw_ref[0] * (1.0 + scale_ref[0, 0])
        shift = shift_ref[0, 0]
        w = w_ref[...].astype(BF16)
        w_small = jnp.concatenate([wg_ref[...], wd_ref[...]], axis=0).astype(BF16)
        for i in range(PROJ_TM // PROJ_ROWS):
            rows = slice(i * PROJ_ROWS, (i + 1) * PROJ_ROWS)
            x = x_ref[rows, :]
            h = (x * lax.rsqrt(jnp.mean(x * x, axis=-1, keepdims=True) + EPS) * gain + shift).astype(BF16)
            h_ref[rows, :] = h
            ob_ref[rows, :] = _dot_nt(h, w).astype(BF16)
            os_ref[rows, :] = _dot_nt(h, w_small)

    @pl.when(jnp.logical_and(n > 0, n < NB_BF))
    def _():
        ob_ref[...] = _dot_nt(h_ref[...], w_ref[...].astype(BF16)).astype(BF16)

    @pl.when(n >= NB_BF)
    def _():
        of_ref[...] = _dot_nt(h_ref[...], w_ref[...].astype(BF16))


def _w_in_sources():
    starts = []
    for name in _BF_ORDER + _F32_ORDER:
        lo, hi = _IN_OFFSETS[name]
        starts += list(range(lo, hi, 256))
    firsts = starts[0::2]
    assert all(b == a + 256 for a, b in zip(firsts, starts[1::2])), "each 512 block must be contiguous"
    assert len(firsts) * PROJ_TN == N_BF + N_F32
    return firsts


def _in_proj(l, x2d, norm_w3, mods4, w_in_t):
    per_batch = SEQ // PROJ_TM
    sublanes = 8
    assert all(s % sublanes == 0 for s in _w_in_sources())
    starts = jnp.asarray([s // sublanes for s in _w_in_sources()], jnp.int32)

    def window(first_row):
        return pl.BlockSpec((pl.Squeezed(), pl.Element(LANES), pl.Element(D_MODEL)),
                            lambda m, n, src: (l, first_row, 0))

    grid_spec = pltpu.PrefetchScalarGridSpec(
        num_scalar_prefetch=1,
        grid=(TOKENS // PROJ_TM, NB_BF + NB_F32),
        in_specs=[
            pl.BlockSpec((PROJ_TM, D_MODEL), lambda m, n, src: (m, 0)),
            pl.BlockSpec((1, 1, D_MODEL), lambda m, n, src: (l, 0, 0)),
            pl.BlockSpec((1, 1, 1, D_MODEL), lambda m, n, src: (l, m // per_batch, 0, 0)),
            pl.BlockSpec((1, 1, 1, D_MODEL), lambda m, n, src: (l, m // per_batch, 0, 1)),
            pl.BlockSpec((pl.Squeezed(), pl.Element(PROJ_TN), pl.Element(D_MODEL)),
                         lambda m, n, src: (l, src[n] * sublanes, 0)),
            window(GLR_SRC),
            window(DAB_SRC),
        ],
        out_specs=[
            pl.BlockSpec((PROJ_TM, PROJ_TN), lambda m, n, src: (m, jnp.minimum(n, NB_BF - 1))),
            pl.BlockSpec((PROJ_TM, PROJ_TN), lambda m, n, src: (m, jnp.maximum(n - NB_BF, 0))),
            pl.BlockSpec((PROJ_TM, 2 * LANES), lambda m, n, src: (m, 0)),
        ],
        scratch_shapes=[pltpu.VMEM((PROJ_TM, D_MODEL), BF16)],
    )
    return pl.pallas_call(
        _inproj_kernel,
        grid_spec=grid_spec,
        out_shape=[
            jax.ShapeDtypeStruct((TOKENS, N_BF), BF16),
            jax.ShapeDtypeStruct((TOKENS, N_F32), F32),
            jax.ShapeDtypeStruct((TOKENS, 2 * LANES), F32),
        ],
        compiler_params=_params("arbitrary", "arbitrary"),
        name="in_proj",
    )(starts, x2d, norm_w3, mods4, mods4, w_in_t, w_in_t, w_in_t)


GLA_GROUP = 4


def _gla_kernel(gq_ref, gk_ref, gv_ref, gz_ref, sm_ref, wlr_ref, blr_ref, nw_ref, o_ref, la_ref):
    z = _dot_f32(sm_ref[...], wlr_ref[0]) + blr_ref[0]
    la_ref[...] = -_softplus(-z) * (1.0 / GLA_TAU)

    row = lax.broadcasted_iota(jnp.int32, (CHUNK, CHUNK), 0)
    col = lax.broadcasted_iota(jnp.int32, (CHUNK, CHUNK), 1)
    incl = col <= row
    tril = jnp.where(incl, 1.0, 0.0)
    nw = nw_ref[0]
    heads = range(GLA_HEADS)
    group = range(GLA_GROUP)

    def kcols(x, h):
        return x[:, h * GLA_DK:(h + 1) * GLA_DK]

    def vcols(x, h):
        return x[:, h * GLA_DV:(h + 1) * GLA_DV]

    def body(gi, states):
        rs = [pl.ds(pl.multiple_of((gi * GLA_GROUP + c) * CHUNK, CHUNK), CHUNK) for c in group]
        bc = [_dot_f32(tril, la_ref[r, :]) for r in rs]
        qe, ke, kt, dec, v = [], [], [], [], []
        for c in group:
            b_last = bc[c][CHUNK - 1:CHUNK, :]
            q = gq_ref[rs[c], :].astype(F32) * GLA_DK ** -0.5
            k = gk_ref[rs[c], :].astype(F32)
            qe.append((q * jnp.exp(bc[c])).astype(BF16))
            ke.append((k * jnp.exp(-bc[c])).astype(BF16))
            kt.append((k * jnp.exp(b_last - bc[c])).astype(BF16))
            dec.append(jnp.exp(b_last))
            v.append(gv_ref[rs[c], :])
        att = [[jnp.where(incl, _dot_nt(kcols(qe[c], h), kcols(ke[c], h)), 0.0).astype(BF16) for h in heads]
               for c in group]
        o_intra = [[_dot(att[c][h], vcols(v[c], h)) for h in heads] for c in group]
        d_state = [[_dot_tn(vcols(v[c], h), kcols(kt[c], h)) for h in heads] for c in group]
        st = list(states)
        for c in group:
            gz = gz_ref[rs[c], :]
            outs = []
            for h in heads:
                o = o_intra[c][h] + _dot_nt(kcols(qe[c], h), st[h].astype(BF16))
                st[h] = st[h] * kcols(dec[c], h) + d_state[c][h]
                outs.append(_rms(o, nw) * _silu(vcols(gz, h)))
            o_ref[rs[c], :] = jnp.concatenate(outs, axis=-1).astype(BF16)
        return tuple(st)

    lax.fori_loop(0, N_CHUNKS // GLA_GROUP, body,
                  tuple(jnp.zeros((GLA_DV, GLA_DK), F32) for _ in heads))


def _gla(l, proj_bf, proj_f32, proj_small, wlr_pad, blr, nw):
    return pl.pallas_call(
        _gla_kernel,
        grid=(BATCH,),
        in_specs=[
            pl.BlockSpec((SEQ, 256), lambda b: (b, 12)),
            pl.BlockSpec((SEQ, 256), lambda b: (b, 13)),
            pl.BlockSpec((SEQ, GLA_W), lambda b: (b, 7)),
            pl.BlockSpec((SEQ, GLA_W), lambda b: (b, 0)),
            pl.BlockSpec((SEQ, LANES), lambda b: (b, GLR_WINDOW)),
            pl.BlockSpec((1, LANES, GLA_HEADS * GLA_DK), lambda b: (l, 0, 0)),
            pl.BlockSpec((1, 1, GLA_HEADS * GLA_DK), lambda b: (l, 0, 0)),
            pl.BlockSpec((1, 1, GLA_DV), lambda b: (l, 0, 0)),
        ],
        out_specs=pl.BlockSpec((SEQ, GLA_W), lambda b: (b, 0)),
        out_shape=jax.ShapeDtypeStruct((TOKENS, GLA_W), BF16),
        scratch_shapes=[pltpu.VMEM((SEQ, GLA_HEADS * GLA_DK), F32)],
        compiler_params=_params("arbitrary"),
        name="gla",
    )(proj_bf, proj_bf, proj_bf, proj_f32, proj_small, wlr_pad, blr, nw)


GDN_RB = 256
GDN_HALO = 16
GDN_GROUP = 2


def _heads(x, h):
    return x[:, h * GDN_D:(h + 1) * GDN_D]


def _gdn_kernel(dq_ref, dk_ref, dv_ref, dz_ref, sm_ref, cw_ref, alog_ref, dtb_ref, nw_ref, o_ref,
                q_s, k_s, vu_s, w_s, qe_s, kt_s, qk_s, cd_s, g_s, beta_s, st_s):
    heads = range(GDN_HEADS)
    row = lax.broadcasted_iota(jnp.int32, (CHUNK, CHUNK), 0)
    col = lax.broadcasted_iota(jnp.int32, (CHUNK, CHUNK), 1)
    incl = col <= row
    strict = col < row
    tril = jnp.where(incl, 1.0, 0.0)
    nw = nw_ref[0]
    cw = cw_ref[0]
    neg_a = -jnp.exp(alog_ref[0])
    dtb = dtb_ref[0]

    def conv(x_ref, base, w):
        cur = x_ref[pl.ds(base, GDN_RB), :].astype(F32)
        prev = x_ref[pl.ds(pl.multiple_of(jnp.maximum(base - GDN_HALO, 0), GDN_HALO), GDN_HALO), :].astype(F32)
        prev = jnp.where(base > 0, prev, 0.0)
        win = jnp.concatenate([prev, cur], axis=0)
        first = GDN_HALO - (CONV_K - 1)
        acc = win[first:first + GDN_RB] * w[0:1]
        for j in range(1, CONV_K):
            acc = acc + win[first + j:first + j + GDN_RB] * w[j:j + 1]
        return _silu(acc)

    def l2n(x, scale):
        parts = []
        for h in heads:
            xh = _heads(x, h)
            parts.append(xh * (lax.rsqrt(jnp.sum(xh * xh, axis=-1, keepdims=True) + EPS) * scale))
        return jnp.concatenate(parts, axis=-1)

    def phase0(i, carry):
        base = pl.multiple_of(i * GDN_RB, GDN_RB)
        rows = pl.ds(base, GDN_RB)
        q_s[rows, :] = l2n(conv(dq_ref, base, cw[:, 0:GDN_W]), GDN_D ** -0.5)
        k_s[rows, :] = l2n(conv(dk_ref, base, cw[:, GDN_W:2 * GDN_W]), 1.0)
        vu_s[rows, :] = conv(dv_ref, base, cw[:, 2 * GDN_W:3 * GDN_W])
        sm = sm_ref[rows, :]
        g_s[rows, :] = neg_a * _softplus(sm + dtb)
        beta_s[rows, :] = _sigmoid(sm)
        return carry

    lax.fori_loop(0, SEQ // GDN_RB, phase0, 0)

    def phase_a(gi, carry):
        group = range(GDN_GROUP)
        units = [(c, h) for c in group for h in heads]
        rs = [pl.ds(pl.multiple_of((gi * GDN_GROUP + c) * CHUNK, CHUNK), CHUNK) for c in group]
        gcs = [_dot_f32(tril, g_s[r, :]) for r in rs]
        gcs_t = [g.T for g in gcs]
        beta_all = [beta_s[r, :] for r in rs]
        qn = [q_s[r, :] for r in rs]
        kn = [k_s[r, :] for r in rs]
        v = [vu_s[r, :] for r in rs]
        decay, eg, ekt, cd, k_beta, v_beta, kb = {}, {}, {}, {}, {}, {}, {}
        for c, h in units:
            gc_col = gcs[c][:, DA_LANE + h:DA_LANE + h + 1]
            gc_row = gcs_t[c][DA_LANE + h:DA_LANE + h + 1, :]
            decay[c, h] = jnp.exp(jnp.where(incl, gc_col - gc_row, -jnp.inf))
            gcb = jnp.broadcast_to(gc_col, (CHUNK, GDN_D))
            g_last = gcb[CHUNK - 1:CHUNK, :]
            eg[c, h] = jnp.exp(gcb)
            ekt[c, h] = jnp.exp(g_last - gcb)
            cd[c, h] = jnp.broadcast_to(jnp.exp(g_last), (8, GDN_D))
            b_col = beta_all[c][:, DB_LANE + h:DB_LANE + h + 1]
            k_beta[c, h] = _heads(kn[c], h) * b_col
            v_beta[c, h] = _heads(v[c], h) * b_col
            kb[c, h] = _heads(kn[c], h).astype(BF16)
        pw = {u: -jnp.where(strict, _dot_nt(k_beta[u].astype(BF16), kb[u]) * decay[u], 0.0) for u in units}
        tq = dict(pw)
        pw = {u: _dot_f32(pw[u], pw[u]) for u in units}
        for _ in range(4):
            both = {u: _dot_f32(jnp.concatenate([tq[u], pw[u]], axis=0), pw[u]) for u in units}
            tq = {u: tq[u] + pw[u] + both[u][0:CHUNK] for u in units}
            pw = {u: both[u][CHUNK:2 * CHUNK] for u in units}
        tq = {u: tq[u] + pw[u] + _dot_f32(tq[u], pw[u]) for u in units}
        rhs = {u: jnp.concatenate([v_beta[u], k_beta[u] * eg[u]], axis=-1) for u in units}
        sol = {u: rhs[u] + _dot(tq[u].astype(BF16), rhs[u].astype(BF16)) for u in units}
        qk = {(c, h): _dot_nt(_heads(qn[c], h).astype(BF16), kb[c, h]) * decay[c, h] for c, h in units}
        for c in group:
            r = rs[c]
            vu_s[r, :] = jnp.concatenate([sol[c, h][:, 0:GDN_D] for h in heads], axis=-1)
            w_s[r, :] = jnp.concatenate([sol[c, h][:, GDN_D:2 * GDN_D] for h in heads], axis=-1).astype(BF16)
            qe_s[r, :] = jnp.concatenate([_heads(qn[c], h) * eg[c, h] for h in heads], axis=-1).astype(BF16)
            kt_s[r, :] = jnp.concatenate([_heads(kn[c], h) * ekt[c, h] for h in heads], axis=-1).astype(BF16)
            qk_s[r, :] = jnp.concatenate([qk[c, h] for h in heads], axis=-1).astype(BF16)
            cd_s[gi * GDN_GROUP + c] = jnp.concatenate([cd[c, h] for h in heads], axis=-1)
        return carry

    lax.fori_loop(0, N_CHUNKS // GDN_GROUP, phase_a, 0)

    st_s[...] = jnp.zeros((GDN_HEADS, GDN_D, GDN_D), F32)

    def phase_b(n, carry):
        r = pl.ds(pl.multiple_of(n * CHUNK, CHUNK), CHUNK)
        u = vu_s[r, :]
        w = w_s[r, :]
        qe = qe_s[r, :]
        kt = kt_s[r, :]
        qk = qk_s[r, :]
        cd = cd_s[n]
        state = [st_s[h] for h in heads]
        sb = [s.astype(BF16) for s in state]
        ws = [_dot(_heads(w, h), sb[h]) for h in heads]
        qs = [_dot(_heads(qe, h), sb[h]) for h in heads]
        v_new = [(_heads(u, h) - ws[h]).astype(BF16) for h in heads]
        o = [qs[h] + _dot(qk[:, h * CHUNK:(h + 1) * CHUNK], v_new[h]) for h in heads]
        for h in heads:
            st_s[h] = state[h] * _heads(cd, h)[0:1] + _dot_tn(_heads(kt, h), v_new[h])
        vu_s[r, :] = jnp.concatenate(o, axis=-1)
        return carry

    lax.fori_loop(0, N_CHUNKS, phase_b, 0)

    def phase_c(i, carry):
        rows = pl.ds(pl.multiple_of(i * GDN_RB, GDN_RB), GDN_RB)
        o = vu_s[rows, :]
        o = jnp.concatenate([_rms(_heads(o, h), nw) for h in heads], axis=-1)
        o_ref[rows, :] = (o * _silu(dz_ref[rows, :])).astype(BF16)
        return carry

    lax.fori_loop(0, SEQ // GDN_RB, phase_c, 0)


def _gdn(l, proj_bf, proj_f32, proj_small, conv_w, alog_pad, dtb_pad, nw):
    once = pl.Buffered(1)
    full_f32 = pltpu.VMEM((SEQ, GDN_W), F32)
    full_bf = pltpu.VMEM((SEQ, GDN_W), BF16)
    return pl.pallas_call(
        _gdn_kernel,
        grid=(BATCH,),
        in_specs=[
            pl.BlockSpec((SEQ, GDN_W), lambda b: (b, 8), pipeline_mode=once),
            pl.BlockSpec((SEQ, GDN_W), lambda b: (b, 9), pipeline_mode=once),
            pl.BlockSpec((SEQ, GDN_W), lambda b: (b, 10), pipeline_mode=once),
            pl.BlockSpec((SEQ, GDN_W), lambda b: (b, 1), pipeline_mode=once),
            pl.BlockSpec((SEQ, LANES), lambda b: (b, DAB_WINDOW), pipeline_mode=once),
            pl.BlockSpec((1, CONV_K, 3 * GDN_W), lambda b: (l, 0, 0)),
            pl.BlockSpec((1, 1, LANES), lambda b: (l, 0, 0)),
            pl.BlockSpec((1, 1, LANES), lambda b: (l, 0, 0)),
            pl.BlockSpec((1, 1, GDN_D), lambda b: (l, 0, 0)),
        ],
        out_specs=pl.BlockSpec((SEQ, GDN_W), lambda b: (b, 0)),
        out_shape=jax.ShapeDtypeStruct((TOKENS, GDN_W), BF16),
        scratch_shapes=[full_f32, full_f32, full_f32,
                        full_bf, full_bf, full_bf,
                        pltpu.VMEM((SEQ, GDN_HEADS * CHUNK), BF16),
                        pltpu.VMEM((N_CHUNKS, 8, GDN_W), F32),
                        pltpu.VMEM((SEQ, LANES), F32), pltpu.VMEM((SEQ, LANES), F32),
                        pltpu.VMEM((GDN_HEADS, GDN_D, GDN_D), F32)],
        compiler_params=_params("arbitrary"),
        name="gdn",
    )(proj_bf, proj_bf, proj_bf, proj_f32, proj_small, conv_w, alog_pad, dtb_pad, nw)


PREP_TS = 512
ATT_TQ = 256


def _diff_prep_kernel(aq_ref, ak_ref, cos_ref, sin_ref, qw_ref, kw_ref, q_ref, k_ref):
    cos2 = cos_ref[0]
    sin2 = sin_ref[0]
    for src, w_ref, dst, scale in ((aq_ref, qw_ref, q_ref, DIFF_D ** -0.5 * LOG2E), (ak_ref, kw_ref, k_ref, 1.0)):
        w = w_ref[0]
        for g in range(2 * DIFF_HEADS):
            cols = slice(g * DIFF_D, (g + 1) * DIFF_D)
            y = _rms(src[:, cols].astype(F32), w)
            y = y * cos2 + pltpu.roll(y, DIFF_D // 2, 1) * sin2
            dst[:, cols] = (y * scale).astype(BF16)


def _diff_prep(l, proj_bf, cos2, sin2, qw, kw):
    per_batch = SEQ // PREP_TS
    width = 2 * DIFF_HEADS * DIFF_D
    tab_spec = pl.BlockSpec((1, PREP_TS, DIFF_D), lambda i: (i // per_batch, i % per_batch, 0))
    out = jax.ShapeDtypeStruct((TOKENS, width), BF16)
    return pl.pallas_call(
        _diff_prep_kernel,
        grid=(TOKENS // PREP_TS,),
        in_specs=[
            pl.BlockSpec((PREP_TS, width), lambda i: (i, 0)),
            pl.BlockSpec((PREP_TS, width), lambda i: (i, 1)),
            tab_spec, tab_spec,
            pl.BlockSpec((1, 1, DIFF_D), lambda i: (l, 0, 0)),
            pl.BlockSpec((1, 1, DIFF_D), lambda i: (l, 0, 0)),
        ],
        out_specs=[pl.BlockSpec((PREP_TS, width), lambda i: (i, 0))] * 2,
        out_shape=[out, out],
        compiler_params=_params("arbitrary"),
        name="diff_prep",
    )(proj_bf, proj_bf, cos2, sin2, qw, kw)


def _diff_attn_kernel(q1_ref, q2_ref, k1_ref, k2_ref, v_ref, az_ref, lam_ref, nw_ref, o_ref, *, lam_init):
    lv = lam_ref[0]
    lam = (jnp.exp(jnp.sum(lv[0:1] * lv[1:2], axis=-1, keepdims=True))
           - jnp.exp(jnp.sum(lv[2:3] * lv[3:4], axis=-1, keepdims=True)) + lam_init)
    nw = nw_ref[0]
    row = lax.broadcasted_iota(jnp.int32, (ATT_TQ, ATT_TQ), 0)
    col = lax.broadcasted_iota(jnp.int32, (ATT_TQ, ATT_TQ), 1)
    causal = col <= row
    q_refs = (q1_ref, q2_ref)
    k_refs = (k1_ref, k2_ref)
    n_blocks = SEQ // ATT_TQ

    def scores(i):
        start = i * ATT_TQ
        rows = slice(start, start + ATT_TQ)
        out = []
        for m in range(2):
            q = q_refs[m][rows, :]
            s_diag = jnp.where(causal, _dot_nt(q, k_refs[m][rows, :]), -jnp.inf)
            s_past = _dot_nt(q, k_refs[m][0:start, :]) if i > 0 else None
            out.append((s_diag, s_past))
        return out

    s_next = scores(0)
    for i in range(n_blocks):
        start = i * ATT_TQ
        rows = slice(start, start + ATT_TQ)
        s_cur = s_next
        if i + 1 < n_blocks:
            s_next = scores(i + 1)
        ps, inv_l = [], []
        for s_diag, s_past in s_cur:
            mx = jnp.max(s_diag, axis=-1, keepdims=True)
            if i > 0:
                mx = jnp.maximum(mx, jnp.max(s_past, axis=-1, keepdims=True))
            p_diag = jnp.exp2(s_diag - mx)
            l = jnp.sum(p_diag, axis=-1, keepdims=True)
            p_past = None
            if i > 0:
                p_past = jnp.exp2(s_past - mx)
                l = l + jnp.sum(p_past, axis=-1, keepdims=True)
                p_past = p_past.astype(BF16)
            ps.append((p_diag.astype(BF16), p_past))
            inv_l.append(1.0 / l)
        pv = []
        for p_diag, p_past in ps:
            acc = _dot(p_diag, v_ref[rows, :])
            if i > 0:
                acc = acc + _dot(p_past, v_ref[0:start, :])
            pv.append(acc)
        o = pv[0] * inv_l[0] - pv[1] * (lam * inv_l[1])
        o = _rms(o, nw) * (1.0 - lam_init)
        o_ref[rows, :] = (o * _silu(az_ref[rows, :])).astype(BF16)


def _diff_attn(l, q_d, k_d, proj_bf, proj_f32, lam, nw):
    lam_init = 0.8 - 0.6 * math.exp(-0.3 * l)
    return pl.pallas_call(
        functools.partial(_diff_attn_kernel, lam_init=lam_init),
        grid=(BATCH, DIFF_HEADS),
        in_specs=[
            pl.BlockSpec((SEQ, DIFF_D), lambda b, h: (b, 2 * h)),
            pl.BlockSpec((SEQ, DIFF_D), lambda b, h: (b, 2 * h + 1)),
            pl.BlockSpec((SEQ, DIFF_D), lambda b, h: (b, 2 * h)),
            pl.BlockSpec((SEQ, DIFF_D), lambda b, h: (b, 2 * h + 1)),
            pl.BlockSpec((SEQ, DIFF_DV), lambda b, h: (b, 8 + h)),
            pl.BlockSpec((SEQ, DIFF_DV), lambda b, h: (b, 4 + h)),
            pl.BlockSpec((1, 4, DIFF_D), lambda b, h: (l, 0, 0)),
            pl.BlockSpec((1, 1, DIFF_DV), lambda b, h: (l, 0, 0)),
        ],
        out_specs=pl.BlockSpec((SEQ, DIFF_DV), lambda b, h: (b, h)),
        out_shape=jax.ShapeDtypeStruct((TOKENS, DIFF_W), BF16),
        compiler_params=_params("arbitrary", "arbitrary"),
        name="diff_attn",
    )(q_d, q_d, k_d, k_d, proj_bf, proj_f32, lam, nw)


OUT_TM = 512


OUT_WROWS = 256


def _outproj_kernel(og_ref, od_ref, oa_ref, w_ref, x_ref, gate_ref, o_ref, wb_ref):
    @pl.when(pl.program_id(0) == 0)
    def _():
        def cast(i, carry):
            rows = pl.ds(pl.multiple_of(i * OUT_WROWS, OUT_WROWS), OUT_WROWS)
            wb_ref[rows, :] = w_ref[0, rows, :].astype(BF16)
            return carry

        lax.fori_loop(0, D_MODEL // OUT_WROWS, cast, 0)

    y = _dot(og_ref[...], wb_ref[0:GLA_W, :])
    y = y + _dot(od_ref[...], wb_ref[GLA_W:GLA_W + GDN_W, :])
    y = y + _dot(oa_ref[...], wb_ref[GLA_W + GDN_W:, :])
    o_ref[...] = x_ref[...] + gate_ref[0, 0] * y


def _out_proj(l, o_gla, o_gdn, o_diff, w_out, x2d, mods4):
    per_batch = SEQ // OUT_TM
    return pl.pallas_call(
        _outproj_kernel,
        grid=(TOKENS // OUT_TM,),
        in_specs=[
            pl.BlockSpec((OUT_TM, GLA_W), lambda m: (m, 0)),
            pl.BlockSpec((OUT_TM, GDN_W), lambda m: (m, 0)),
            pl.BlockSpec((OUT_TM, DIFF_W), lambda m: (m, 0)),
            pl.BlockSpec((1, D_MODEL, D_MODEL), lambda m: (l, 0, 0), pipeline_mode=pl.Buffered(1)),
            pl.BlockSpec((OUT_TM, D_MODEL), lambda m: (m, 0)),
            pl.BlockSpec((1, 1, 1, D_MODEL), lambda m: (l, m // per_batch, 0, 2)),
        ],
        out_specs=pl.BlockSpec((OUT_TM, D_MODEL), lambda m: (m, 0)),
        out_shape=jax.ShapeDtypeStruct((TOKENS, D_MODEL), F32),
        scratch_shapes=[pltpu.VMEM((D_MODEL, D_MODEL), BF16)],
        compiler_params=_params("arbitrary", vmem=52 * 1024 * 1024),
        name="out_proj",
    )(o_gla, o_gdn, o_diff, w_out, x2d, mods4)


def _lane_place(vecs, lane):
    n = vecs.shape[1]
    return jnp.pad(vecs.astype(F32), ((0, 0), (lane, LANES - lane - n))).reshape(DEPTH, 1, LANES)


def kernel(x, c, positions, norm_w, w_ada, b_ada, w_in, gla_w_lr, gla_b_lr, gla_norm_w, gdn_conv_w, gdn_a_log,
           gdn_dt_bias, gdn_norm_w, diff_q_norm_w, diff_k_norm_w, diff_lambda, diff_norm_w, w_out):
    c_pad = jnp.pad(c, ((0, 8 - BATCH), (0, 0)))
    mods4 = _ada_mod(c_pad, w_ada, b_ada).reshape(DEPTH, 8, 1, 3 * D_MODEL)
    cos2, sin2 = _rope_tables(positions)
    w_in_t = jnp.swapaxes(w_in, 1, 2)
    row3 = lambda p: p.reshape(DEPTH, 1, p.shape[-1])
    norm_w3 = row3(norm_w)
    wlr_pad = jnp.pad(gla_w_lr, ((0, 0), (GLR_LANE, LANES - GLR_LANE - GLA_RANK), (0, 0)))
    alog_pad = _lane_place(gdn_a_log, DA_LANE)
    dtb_pad = _lane_place(gdn_dt_bias, DA_LANE)
    x2d = x.reshape(TOKENS, D_MODEL)
    for l in range(DEPTH):
        proj_bf, proj_f32, proj_small = _in_proj(l, x2d, norm_w3, mods4, w_in_t)
        o_gla = _gla(l, proj_bf, proj_f32, proj_small, wlr_pad, row3(gla_b_lr), row3(gla_norm_w))
        o_gdn = _gdn(l, proj_bf, proj_f32, proj_small, gdn_conv_w, alog_pad, dtb_pad, row3(gdn_norm_w))
        q_d, k_d = _diff_prep(l, proj_bf, cos2, sin2, row3(diff_q_norm_w), row3(diff_k_norm_w))
        o_diff = _diff_attn(l, q_d, k_d, proj_bf, proj_f32, diff_lambda, row3(diff_norm_w))
        x2d = _out_proj(l, o_gla, o_gdn, o_diff, w_out, x2d, mods4)
    return x2d.reshape(BATCH, SEQ, D_MODEL)
```

```python
import functools
import math

import jax
import jax.numpy as jnp
from jax import lax
from jax.experimental import pallas as pl
from jax.experimental.pallas import tpu as pltpu

F32 = jnp.float32
BF16 = jnp.bfloat16

D_MODEL = 2048
BATCH = 4
SEQ = 2048
DEPTH = 2
TOKENS = BATCH * SEQ

GLA_HEADS = 4
GLA_DK = 64
GLA_DV = 128
GLA_W = GLA_HEADS * GLA_DV
GLA_RANK = 16
GLA_TAU = 16.0
GDN_HEADS = 4
GDN_D = 128
GDN_W = GDN_HEADS * GDN_D
CONV_K = 4
DIFF_HEADS = 4
DIFF_D = 128
DIFF_DV = 256
DIFF_W = DIFF_HEADS * DIFF_DV
CHUNK = 64
N_CHUNKS = SEQ // CHUNK
ROPE_THETA = 10000.0
EPS = 1e-6
LOG2E = math.log2(math.e)

LANES = 128

_IN_SPLITS = (
    ("gq", GLA_HEADS * GLA_DK), ("gk", GLA_HEADS * GLA_DK), ("gv", GLA_W), ("glr", GLA_RANK), ("gz", GLA_W),
    ("dq", GDN_W), ("dk", GDN_W), ("dv", GDN_W), ("da", GDN_HEADS), ("db", GDN_HEADS), ("dz", GDN_W),
    ("aq", DIFF_HEADS * 2 * DIFF_D), ("ak", DIFF_HEADS * 2 * DIFF_D), ("av", DIFF_W), ("az", DIFF_W),
)
_IN_OFFSETS = {}
_off = 0
for _name, _width in _IN_SPLITS:
    _IN_OFFSETS[_name] = (_off, _off + _width)
    _off += _width
D_IN = _off

_BF_ORDER = ("aq", "ak", "av", "gq", "gk", "gv", "dq", "dk", "dv")
_F32_ORDER = ("gz", "dz", "az")
N_BF = 5632
N_F32 = 2048
PROJ_TN = 512
NB_BF = N_BF // PROJ_TN
NB_F32 = N_F32 // PROJ_TN
GLR_SRC = (_IN_OFFSETS["glr"][0] // LANES) * LANES
DAB_SRC = (_IN_OFFSETS["da"][0] // LANES) * LANES
GLR_WINDOW = 0
DAB_WINDOW = 1
GLR_LANE = _IN_OFFSETS["glr"][0] - GLR_SRC
DA_LANE = _IN_OFFSETS["da"][0] - DAB_SRC
DB_LANE = _IN_OFFSETS["db"][0] - DAB_SRC
assert GLR_LANE + GLA_RANK <= LANES and DB_LANE + GDN_HEADS <= LANES

VMEM_LIMIT = 48 * 1024 * 1024


def _sigmoid(x):
    return 1.0 / (1.0 + jnp.exp(-x))


def _silu(x):
    return x * _sigmoid(x)


def _softplus(x):
    return jnp.maximum(x, 0.0) + jnp.log1p(jnp.exp(-jnp.abs(x)))


def _dot(a, b):
    return jnp.dot(a, b, preferred_element_type=F32)


def _dot_nt(a, b):
    return lax.dot_general(a, b, (((1,), (1,)), ((), ())), preferred_element_type=F32)


def _dot_tn(a, b):
    return lax.dot_general(a, b, (((0,), (0,)), ((), ())), preferred_element_type=F32)


def _dot_f32(a, b):
    return jnp.dot(a, b, precision=lax.Precision.HIGHEST, preferred_element_type=F32)


def _rms(x, w):
    return x * lax.rsqrt(jnp.mean(x * x, axis=-1, keepdims=True) + EPS) * w


def _params(*semantics, vmem=VMEM_LIMIT):
    return pltpu.CompilerParams(dimension_semantics=semantics, vmem_limit_bytes=vmem)


ADA_TN = 1024


def _ada_kernel(c_ref, w_ref, b_ref, o_ref):
    c_act = _silu(c_ref[...])
    o_ref[0] = _dot(c_act.astype(BF16), w_ref[0].astype(BF16)) + b_ref[0]


def _ada_mod(c_pad, w_ada, b_ada):
    n3 = 3 * D_MODEL
    return pl.pallas_call(
        _ada_kernel,
        grid=(DEPTH, n3 // ADA_TN),
        in_specs=[
            pl.BlockSpec((8, D_MODEL), lambda l, n: (0, 0)),
            pl.BlockSpec((1, D_MODEL, ADA_TN), lambda l, n: (l, 0, n)),
            pl.BlockSpec((1, 1, ADA_TN), lambda l, n: (l, 0, n)),
        ],
        out_specs=pl.BlockSpec((1, 8, ADA_TN), lambda l, n: (l, 0, n)),
        out_shape=jax.ShapeDtypeStruct((DEPTH, 8, n3), F32),
        compiler_params=_params("arbitrary", "arbitrary"),
        name="ada_mod",
    )(c_pad, w_ada, b_ada.reshape(DEPTH, 1, n3))


def _rope_kernel(pos_ref, freq_ref, sign_ref, cos_ref, sin_ref):
    ang = pos_ref[0].astype(F32) * freq_ref[...]
    cos_ref[0] = jnp.cos(ang)
    sin_ref[0] = jnp.sin(ang) * sign_ref[...]


def _rope_tables(positions):
    half = DIFF_D // 2
    inv_freq = ROPE_THETA ** (-jnp.arange(half, dtype=F32) / half)
    freq2 = jnp.concatenate([inv_freq, inv_freq]).reshape(1, DIFF_D)
    sign = jnp.concatenate([-jnp.ones((half,), F32), jnp.ones((half,), F32)]).reshape(1, DIFF_D)
    tab = jax.ShapeDtypeStruct((BATCH, SEQ, DIFF_D), F32)
    return pl.pallas_call(
        _rope_kernel,
        grid=(BATCH,),
        in_specs=[
            pl.BlockSpec((1, SEQ, 1), lambda b: (b, 0, 0)),
            pl.BlockSpec((1, DIFF_D), lambda b: (0, 0)),
            pl.BlockSpec((1, DIFF_D), lambda b: (0, 0)),
        ],
        out_specs=[pl.BlockSpec((1, SEQ, DIFF_D), lambda b: (b, 0, 0))] * 2,
        out_shape=[tab, tab],
        compiler_params=_params("arbitrary"),
        name="rope_tables",
    )(positions.reshape(BATCH, SEQ, 1), freq2, sign)


PROJ_TM = 1024
PROJ_ROWS = 256


def _inproj_kernel(src_ref, x_ref, nw_ref, shift_ref, scale_ref, w_ref, wg_ref, wd_ref,
                   ob_ref, of_ref, os_ref, h_ref):
    del src_ref
    n = pl.program_id(1)

    @pl.when(n == 0)
    def _():
        gain = nw_ref[0] * (1.0 + scale_ref[0, 0])
        shift = shift_ref[0, 0]
        w = w_ref[...].astype(BF16)
        w_small = jnp.concatenate([wg_ref[...], wd_ref[...]], axis=0).astype(BF16)
        for i in range(PROJ_TM // PROJ_ROWS):
            rows = slice(i * PROJ_ROWS, (i + 1) * PROJ_ROWS)
            x = x_ref[rows, :]
            h = (x * lax.rsqrt(jnp.mean(x * x, axis=-1, keepdims=True) + EPS) * gain + shift).astype(BF16)
            h_ref[rows, :] = h
            ob_ref[rows, :] = _dot_nt(h, w).astype(BF16)
            os_ref[rows, :] = _dot_nt(h, w_small)

    @pl.when(jnp.logical_and(n > 0, n < NB_BF))
    def _():
        ob_ref[...] = _dot_nt(h_ref[...], w_ref[...].astype(BF16)).astype(BF16)

    @pl.when(n >= NB_BF)
    def _():
        of_ref[...] = _dot_nt(h_ref[...], w_ref[...].astype(BF16))


def _w_in_sources():
    starts = []
    for name in _BF_ORDER + _F32_ORDER:
        lo, hi = _IN_OFFSETS[name]
        starts += list(range(lo, hi, 256))
    firsts = starts[0::2]
    assert all(b == a + 256 for a, b in zip(firsts, starts[1::2])), "each 512 block must be contiguous"
    assert len(firsts) * PROJ_TN == N_BF + N_F32
    return firsts


def _in_proj(l, x2d, norm_w3, mods4, w_in_t):
    per_batch = SEQ // PROJ_TM
    sublanes = 8
    assert all(s % sublanes == 0 for s in _w_in_sources())
    starts = jnp.asarray([s // sublanes for s in _w_in_sources()], jnp.int32)

    def window(first_row):
        return pl.BlockSpec((pl.Squeezed(), pl.Element(LANES), pl.Element(D_MODEL)),
                            lambda m, n, src: (l, first_row, 0))

    grid_spec = pltpu.PrefetchScalarGridSpec(
        num_scalar_prefetch=1,
        grid=(TOKENS // PROJ_TM, NB_BF + NB_F32),
        in_specs=[
            pl.BlockSpec((PROJ_TM, D_MODEL), lambda m, n, src: (m, 0)),
            pl.BlockSpec((1, 1, D_MODEL), lambda m, n, src: (l, 0, 0)),
            pl.BlockSpec((1, 1, 1, D_MODEL), lambda m, n, src: (l, m // per_batch, 0, 0)),
            pl.BlockSpec((1, 1, 1, D_MODEL), lambda m, n, src: (l, m // per_batch, 0, 1)),
            pl.BlockSpec((pl.Squeezed(), pl.Element(PROJ_TN), pl.Element(D_MODEL)),
                         lambda m, n, src: (l, src[n] * sublanes, 0)),
            window(GLR_SRC),
            window(DAB_SRC),
        ],
        out_specs=[
            pl.BlockSpec((PROJ_TM, PROJ_TN), lambda m, n, src: (m, jnp.minimum(n, NB_BF - 1))),
            pl.BlockSpec((PROJ_TM, PROJ_TN), lambda m, n, src: (m, jnp.maximum(n - NB_BF, 0))),
            pl.BlockSpec((PROJ_TM, 2 * LANES), lambda m, n, src: (m, 0)),
        ],
        scratch_shapes=[pltpu.VMEM((PROJ_TM, D_MODEL), BF16)],
    )
    return pl.pallas_call(
        _inproj_kernel,
        grid_spec=grid_spec,
        out_shape=[
            jax.ShapeDtypeStruct((TOKENS, N_BF), BF16),
            jax.ShapeDtypeStruct((TOKENS, N_F32), F32),
            jax.ShapeDtypeStruct((TOKENS, 2 * LANES), F32),
        ],
        compiler_params=_params("arbitrary", "arbitrary"),
        name="in_proj",
    )(starts, x2d, norm_w3, mods4, mods4, w_in_t, w_in_t, w_in_t)


GLA_GROUP = 4


def _gla_kernel(gq_ref, gk_ref, gv_ref, gz_ref, sm_ref, wlr_ref, blr_ref, nw_ref, o_ref, la_ref):
    z = _dot_f32(sm_ref[...], wlr_ref[0]) + blr_ref[0]
    la_ref[...] = -_softplus(-z) * (1.0 / GLA_TAU)

    row = lax.broadcasted_iota(jnp.int32, (CHUNK, CHUNK), 0)
    col = lax.broadcasted_iota(jnp.int32, (CHUNK, CHUNK), 1)
    incl = col <= row
    tril = jnp.where(incl, 1.0, 0.0)
    nw = nw_ref[0]
    heads = range(GLA_HEADS)
    group = range(GLA_GROUP)

    def kcols(x, h):
        return x[:, h * GLA_DK:(h + 1) * GLA_DK]

    def vcols(x, h):
        return x[:, h * GLA_DV:(h + 1) * GLA_DV]

    def body(gi, states):
        rs = [pl.ds(pl.multiple_of((gi * GLA_GROUP + c) * CHUNK, CHUNK), CHUNK) for c in group]
        bc = [_dot_f32(tril, la_ref[r, :]) for r in rs]
        qe, ke, kt, dec, v = [], [], [], [], []
        for c in group:
            b_last = bc[c][CHUNK - 1:CHUNK, :]
            q = gq_ref[rs[c], :].astype(F32) * GLA_DK ** -0.5
            k = gk_ref[rs[c], :].astype(F32)
            qe.append((q * jnp.exp(bc[c])).astype(BF16))
            ke.append((k * jnp.exp(-bc[c])).astype(BF16))
            kt.append((k * jnp.exp(b_last - bc[c])).astype(BF16))
            dec.append(jnp.exp(b_last))
            v.append(gv_ref[rs[c], :])
        att = [[jnp.where(incl, _dot_nt(kcols(qe[c], h), kcols(ke[c], h)), 0.0).astype(BF16) for h in heads]
               for c in group]
        o_intra = [[_dot(att[c][h], vcols(v[c], h)) for h in heads] for c in group]
        d_state = [[_dot_tn(vcols(v[c], h), kcols(kt[c], h)) for h in heads] for c in group]
        st = list(states)
        for c in group:
            gz = gz_ref[rs[c], :]
            outs = []
            for h in heads:
                o = o_intra[c][h] + _dot_nt(kcols(qe[c], h), st[h].astype(BF16))
                st[h] = st[h] * kcols(dec[c], h) + d_state[c][h]
                outs.append(_rms(o, nw) * _silu(vcols(gz, h)))
            o_ref[rs[c], :] = jnp.concatenate(outs, axis=-1).astype(BF16)
        return tuple(st)

    lax.fori_loop(0, N_CHUNKS // GLA_GROUP, body,
                  tuple(jnp.zeros((GLA_DV, GLA_DK), F32) for _ in heads))


def _gla(l, proj_bf, proj_f32, proj_small, wlr_pad, blr, nw):
    return pl.pallas_call(
        _gla_kernel,
        grid=(BATCH,),
        in_specs=[
            pl.BlockSpec((SEQ, 256), lambda b: (b, 12)),
            pl.BlockSpec((SEQ, 256), lambda b: (b, 13)),
            pl.BlockSpec((SEQ, GLA_W), lambda b: (b, 7)),
            pl.BlockSpec((SEQ, GLA_W), lambda b: (b, 0)),
            pl.BlockSpec((SEQ, LANES), lambda b: (b, GLR_WINDOW)),
            pl.BlockSpec((1, LANES, GLA_HEADS * GLA_DK), lambda b: (l, 0, 0)),
            pl.BlockSpec((1, 1, GLA_HEADS * GLA_DK), lambda b: (l, 0, 0)),
            pl.BlockSpec((1, 1, GLA_DV), lambda b: (l, 0, 0)),
        ],
        out_specs=pl.BlockSpec((SEQ, GLA_W), lambda b: (b, 0)),
        out_shape=jax.ShapeDtypeStruct((TOKENS, GLA_W), BF16),
        scratch_shapes=[pltpu.VMEM((SEQ, GLA_HEADS * GLA_DK), F32)],
        compiler_params=_params("arbitrary"),
        name="gla",
    )(proj_bf, proj_bf, proj_bf, proj_f32, proj_small, wlr_pad, blr, nw)


GDN_HALO = 16
GDN_GROUP = 2
GDN_ROWS = GDN_GROUP * CHUNK
N_GROUPS = N_CHUNKS // GDN_GROUP


def _heads(x, h):
    return x[:, h * GDN_D:(h + 1) * GDN_D]


def _gdn_kernel(dq_ref, dk_ref, dv_ref, dz_ref, sm_ref, cw_ref, alog_ref, dtb_ref, nw_ref, o_ref,
                pq_s, pk_s, pv_s, pg_s, pb_s, u_s, w_s, qe_s, kt_s, qk_s, cd_s, o_s, st_s):
    heads = range(GDN_HEADS)
    row = lax.broadcasted_iota(jnp.int32, (CHUNK, CHUNK), 0)
    col = lax.broadcasted_iota(jnp.int32, (CHUNK, CHUNK), 1)
    incl = col <= row
    strict = col < row
    tril = jnp.where(incl, 1.0, 0.0)
    nw = nw_ref[0]
    cw = cw_ref[0]
    neg_a = -jnp.exp(alog_ref[0])
    dtb = dtb_ref[0]

    def conv(x_ref, src, w, first_block):
        cur = x_ref[pl.ds(src, GDN_ROWS), :].astype(F32)
        if first_block:
            prev = jnp.zeros((GDN_HALO, GDN_W), F32)
        else:
            prev = x_ref[pl.ds(pl.multiple_of(src - GDN_HALO, GDN_HALO), GDN_HALO), :].astype(F32)
        win = jnp.concatenate([prev, cur], axis=0)
        first = GDN_HALO - (CONV_K - 1)
        acc = win[first:first + GDN_ROWS] * w[0:1]
        for j in range(1, CONV_K):
            acc = acc + win[first + j:first + j + GDN_ROWS] * w[j:j + 1]
        return _silu(acc)

    def l2n(x, scale):
        parts = []
        for h in heads:
            xh = _heads(x, h)
            parts.append(xh * (lax.rsqrt(jnp.sum(xh * xh, axis=-1, keepdims=True) + EPS) * scale))
        return jnp.concatenate(parts, axis=-1)

    def group_rows(g):
        return pl.multiple_of(g * GDN_ROWS, GDN_ROWS)

    def phase0_steps(g, slot, first_block=False):
        src = 0 if first_block else group_rows(g)

        def q_step():
            pq_s[slot] = l2n(conv(dq_ref, src, cw[:, 0:GDN_W], first_block), GDN_D ** -0.5)

        def k_step():
            pk_s[slot] = l2n(conv(dk_ref, src, cw[:, GDN_W:2 * GDN_W], first_block), 1.0)

        def v_step():
            pv_s[slot] = conv(dv_ref, src, cw[:, 2 * GDN_W:3 * GDN_W], first_block)

        def gate_step():
            sm = sm_ref[pl.ds(src, GDN_ROWS), :]
            pg_s[slot] = neg_a * _softplus(sm + dtb)
            pb_s[slot] = _sigmoid(sm)

        return [q_step, k_step, v_step, gate_step]

    for step in phase0_steps(0, 0, first_block=True):
        step()

    def wy_group(g, slot, ahead):
        group = range(GDN_GROUP)
        units = [(c, h) for c in group for h in heads]
        rs = [pl.ds(pl.multiple_of((g * GDN_GROUP + c) * CHUNK, CHUNK), CHUNK) for c in group]
        crows = [slice(c * CHUNK, (c + 1) * CHUNK) for c in group]
        gcs = [_dot_f32(tril, pg_s[slot, cr, :]) for cr in crows]
        gcs_t = [x.T for x in gcs]
        beta_all = [pb_s[slot, cr, :] for cr in crows]
        qn = [pq_s[slot, cr, :] for cr in crows]
        kn = [pk_s[slot, cr, :] for cr in crows]
        v = [pv_s[slot, cr, :] for cr in crows]
        decay, eg, ekt, cd, k_beta, v_beta, kb = {}, {}, {}, {}, {}, {}, {}
        for c, h in units:
            gc_col = gcs[c][:, DA_LANE + h:DA_LANE + h + 1]
            gc_row = gcs_t[c][DA_LANE + h:DA_LANE + h + 1, :]
            decay[c, h] = jnp.exp(jnp.where(incl, gc_col - gc_row, -jnp.inf))
            gcb = jnp.broadcast_to(gc_col, (CHUNK, GDN_D))
            g_last = gcb[CHUNK - 1:CHUNK, :]
            eg[c, h] = jnp.exp(gcb)
            ekt[c, h] = jnp.exp(g_last - gcb)
            cd[c, h] = jnp.broadcast_to(jnp.exp(g_last), (8, GDN_D))
            b_col = beta_all[c][:, DB_LANE + h:DB_LANE + h + 1]
            k_beta[c, h] = _heads(kn[c], h) * b_col
            v_beta[c, h] = _heads(v[c], h) * b_col
            kb[c, h] = _heads(kn[c], h).astype(BF16)
        pw = {u: -jnp.where(strict, _dot_nt(k_beta[u].astype(BF16), kb[u]) * decay[u], 0.0) for u in units}
        tq = dict(pw)
        pw = {u: _dot_f32(pw[u], pw[u]) for u in units}
        for level in range(4):
            both = {u: _dot_f32(jnp.concatenate([tq[u], pw[u]], axis=0), pw[u]) for u in units}
            ahead[level]()
            tq = {u: tq[u] + pw[u] + both[u][0:CHUNK] for u in units}
            pw = {u: both[u][CHUNK:2 * CHUNK] for u in units}
        tq = {u: tq[u] + pw[u] + _dot_f32(tq[u], pw[u]) for u in units}
        rhs = {u: jnp.concatenate([v_beta[u], k_beta[u] * eg[u]], axis=-1) for u in units}
        sol = {u: rhs[u] + _dot(tq[u].astype(BF16), rhs[u].astype(BF16)) for u in units}
        qk = {(c, h): _dot_nt(_heads(qn[c], h).astype(BF16), kb[c, h]) * decay[c, h] for c, h in units}
        for c in group:
            r = rs[c]
            u_s[r, :] = jnp.concatenate([sol[c, h][:, 0:GDN_D] for h in heads], axis=-1)
            w_s[r, :] = jnp.concatenate([sol[c, h][:, GDN_D:2 * GDN_D] for h in heads], axis=-1).astype(BF16)
            qe_s[r, :] = jnp.concatenate([_heads(qn[c], h) * eg[c, h] for h in heads], axis=-1).astype(BF16)
            kt_s[r, :] = jnp.concatenate([_heads(kn[c], h) * ekt[c, h] for h in heads], axis=-1).astype(BF16)
            qk_s[r, :] = jnp.concatenate([qk[c, h] for h in heads], axis=-1).astype(BF16)
            cd_s[g * GDN_GROUP + c] = jnp.concatenate([cd[c, h] for h in heads], axis=-1)

    def phase_a(i, carry):
        g = 2 * i
        wy_group(g, 0, phase0_steps(g + 1, 1))
        wy_group(g + 1, 1, phase0_steps(jnp.minimum(g + 2, N_GROUPS - 1), 0))
        return carry

    lax.fori_loop(0, N_GROUPS // 2, phase_a, 0)

    st_s[...] = jnp.zeros((GDN_HEADS, GDN_D, GDN_D), F32)

    def finish(n, slot):
        r = pl.ds(pl.multiple_of(n * CHUNK, CHUNK), CHUNK)
        o = o_s[slot]
        o = jnp.concatenate([_rms(_heads(o, h), nw) for h in heads], axis=-1)
        o_ref[r, :] = (o * _silu(dz_ref[r, :])).astype(BF16)

    def scan_chunk(n, slot):
        r = pl.ds(pl.multiple_of(n * CHUNK, CHUNK), CHUNK)
        u = u_s[r, :]
        w = w_s[r, :]
        qe = qe_s[r, :]
        kt = kt_s[r, :]
        qk = qk_s[r, :]
        cd = cd_s[n]
        state = [st_s[h] for h in heads]
        sb = [s.astype(BF16) for s in state]
        ws = [_dot(_heads(w, h), sb[h]) for h in heads]
        qs = [_dot(_heads(qe, h), sb[h]) for h in heads]
        v_new = [(_heads(u, h) - ws[h]).astype(BF16) for h in heads]
        o = [qs[h] + _dot(qk[:, h * CHUNK:(h + 1) * CHUNK], v_new[h]) for h in heads]
        for h in heads:
            st_s[h] = state[h] * _heads(cd, h)[0:1] + _dot_tn(_heads(kt, h), v_new[h])
        o_s[slot] = jnp.concatenate(o, axis=-1)

    scan_chunk(0, 0)

    def phase_b(j, carry):
        n = 2 * j
        finish(n, 0)
        scan_chunk(n + 1, 1)
        finish(n + 1, 1)
        scan_chunk(n + 2, 0)
        return carry

    lax.fori_loop(0, (N_CHUNKS - 2) // 2, phase_b, 0)
    finish(N_CHUNKS - 2, 0)
    scan_chunk(N_CHUNKS - 1, 1)
    finish(N_CHUNKS - 1, 1)


def _gdn(l, proj_bf, proj_f32, proj_small, conv_w, alog_pad, dtb_pad, nw):
    full_bf = pltpu.VMEM((SEQ, GDN_W), BF16)
    slot_wide = pltpu.VMEM((2, GDN_ROWS, GDN_W), F32)
    slot_narrow = pltpu.VMEM((2, GDN_ROWS, LANES), F32)
    return pl.pallas_call(
        _gdn_kernel,
        grid=(BATCH,),
        in_specs=[
            pl.BlockSpec((SEQ, GDN_W), lambda b: (b, 8)),
            pl.BlockSpec((SEQ, GDN_W), lambda b: (b, 9)),
            pl.BlockSpec((SEQ, GDN_W), lambda b: (b, 10)),
            pl.BlockSpec((SEQ, GDN_W), lambda b: (b, 1)),
            pl.BlockSpec((SEQ, LANES), lambda b: (b, DAB_WINDOW)),
            pl.BlockSpec((1, CONV_K, 3 * GDN_W), lambda b: (l, 0, 0)),
            pl.BlockSpec((1, 1, LANES), lambda b: (l, 0, 0)),
            pl.BlockSpec((1, 1, LANES), lambda b: (l, 0, 0)),
            pl.BlockSpec((1, 1, GDN_D), lambda b: (l, 0, 0)),
        ],
        out_specs=pl.BlockSpec((SEQ, GDN_W), lambda b: (b, 0)),
        out_shape=jax.ShapeDtypeStruct((TOKENS, GDN_W), BF16),
        scratch_shapes=[slot_wide, slot_wide, slot_wide,
                        slot_narrow, slot_narrow,
                        pltpu.VMEM((SEQ, GDN_W), F32),
                        full_bf, full_bf, full_bf,
                        pltpu.VMEM((SEQ, GDN_HEADS * CHUNK), BF16),
                        pltpu.VMEM((N_CHUNKS, 8, GDN_W), F32),
                        pltpu.VMEM((2, CHUNK, GDN_W), F32),
                        pltpu.VMEM((GDN_HEADS, GDN_D, GDN_D), F32)],
        compiler_params=_params("arbitrary"),
        name="gdn",
    )(proj_bf, proj_bf, proj_bf, proj_f32, proj_small, conv_w, alog_pad, dtb_pad, nw)


PREP_TS = 512
ATT_TQ = 256


def _diff_prep_kernel(aq_ref, ak_ref, cos_ref, sin_ref, qw_ref, kw_ref, q_ref, k_ref):
    cos2 = cos_ref[0]
    sin2 = sin_ref[0]
    for src, w_ref, dst, scale in ((aq_ref, qw_ref, q_ref, DIFF_D ** -0.5 * LOG2E), (ak_ref, kw_ref, k_ref, 1.0)):
        w = w_ref[0]
        for g in range(2 * DIFF_HEADS):
            cols = slice(g * DIFF_D, (g + 1) * DIFF_D)
            y = _rms(src[:, cols].astype(F32), w)
            y = y * cos2 + pltpu.roll(y, DIFF_D // 2, 1) * sin2
            dst[:, cols] = (y * scale).astype(BF16)


def _diff_prep(l, proj_bf, cos2, sin2, qw, kw):
    per_batch = SEQ // PREP_TS
    width = 2 * DIFF_HEADS * DIFF_D
    tab_spec = pl.BlockSpec((1, PREP_TS, DIFF_D), lambda i: (i // per_batch, i % per_batch, 0))
    out = jax.ShapeDtypeStruct((TOKENS, width), BF16)
    return pl.pallas_call(
        _diff_prep_kernel,
        grid=(TOKENS // PREP_TS,),
        in_specs=[
            pl.BlockSpec((PREP_TS, width), lambda i: (i, 0)),
            pl.BlockSpec((PREP_TS, width), lambda i: (i, 1)),
            tab_spec, tab_spec,
            pl.BlockSpec((1, 1, DIFF_D), lambda i: (l, 0, 0)),
            pl.BlockSpec((1, 1, DIFF_D), lambda i: (l, 0, 0)),
        ],
        out_specs=[pl.BlockSpec((PREP_TS, width), lambda i: (i, 0))] * 2,
        out_shape=[out, out],
        compiler_params=_params("arbitrary"),
        name="diff_prep",
    )(proj_bf, proj_bf, cos2, sin2, qw, kw)


def _diff_attn_kernel(q1_ref, q2_ref, k1_ref, k2_ref, v_ref, az_ref, lam_ref, nw_ref, o_ref, *, lam_init):
    lv = lam_ref[0]
    lam = (jnp.exp(jnp.sum(lv[0:1] * lv[1:2], axis=-1, keepdims=True))
           - jnp.exp(jnp.sum(lv[2:3] * lv[3:4], axis=-1, keepdims=True)) + lam_init)
    nw = nw_ref[0]
    row = lax.broadcasted_iota(jnp.int32, (ATT_TQ, ATT_TQ), 0)
    col = lax.broadcasted_iota(jnp.int32, (ATT_TQ, ATT_TQ), 1)
    causal = col <= row
    q_refs = (q1_ref, q2_ref)
    k_refs = (k1_ref, k2_ref)
    n_blocks = SEQ // ATT_TQ

    def scores(i):
        start = i * ATT_TQ
        rows = slice(start, start + ATT_TQ)
        out = []
        for m in range(2):
            q = q_refs[m][rows, :]
            s_diag = jnp.where(causal, _dot_nt(q, k_refs[m][rows, :]), -jnp.inf)
            s_past = _dot_nt(q, k_refs[m][0:start, :]) if i > 0 else None
            out.append((s_diag, s_past))
        return out

    s_next = scores(0)
    for i in range(n_blocks):
        start = i * ATT_TQ
        rows = slice(start, start + ATT_TQ)
        s_cur = s_next
        if i + 1 < n_blocks:
            s_next = scores(i + 1)
        ps, inv_l = [], []
        for s_diag, s_past in s_cur:
            mx = jnp.max(s_diag, axis=-1, keepdims=True)
            if i > 0:
                mx = jnp.maximum(mx, jnp.max(s_past, axis=-1, keepdims=True))
            p_diag = jnp.exp2(s_diag - mx)
            l = jnp.sum(p_diag, axis=-1, keepdims=True)
            p_past = None
            if i > 0:
                p_past = jnp.exp2(s_past - mx)
                l = l + jnp.sum(p_past, axis=-1, keepdims=True)
                p_past = p_past.astype(BF16)
            ps.append((p_diag.astype(BF16), p_past))
            inv_l.append(1.0 / l)
        pv = []
        for p_diag, p_past in ps:
            acc = _dot(p_diag, v_ref[rows, :])
            if i > 0:
                acc = acc + _dot(p_past, v_ref[0:start, :])
            pv.append(acc)
        o = pv[0] * inv_l[0] - pv[1] * (lam * inv_l[1])
        o = _rms(o, nw) * (1.0 - lam_init)
        o_ref[rows, :] = (o * _silu(az_ref[rows, :])).astype(BF16)


def _diff_attn(l, q_d, k_d, proj_bf, proj_f32, lam, nw):
    lam_init = 0.8 - 0.6 * math.exp(-0.3 * l)
    return pl.pallas_call(
        functools.partial(_diff_attn_kernel, lam_init=lam_init),
        grid=(BATCH, DIFF_HEADS),
        in_specs=[
            pl.BlockSpec((SEQ, DIFF_D), lambda b, h: (b, 2 * h)),
            pl.BlockSpec((SEQ, DIFF_D), lambda b, h: (b, 2 * h + 1)),
            pl.BlockSpec((SEQ, DIFF_D), lambda b, h: (b, 2 * h)),
            pl.BlockSpec((SEQ, DIFF_D), lambda b, h: (b, 2 * h + 1)),
            pl.BlockSpec((SEQ, DIFF_DV), lambda b, h: (b, 8 + h)),
            pl.BlockSpec((SEQ, DIFF_DV), lambda b, h: (b, 4 + h)),
            pl.BlockSpec((1, 4, DIFF_D), lambda b, h: (l, 0, 0)),
            pl.BlockSpec((1, 1, DIFF_DV), lambda b, h: (l, 0, 0)),
        ],
        out_specs=pl.BlockSpec((SEQ, DIFF_DV), lambda b, h: (b, h)),
        out_shape=jax.ShapeDtypeStruct((TOKENS, DIFF_W), BF16),
        compiler_params=_params("arbitrary", "arbitrary"),
        name="diff_attn",
    )(q_d, q_d, k_d, k_d, proj_bf, proj_f32, lam, nw)


OUT_TM = 512


OUT_WROWS = 256


def _outproj_kernel(og_ref, od_ref, oa_ref, w_ref, x_ref, gate_ref, o_ref, wb_ref):
    @pl.when(pl.program_id(0) == 0)
    def _():
        def cast(i, carry):
            rows = pl.ds(pl.multiple_of(i * OUT_WROWS, OUT_WROWS), OUT_WROWS)
            wb_ref[rows, :] = w_ref[0, rows, :].astype(BF16)
            return carry

        lax.fori_loop(0, D_MODEL // OUT_WROWS, cast, 0)

    y = _dot(og_ref[...], wb_ref[0:GLA_W, :])
    y = y + _dot(od_ref[...], wb_ref[GLA_W:GLA_W + GDN_W, :])
    y = y + _dot(oa_ref[...], wb_ref[GLA_W + GDN_W:, :])
    o_ref[...] = x_ref[...] + gate_ref[0, 0] * y


def _out_proj(l, o_gla, o_gdn, o_diff, w_out, x2d, mods4):
    per_batch = SEQ // OUT_TM
    return pl.pallas_call(
        _outproj_kernel,
        grid=(TOKENS // OUT_TM,),
        in_specs=[
            pl.BlockSpec((OUT_TM, GLA_W), lambda m: (m, 0)),
            pl.BlockSpec((OUT_TM, GDN_W), lambda m: (m, 0)),
            pl.BlockSpec((OUT_TM, DIFF_W), lambda m: (m, 0)),
            pl.BlockSpec((1, D_MODEL, D_MODEL), lambda m: (l, 0, 0), pipeline_mode=pl.Buffered(1)),
            pl.BlockSpec((OUT_TM, D_MODEL), lambda m: (m, 0)),
            pl.BlockSpec((1, 1, 1, D_MODEL), lambda m: (l, m // per_batch, 0, 2)),
        ],
        out_specs=pl.BlockSpec((OUT_TM, D_MODEL), lambda m: (m, 0)),
        out_shape=jax.ShapeDtypeStruct((TOKENS, D_MODEL), F32),
        scratch_shapes=[pltpu.VMEM((D_MODEL, D_MODEL), BF16)],
        compiler_params=_params("arbitrary", vmem=52 * 1024 * 1024),
        name="out_proj",
    )(o_gla, o_gdn, o_diff, w_out, x2d, mods4)


def _lane_place(vecs, lane):
    n = vecs.shape[1]
    return jnp.pad(vecs.astype(F32), ((0, 0), (lane, LANES - lane - n))).reshape(DEPTH, 1, LANES)


def kernel(x, c, positions, norm_w, w_ada, b_ada, w_in, gla_w_lr, gla_b_lr, gla_norm_w, gdn_conv_w, gdn_a_log,
           gdn_dt_bias, gdn_norm_w, diff_q_norm_w, diff_k_norm_w, diff_lambda, diff_norm_w, w_out):
    c_pad = jnp.pad(c, ((0, 8 - BATCH), (0, 0)))
    mods4 = _ada_mod(c_pad, w_ada, b_ada).reshape(DEPTH, 8, 1, 3 * D_MODEL)
    cos2, sin2 = _rope_tables(positions)
    w_in_t = jnp.swapaxes(w_in, 1, 2)
    row3 = lambda p: p.reshape(DEPTH, 1, p.shape[-1])
    norm_w3 = row3(norm_w)
    wlr_pad = jnp.pad(gla_w_lr, ((0, 0), (GLR_LANE, LANES - GLR_LANE - GLA_RANK), (0, 0)))
    alog_pad = _lane_place(gdn_a_log, DA_LANE)
    dtb_pad = _lane_place(gdn_dt_bias, DA_LANE)
    x2d = x.reshape(TOKENS, D_MODEL)
    for l in range(DEPTH):
        proj_bf, proj_f32, proj_small = _in_proj(l, x2d, norm_w3, mods4, w_in_t)
        o_gla = _gla(l, proj_bf, proj_f32, proj_small, wlr_pad, row3(gla_b_lr), row3(gla_norm_w))
        o_gdn = _gdn(l, proj_bf, proj_f32, proj_small, gdn_conv_w, alog_pad, dtb_pad, row3(gdn_norm_w))
        q_d, k_d = _diff_prep(l, proj_bf, cos2, sin2, row3(diff_q_norm_w), row3(diff_k_norm_w))
        o_diff = _diff_attn(l, q_d, k_d, proj_bf, proj_f32, diff_lambda, row3(diff_norm_w))
        x2d = _out_proj(l, o_gla, o_gdn, o_diff, w_out, x2d, mods4)
    return x2d.reshape(BATCH, SEQ, D_MODEL)
```

```python
import functools
import math

import jax
import jax.numpy as jnp
from jax import lax
from jax.experimental import pallas as pl
from jax.experimental.pallas import tpu as pltpu

F32 = jnp.float32
BF16 = jnp.bfloat16

D_MODEL = 2048
BATCH = 4
SEQ = 2048
DEPTH = 2
TOKENS = BATCH * SEQ

GLA_HEADS = 4
GLA_DK = 64
GLA_DV = 128
GLA_W = GLA_HEADS * GLA_DV
GLA_RANK = 16
GLA_TAU = 16.0
GDN_HEADS = 4
GDN_D = 128
GDN_W = GDN_HEADS * GDN_D
CONV_K = 4
DIFF_HEADS = 4
DIFF_D = 128
DIFF_DV = 256
DIFF_W = DIFF_HEADS * DIFF_DV
CHUNK = 64
N_CHUNKS = SEQ // CHUNK
ROPE_THETA = 10000.0
EPS = 1e-6
LOG2E = math.log2(math.e)

LANES = 128

_IN_SPLITS = (
    ("gq", GLA_HEADS * GLA_DK), ("gk", GLA_HEADS * GLA_DK), ("gv", GLA_W), ("glr", GLA_RANK), ("gz", GLA_W),
    ("dq", GDN_W), ("dk", GDN_W), ("dv", GDN_W), ("da", GDN_HEADS), ("db", GDN_HEADS), ("dz", GDN_W),
    ("aq", DIFF_HEADS * 2 * DIFF_D), ("ak", DIFF_HEADS * 2 * DIFF_D), ("av", DIFF_W), ("az", DIFF_W),
)
_IN_OFFSETS = {}
_off = 0
for _name, _width in _IN_SPLITS:
    _IN_OFFSETS[_name] = (_off, _off + _width)
    _off += _width
D_IN = _off

_BF_ORDER = ("aq", "ak", "av", "gq", "gk", "gv", "dq", "dk", "dv")
_F32_ORDER = ("gz", "dz", "az")
N_BF = 5632
N_F32 = 2048
PROJ_TN = 512
NB_BF = N_BF // PROJ_TN
NB_F32 = N_F32 // PROJ_TN
GLR_SRC = (_IN_OFFSETS["glr"][0] // LANES) * LANES
DAB_SRC = (_IN_OFFSETS["da"][0] // LANES) * LANES
GLR_WINDOW = 0
DAB_WINDOW = 1
GLR_LANE = _IN_OFFSETS["glr"][0] - GLR_SRC
DA_LANE = _IN_OFFSETS["da"][0] - DAB_SRC
DB_LANE = _IN_OFFSETS["db"][0] - DAB_SRC
assert GLR_LANE + GLA_RANK <= LANES and DB_LANE + GDN_HEADS <= LANES

VMEM_LIMIT = 48 * 1024 * 1024


def _sigmoid(x):
    return 1.0 / (1.0 + jnp.exp(-x))


def _silu(x):
    return x * _sigmoid(x)


def _softplus(x):
    return jnp.maximum(x, 0.0) + jnp.log1p(jnp.exp(-jnp.abs(x)))


def _dot(a, b):
    return jnp.dot(a, b, preferred_element_type=F32)


def _dot_nt(a, b):
    return lax.dot_general(a, b, (((1,), (1,)), ((), ())), preferred_element_type=F32)


def _dot_tn(a, b):
    return lax.dot_general(a, b, (((0,), (0,)), ((), ())), preferred_element_type=F32)


def _dot_f32(a, b):
    return jnp.dot(a, b, precision=lax.Precision.HIGHEST, preferred_element_type=F32)


def _dot_x3(a, b):
    a_hi = a.astype(BF16)
    a_lo = (a - a_hi.astype(F32)).astype(BF16)
    b_hi = b.astype(BF16)
    b_lo = (b - b_hi.astype(F32)).astype(BF16)
    return _dot(a_hi, b_hi) + (_dot(a_hi, b_lo) + _dot(a_lo, b_hi))


def _rms(x, w):
    return x * lax.rsqrt(jnp.mean(x * x, axis=-1, keepdims=True) + EPS) * w


def _params(*semantics, vmem=VMEM_LIMIT):
    return pltpu.CompilerParams(dimension_semantics=semantics, vmem_limit_bytes=vmem)


ADA_TN = 1024


def _ada_kernel(c_ref, w_ref, b_ref, o_ref):
    c_act = _silu(c_ref[...])
    o_ref[0] = _dot(c_act.astype(BF16), w_ref[0].astype(BF16)) + b_ref[0]


def _ada_mod(c_pad, w_ada, b_ada):
    n3 = 3 * D_MODEL
    return pl.pallas_call(
        _ada_kernel,
        grid=(DEPTH, n3 // ADA_TN),
        in_specs=[
            pl.BlockSpec((8, D_MODEL), lambda l, n: (0, 0)),
            pl.BlockSpec((1, D_MODEL, ADA_TN), lambda l, n: (l, 0, n)),
            pl.BlockSpec((1, 1, ADA_TN), lambda l, n: (l, 0, n)),
        ],
        out_specs=pl.BlockSpec((1, 8, ADA_TN), lambda l, n: (l, 0, n)),
        out_shape=jax.ShapeDtypeStruct((DEPTH, 8, n3), F32),
        compiler_params=_params("arbitrary", "arbitrary"),
        name="ada_mod",
    )(c_pad, w_ada, b_ada.reshape(DEPTH, 1, n3))


def _rope_kernel(pos_ref, freq_ref, sign_ref, cos_ref, sin_ref):
    ang = pos_ref[0].astype(F32) * freq_ref[...]
    cos_ref[0] = jnp.cos(ang)
    sin_ref[0] = jnp.sin(ang) * sign_ref[...]


def _rope_tables(positions):
    half = DIFF_D // 2
    inv_freq = ROPE_THETA ** (-jnp.arange(half, dtype=F32) / half)
    freq2 = jnp.concatenate([inv_freq, inv_freq]).reshape(1, DIFF_D)
    sign = jnp.concatenate([-jnp.ones((half,), F32), jnp.ones((half,), F32)]).reshape(1, DIFF_D)
    tab = jax.ShapeDtypeStruct((BATCH, SEQ, DIFF_D), F32)
    return pl.pallas_call(
        _rope_kernel,
        grid=(BATCH,),
        in_specs=[
            pl.BlockSpec((1, SEQ, 1), lambda b: (b, 0, 0)),
            pl.BlockSpec((1, DIFF_D), lambda b: (0, 0)),
            pl.BlockSpec((1, DIFF_D), lambda b: (0, 0)),
        ],
        out_specs=[pl.BlockSpec((1, SEQ, DIFF_D), lambda b: (b, 0, 0))] * 2,
        out_shape=[tab, tab],
        compiler_params=_params("arbitrary"),
        name="rope_tables",
    )(positions.reshape(BATCH, SEQ, 1), freq2, sign)


PROJ_TM = 1024
PROJ_ROWS = 256


def _inproj_kernel(src_ref, x_ref, nw_ref, shift_ref, scale_ref, w_ref, wg_ref, wd_ref,
                   ob_ref, of_ref, os_ref, h_ref):
    del src_ref
    n = pl.program_id(1)

    @pl.when(n == 0)
    def _():
        gain = nw_ref[0] * (1.0 + scale_ref[0, 0])
        shift = shift_ref[0, 0]
        w = w_ref[...].astype(BF16)
        w_small = jnp.concatenate([wg_ref[...], wd_ref[...]], axis=0).astype(BF16)
        for i in range(PROJ_TM // PROJ_ROWS):
            rows = slice(i * PROJ_ROWS, (i + 1) * PROJ_ROWS)
            x = x_ref[rows, :]
            h = (x * lax.rsqrt(jnp.mean(x * x, axis=-1, keepdims=True) + EPS) * gain + shift).astype(BF16)
            h_ref[rows, :] = h
            ob_ref[rows, :] = _dot_nt(h, w).astype(BF16)
            os_ref[rows, :] = _dot_nt(h, w_small)

    @pl.when(jnp.logical_and(n > 0, n < NB_BF))
    def _():
        ob_ref[...] = _dot_nt(h_ref[...], w_ref[...].astype(BF16)).astype(BF16)

    @pl.when(n >= NB_BF)
    def _():
        of_ref[...] = _dot_nt(h_ref[...], w_ref[...].astype(BF16))


def _w_in_sources():
    starts = []
    for name in _BF_ORDER + _F32_ORDER:
        lo, hi = _IN_OFFSETS[name]
        starts += list(range(lo, hi, 256))
    firsts = starts[0::2]
    assert all(b == a + 256 for a, b in zip(firsts, starts[1::2])), "each 512 block must be contiguous"
    assert len(firsts) * PROJ_TN == N_BF + N_F32
    return firsts


def _in_proj(l, x2d, norm_w3, mods4, w_in_t):
    per_batch = SEQ // PROJ_TM
    sublanes = 8
    assert all(s % sublanes == 0 for s in _w_in_sources())
    starts = jnp.asarray([s // sublanes for s in _w_in_sources()], jnp.int32)

    def window(first_row):
        return pl.BlockSpec((pl.Squeezed(), pl.Element(LANES), pl.Element(D_MODEL)),
                            lambda m, n, src: (l, first_row, 0))

    grid_spec = pltpu.PrefetchScalarGridSpec(
        num_scalar_prefetch=1,
        grid=(TOKENS // PROJ_TM, NB_BF + NB_F32),
        in_specs=[
            pl.BlockSpec((PROJ_TM, D_MODEL), lambda m, n, src: (m, 0)),
            pl.BlockSpec((1, 1, D_MODEL), lambda m, n, src: (l, 0, 0)),
            pl.BlockSpec((1, 1, 1, D_MODEL), lambda m, n, src: (l, m // per_batch, 0, 0)),
            pl.BlockSpec((1, 1, 1, D_MODEL), lambda m, n, src: (l, m // per_batch, 0, 1)),
            pl.BlockSpec((pl.Squeezed(), pl.Element(PROJ_TN), pl.Element(D_MODEL)),
                         lambda m, n, src: (l, src[n] * sublanes, 0)),
            window(GLR_SRC),
            window(DAB_SRC),
        ],
        out_specs=[
            pl.BlockSpec((PROJ_TM, PROJ_TN), lambda m, n, src: (m, jnp.minimum(n, NB_BF - 1))),
            pl.BlockSpec((PROJ_TM, PROJ_TN), lambda m, n, src: (m, jnp.maximum(n - NB_BF, 0))),
            pl.BlockSpec((PROJ_TM, 2 * LANES), lambda m, n, src: (m, 0)),
        ],
        scratch_shapes=[pltpu.VMEM((PROJ_TM, D_MODEL), BF16)],
    )
    return pl.pallas_call(
        _inproj_kernel,
        grid_spec=grid_spec,
        out_shape=[
            jax.ShapeDtypeStruct((TOKENS, N_BF), BF16),
            jax.ShapeDtypeStruct((TOKENS, N_F32), F32),
            jax.ShapeDtypeStruct((TOKENS, 2 * LANES), F32),
        ],
        compiler_params=_params("arbitrary", "arbitrary"),
        name="in_proj",
    )(starts, x2d, norm_w3, mods4, mods4, w_in_t, w_in_t, w_in_t)


GLA_GROUP = 4


def _gla_kernel(gq_ref, gk_ref, gv_ref, gz_ref, sm_ref, wlr_ref, blr_ref, nw_ref, o_ref, la_ref):
    z = _dot_f32(sm_ref[...], wlr_ref[0]) + blr_ref[0]
    la_ref[...] = -_softplus(-z) * (1.0 / GLA_TAU)

    row = lax.broadcasted_iota(jnp.int32, (CHUNK, CHUNK), 0)
    col = lax.broadcasted_iota(jnp.int32, (CHUNK, CHUNK), 1)
    incl = col <= row
    tril = jnp.where(incl, 1.0, 0.0)
    nw = nw_ref[0]
    heads = range(GLA_HEADS)
    group = range(GLA_GROUP)

    def kcols(x, h):
        return x[:, h * GLA_DK:(h + 1) * GLA_DK]

    def vcols(x, h):
        return x[:, h * GLA_DV:(h + 1) * GLA_DV]

    def body(gi, states):
        rs = [pl.ds(pl.multiple_of((gi * GLA_GROUP + c) * CHUNK, CHUNK), CHUNK) for c in group]
        bc = [_dot_f32(tril, la_ref[r, :]) for r in rs]
        qe, ke, kt, dec, v = [], [], [], [], []
        for c in group:
            b_last = bc[c][CHUNK - 1:CHUNK, :]
            q = gq_ref[rs[c], :].astype(F32) * GLA_DK ** -0.5
            k = gk_ref[rs[c], :].astype(F32)
            qe.append((q * jnp.exp(bc[c])).astype(BF16))
            ke.append((k * jnp.exp(-bc[c])).astype(BF16))
            kt.append((k * jnp.exp(b_last - bc[c])).astype(BF16))
            dec.append(jnp.exp(b_last))
            v.append(gv_ref[rs[c], :])
        att = [[jnp.where(incl, _dot_nt(kcols(qe[c], h), kcols(ke[c], h)), 0.0).astype(BF16) for h in heads]
               for c in group]
        o_intra = [[_dot(att[c][h], vcols(v[c], h)) for h in heads] for c in group]
        d_state = [[_dot_tn(vcols(v[c], h), kcols(kt[c], h)) for h in heads] for c in group]
        st = list(states)
        for c in group:
            gz = gz_ref[rs[c], :]
            outs = []
            for h in heads:
                o = o_intra[c][h] + _dot_nt(kcols(qe[c], h), st[h].astype(BF16))
                st[h] = st[h] * kcols(dec[c], h) + d_state[c][h]
                outs.append(_rms(o, nw) * _silu(vcols(gz, h)))
            o_ref[rs[c], :] = jnp.concatenate(outs, axis=-1).astype(BF16)
        return tuple(st)

    lax.fori_loop(0, N_CHUNKS // GLA_GROUP, body,
                  tuple(jnp.zeros((GLA_DV, GLA_DK), F32) for _ in heads))


def _gla(l, proj_bf, proj_f32, proj_small, wlr_pad, blr, nw):
    return pl.pallas_call(
        _gla_kernel,
        grid=(BATCH,),
        in_specs=[
            pl.BlockSpec((SEQ, 256), lambda b: (b, 12)),
            pl.BlockSpec((SEQ, 256), lambda b: (b, 13)),
            pl.BlockSpec((SEQ, GLA_W), lambda b: (b, 7)),
            pl.BlockSpec((SEQ, GLA_W), lambda b: (b, 0)),
            pl.BlockSpec((SEQ, LANES), lambda b: (b, GLR_WINDOW)),
            pl.BlockSpec((1, LANES, GLA_HEADS * GLA_DK), lambda b: (l, 0, 0)),
            pl.BlockSpec((1, 1, GLA_HEADS * GLA_DK), lambda b: (l, 0, 0)),
            pl.BlockSpec((1, 1, GLA_DV), lambda b: (l, 0, 0)),
        ],
        out_specs=pl.BlockSpec((SEQ, GLA_W), lambda b: (b, 0)),
        out_shape=jax.ShapeDtypeStruct((TOKENS, GLA_W), BF16),
        scratch_shapes=[pltpu.VMEM((SEQ, GLA_HEADS * GLA_DK), F32)],
        compiler_params=_params("arbitrary"),
        name="gla",
    )(proj_bf, proj_bf, proj_bf, proj_f32, proj_small, wlr_pad, blr, nw)


GDN_HALO = 16
GDN_GROUP = 2
GDN_ROWS = GDN_GROUP * CHUNK
N_GROUPS = N_CHUNKS // GDN_GROUP


def _heads(x, h):
    return x[:, h * GDN_D:(h + 1) * GDN_D]


def _gdn_kernel(dq_ref, dk_ref, dv_ref, dz_ref, sm_ref, cw_ref, alog_ref, dtb_ref, nw_ref, o_ref,
                pq_s, pk_s, pv_s, pg_s, pb_s, u_s, w_s, qe_s, kt_s, qk_s, cd_s, st_s):
    heads = range(GDN_HEADS)
    row = lax.broadcasted_iota(jnp.int32, (CHUNK, CHUNK), 0)
    col = lax.broadcasted_iota(jnp.int32, (CHUNK, CHUNK), 1)
    incl = col <= row
    strict = col < row
    tril = jnp.where(incl, 1.0, 0.0)
    eye = jnp.where(col == row, 1.0, 0.0)
    block_bits = jnp.bitwise_xor(row, col)
    nw = nw_ref[0]
    cw = cw_ref[0]
    neg_a = -jnp.exp(alog_ref[0])
    dtb = dtb_ref[0]

    def conv(x_ref, src, w, first_block):
        cur = x_ref[pl.ds(src, GDN_ROWS), :].astype(F32)
        if first_block:
            prev = jnp.zeros((GDN_HALO, GDN_W), F32)
        else:
            prev = x_ref[pl.ds(pl.multiple_of(src - GDN_HALO, GDN_HALO), GDN_HALO), :].astype(F32)
        win = jnp.concatenate([prev, cur], axis=0)
        first = GDN_HALO - (CONV_K - 1)
        acc = win[first:first + GDN_ROWS] * w[0:1]
        for j in range(1, CONV_K):
            acc = acc + win[first + j:first + j + GDN_ROWS] * w[j:j + 1]
        return _silu(acc)

    def l2n(x, scale):
        parts = []
        for h in heads:
            xh = _heads(x, h)
            parts.append(xh * (lax.rsqrt(jnp.sum(xh * xh, axis=-1, keepdims=True) + EPS) * scale))
        return jnp.concatenate(parts, axis=-1)

    def group_rows(g):
        return pl.multiple_of(g * GDN_ROWS, GDN_ROWS)

    def phase0_steps(g, slot, first_block=False):
        src = 0 if first_block else group_rows(g)

        def q_step():
            pq_s[slot] = l2n(conv(dq_ref, src, cw[:, 0:GDN_W], first_block), GDN_D ** -0.5)

        def k_step():
            pk_s[slot] = l2n(conv(dk_ref, src, cw[:, GDN_W:2 * GDN_W], first_block), 1.0)

        def v_step():
            pv_s[slot] = conv(dv_ref, src, cw[:, 2 * GDN_W:3 * GDN_W], first_block)

        def gate_step():
            sm = sm_ref[pl.ds(src, GDN_ROWS), :]
            pg_s[slot] = neg_a * _softplus(sm + dtb)
            pb_s[slot] = _sigmoid(sm)

        return [q_step, k_step, v_step, gate_step]

    for step in phase0_steps(0, 0, first_block=True):
        step()

    def wy_group(slot, fillers):
        fillers = list(fillers)

        def emit():
            if fillers:
                fillers.pop(0)()

        group = range(GDN_GROUP)
        units = [(c, h) for c in group for h in heads]
        crows = [slice(c * CHUNK, (c + 1) * CHUNK) for c in group]
        gcs = [_dot_f32(tril, pg_s[slot, cr, :]) for cr in crows]
        gcs_t = [x.T for x in gcs]
        beta_all = [pb_s[slot, cr, :] for cr in crows]
        qn = [pq_s[slot, cr, :] for cr in crows]
        kn = [pk_s[slot, cr, :] for cr in crows]
        v = [pv_s[slot, cr, :] for cr in crows]
        decay, eg, ekt, cd, k_beta, v_beta, kb = {}, {}, {}, {}, {}, {}, {}
        for c, h in units:
            gc_col = gcs[c][:, DA_LANE + h:DA_LANE + h + 1]
            gc_row = gcs_t[c][DA_LANE + h:DA_LANE + h + 1, :]
            decay[c, h] = jnp.exp(jnp.where(incl, gc_col - gc_row, -jnp.inf))
            gcb = jnp.broadcast_to(gc_col, (CHUNK, GDN_D))
            g_last = gcb[CHUNK - 1:CHUNK, :]
            eg[c, h] = jnp.exp(gcb)
            ekt[c, h] = jnp.exp(g_last - gcb)
            cd[c, h] = jnp.broadcast_to(jnp.exp(g_last), (8, GDN_D))
            b_col = beta_all[c][:, DB_LANE + h:DB_LANE + h + 1]
            k_beta[c, h] = _heads(kn[c], h) * b_col
            v_beta[c, h] = _heads(v[c], h) * b_col
            kb[c, h] = _heads(kn[c], h).astype(BF16)
        lower = {u: jnp.where(strict, _dot_nt(k_beta[u].astype(BF16), kb[u]) * decay[u], 0.0) for u in units}
        emit()
        base = 8
        in_base = lax.shift_right_logical(block_bits, int(math.log2(base))) == 0
        pw = {u: jnp.where(in_base, -lower[u], 0.0) for u in units}
        tq = dict(pw)
        pw = {u: _dot_x3(pw[u], pw[u]) for u in units}
        emit()
        both = {u: _dot_x3(jnp.concatenate([tq[u], pw[u]], axis=0), pw[u]) for u in units}
        emit()
        tq = {u: tq[u] + pw[u] + both[u][0:CHUNK] for u in units}
        pw = {u: both[u][CHUNK:2 * CHUNK] for u in units}
        inv = {u: eye + (tq[u] + pw[u] + _dot_x3(tq[u], pw[u])) for u in units}
        emit()
        for size in (base, 2 * base, 4 * base):
            inside = lax.shift_right_logical(block_bits, int(math.log2(size))) == 1
            nt = {u: _dot_x3(jnp.where(inside, lower[u], 0.0), inv[u]) for u in units}
            emit()
            inv = {u: inv[u] - _dot_x3(inv[u], nt[u]) for u in units}
            emit()
        rhs = {u: jnp.concatenate([v_beta[u], k_beta[u] * eg[u]], axis=-1) for u in units}
        sol = {u: rhs[u] + _dot((inv[u] - eye).astype(BF16), rhs[u].astype(BF16)) for u in units}
        emit()
        qk = {(c, h): _dot_nt(_heads(qn[c], h).astype(BF16), kb[c, h]) * decay[c, h] for c, h in units}
        while fillers:
            emit()
        for c in group:
            cr = crows[c]
            u_s[slot, cr, :] = jnp.concatenate([sol[c, h][:, 0:GDN_D] for h in heads], axis=-1)
            w_s[slot, cr, :] = jnp.concatenate([sol[c, h][:, GDN_D:2 * GDN_D] for h in heads],
                                               axis=-1).astype(BF16)
            qe_s[slot, cr, :] = jnp.concatenate([_heads(qn[c], h) * eg[c, h] for h in heads],
                                                axis=-1).astype(BF16)
            kt_s[slot, cr, :] = jnp.concatenate([_heads(kn[c], h) * ekt[c, h] for h in heads],
                                                axis=-1).astype(BF16)
            qk_s[slot, cr, :] = jnp.concatenate([qk[c, h] for h in heads], axis=-1).astype(BF16)
            cd_s[slot, c] = jnp.concatenate([cd[c, h] for h in heads], axis=-1)

    def scan_steps(g, slot):
        box = {}
        steps = []
        for c in range(GDN_GROUP):
            cr = slice(c * CHUNK, (c + 1) * CHUNK)

            def first(c=c, cr=cr):
                state = [st_s[h] for h in heads] if c == 0 else box["state"]
                sb = [s.astype(BF16) for s in state]
                w = w_s[slot, cr, :]
                qe = qe_s[slot, cr, :]
                box["state"] = state
                box["ws"] = [_dot(_heads(w, h), sb[h]) for h in heads]
                box["qs"] = [_dot(_heads(qe, h), sb[h]) for h in heads]

            def second(c=c, cr=cr):
                u = u_s[slot, cr, :]
                kt = kt_s[slot, cr, :]
                qk = qk_s[slot, cr, :]
                cd = cd_s[slot, c]
                v_new = [(_heads(u, h) - box["ws"][h]).astype(BF16) for h in heads]
                o = [box["qs"][h] + _dot(qk[:, h * CHUNK:(h + 1) * CHUNK], v_new[h]) for h in heads]
                state = [box["state"][h] * _heads(cd, h)[0:1] + _dot_tn(_heads(kt, h), v_new[h]) for h in heads]
                if c == GDN_GROUP - 1:
                    for h in heads:
                        st_s[h] = state[h]
                box["state"] = state
                r = pl.ds(pl.multiple_of((g * GDN_GROUP + c) * CHUNK, CHUNK), CHUNK)
                o = jnp.concatenate([_rms(o[h], nw) for h in heads], axis=-1)
                o_ref[r, :] = (o * _silu(dz_ref[r, :])).astype(BF16)

            steps += [first, second]
        return steps

    def interleave(a, b):
        out = []
        for i in range(max(len(a), len(b))):
            out += a[i:i + 1] + b[i:i + 1]
        return out

    st_s[...] = jnp.zeros((GDN_HEADS, GDN_D, GDN_D), F32)
    wy_group(0, phase0_steps(1, 1))

    def pair(i, carry):
        g = 2 * i
        wy_group(1, interleave(scan_steps(g, 0), phase0_steps(g + 2, 0)))
        wy_group(0, interleave(scan_steps(g + 1, 1), phase0_steps(g + 3, 1)))
        return carry

    lax.fori_loop(0, (N_GROUPS - 2) // 2, pair, 0)
    wy_group(1, scan_steps(N_GROUPS - 2, 0))
    for step in scan_steps(N_GROUPS - 1, 1):
        step()


def _gdn(l, proj_bf, proj_f32, proj_small, conv_w, alog_pad, dtb_pad, nw):
    slot_wide = pltpu.VMEM((2, GDN_ROWS, GDN_W), F32)
    slot_wide_bf = pltpu.VMEM((2, GDN_ROWS, GDN_W), BF16)
    slot_narrow = pltpu.VMEM((2, GDN_ROWS, LANES), F32)
    return pl.pallas_call(
        _gdn_kernel,
        grid=(BATCH,),
        in_specs=[
            pl.BlockSpec((SEQ, GDN_W), lambda b: (b, 8)),
            pl.BlockSpec((SEQ, GDN_W), lambda b: (b, 9)),
            pl.BlockSpec((SEQ, GDN_W), lambda b: (b, 10)),
            pl.BlockSpec((SEQ, GDN_W), lambda b: (b, 1)),
            pl.BlockSpec((SEQ, LANES), lambda b: (b, DAB_WINDOW)),
            pl.BlockSpec((1, CONV_K, 3 * GDN_W), lambda b: (l, 0, 0)),
            pl.BlockSpec((1, 1, LANES), lambda b: (l, 0, 0)),
            pl.BlockSpec((1, 1, LANES), lambda b: (l, 0, 0)),
            pl.BlockSpec((1, 1, GDN_D), lambda b: (l, 0, 0)),
        ],
        out_specs=pl.BlockSpec((SEQ, GDN_W), lambda b: (b, 0)),
        out_shape=jax.ShapeDtypeStruct((TOKENS, GDN_W), BF16),
        scratch_shapes=[slot_wide, slot_wide, slot_wide,
                        slot_narrow, slot_narrow,
                        slot_wide,
                        slot_wide_bf, slot_wide_bf, slot_wide_bf,
                        pltpu.VMEM((2, GDN_ROWS, GDN_HEADS * CHUNK), BF16),
                        pltpu.VMEM((2, GDN_GROUP, 8, GDN_W), F32),
                        pltpu.VMEM((GDN_HEADS, GDN_D, GDN_D), F32)],
        compiler_params=_params("arbitrary"),
        name="gdn",
    )(proj_bf, proj_bf, proj_bf, proj_f32, proj_small, conv_w, alog_pad, dtb_pad, nw)


PREP_TS = 512
ATT_TQ = 256


def _diff_prep_kernel(aq_ref, ak_ref, cos_ref, sin_ref, qw_ref, kw_ref, q_ref, k_ref):
    cos2 = cos_ref[0]
    sin2 = sin_ref[0]
    for src, w_ref, dst, scale in ((aq_ref, qw_ref, q_ref, DIFF_D ** -0.5 * LOG2E), (ak_ref, kw_ref, k_ref, 1.0)):
        w = w_ref[0]
        for g in range(2 * DIFF_HEADS):
            cols = slice(g * DIFF_D, (g + 1) * DIFF_D)
            y = _rms(src[:, cols].astype(F32), w)
            y = y * cos2 + pltpu.roll(y, DIFF_D // 2, 1) * sin2
            dst[:, cols] = (y * scale).astype(BF16)


def _diff_prep(l, proj_bf, cos2, sin2, qw, kw):
    per_batch = SEQ // PREP_TS
    width = 2 * DIFF_HEADS * DIFF_D
    tab_spec = pl.BlockSpec((1, PREP_TS, DIFF_D), lambda i: (i // per_batch, i % per_batch, 0))
    out = jax.ShapeDtypeStruct((TOKENS, width), BF16)
    return pl.pallas_call(
        _diff_prep_kernel,
        grid=(TOKENS // PREP_TS,),
        in_specs=[
            pl.BlockSpec((PREP_TS, width), lambda i: (i, 0)),
            pl.BlockSpec((PREP_TS, width), lambda i: (i, 1)),
            tab_spec, tab_spec,
            pl.BlockSpec((1, 1, DIFF_D), lambda i: (l, 0, 0)),
            pl.BlockSpec((1, 1, DIFF_D), lambda i: (l, 0, 0)),
        ],
        out_specs=[pl.BlockSpec((PREP_TS, width), lambda i: (i, 0))] * 2,
        out_shape=[out, out],
        compiler_params=_params("arbitrary"),
        name="diff_prep",
    )(proj_bf, proj_bf, cos2, sin2, qw, kw)


def _diff_attn_kernel(q1_ref, q2_ref, k1_ref, k2_ref, v_ref, az_ref, lam_ref, nw_ref, o_ref, *, lam_init):
    lv = lam_ref[0]
    lam = (jnp.exp(jnp.sum(lv[0:1] * lv[1:2], axis=-1, keepdims=True))
           - jnp.exp(jnp.sum(lv[2:3] * lv[3:4], axis=-1, keepdims=True)) + lam_init)
    nw = nw_ref[0]
    row = lax.broadcasted_iota(jnp.int32, (ATT_TQ, ATT_TQ), 0)
    col = lax.broadcasted_iota(jnp.int32, (ATT_TQ, ATT_TQ), 1)
    causal = col <= row
    q_refs = (q1_ref, q2_ref)
    k_refs = (k1_ref, k2_ref)
    n_blocks = SEQ // ATT_TQ

    def scores(i):
        start = i * ATT_TQ
        rows = slice(start, start + ATT_TQ)
        out = []
        for m in range(2):
            q = q_refs[m][rows, :]
            s_diag = jnp.where(causal, _dot_nt(q, k_refs[m][rows, :]), -jnp.inf)
            s_past = _dot_nt(q, k_refs[m][0:start, :]) if i > 0 else None
            out.append((s_diag, s_past))
        return out

    s_next = scores(0)
    for i in range(n_blocks):
        start = i * ATT_TQ
        rows = slice(start, start + ATT_TQ)
        s_cur = s_next
        if i + 1 < n_blocks:
            s_next = scores(i + 1)
        ps, inv_l = [], []
        for s_diag, s_past in s_cur:
            mx = jnp.max(s_diag, axis=-1, keepdims=True)
            if i > 0:
                mx = jnp.maximum(mx, jnp.max(s_past, axis=-1, keepdims=True))
            p_diag = jnp.exp2(s_diag - mx)
            l = jnp.sum(p_diag, axis=-1, keepdims=True)
            p_past = None
            if i > 0:
                p_past = jnp.exp2(s_past - mx)
                l = l + jnp.sum(p_past, axis=-1, keepdims=True)
                p_past = p_past.astype(BF16)
            ps.append((p_diag.astype(BF16), p_past))
            inv_l.append(1.0 / l)
        pv = []
        for p_diag, p_past in ps:
            acc = _dot(p_diag, v_ref[rows, :])
            if i > 0:
                acc = acc + _dot(p_past, v_ref[0:start, :])
            pv.append(acc)
        o = pv[0] * inv_l[0] - pv[1] * (lam * inv_l[1])
        o = _rms(o, nw) * (1.0 - lam_init)
        o_ref[rows, :] = (o * _silu(az_ref[rows, :])).astype(BF16)


def _diff_attn(l, q_d, k_d, proj_bf, proj_f32, lam, nw):
    lam_init = 0.8 - 0.6 * math.exp(-0.3 * l)
    return pl.pallas_call(
        functools.partial(_diff_attn_kernel, lam_init=lam_init),
        grid=(BATCH, DIFF_HEADS),
        in_specs=[
            pl.BlockSpec((SEQ, DIFF_D), lambda b, h: (b, 2 * h)),
            pl.BlockSpec((SEQ, DIFF_D), lambda b, h: (b, 2 * h + 1)),
            pl.BlockSpec((SEQ, DIFF_D), lambda b, h: (b, 2 * h)),
            pl.BlockSpec((SEQ, DIFF_D), lambda b, h: (b, 2 * h + 1)),
            pl.BlockSpec((SEQ, DIFF_DV), lambda b, h: (b, 8 + h)),
            pl.BlockSpec((SEQ, DIFF_DV), lambda b, h: (b, 4 + h)),
            pl.BlockSpec((1, 4, DIFF_D), lambda b, h: (l, 0, 0)),
            pl.BlockSpec((1, 1, DIFF_DV), lambda b, h: (l, 0, 0)),
        ],
        out_specs=pl.BlockSpec((SEQ, DIFF_DV), lambda b, h: (b, h)),
        out_shape=jax.ShapeDtypeStruct((TOKENS, DIFF_W), BF16),
        compiler_params=_params("arbitrary", "arbitrary"),
        name="diff_attn",
    )(q_d, q_d, k_d, k_d, proj_bf, proj_f32, lam, nw)


OUT_TM = 512


OUT_WROWS = 256


def _outproj_kernel(og_ref, od_ref, oa_ref, w_ref, x_ref, gate_ref, o_ref, wb_ref):
    @pl.when(pl.program_id(0) == 0)
    def _():
        def cast(i, carry):
            rows = pl.ds(pl.multiple_of(i * OUT_WROWS, OUT_WROWS), OUT_WROWS)
            wb_ref[rows, :] = w_ref[0, rows, :].astype(BF16)
            return carry

        lax.fori_loop(0, D_MODEL // OUT_WROWS, cast, 0)

    y = _dot(og_ref[...], wb_ref[0:GLA_W, :])
    y = y + _dot(od_ref[...], wb_ref[GLA_W:GLA_W + GDN_W, :])
    y = y + _dot(oa_ref[...], wb_ref[GLA_W + GDN_W:, :])
    o_ref[...] = x_ref[...] + gate_ref[0, 0] * y


def _out_proj(l, o_gla, o_gdn, o_diff, w_out, x2d, mods4):
    per_batch = SEQ // OUT_TM
    return pl.pallas_call(
        _outproj_kernel,
        grid=(TOKENS // OUT_TM,),
        in_specs=[
            pl.BlockSpec((OUT_TM, GLA_W), lambda m: (m, 0)),
            pl.BlockSpec((OUT_TM, GDN_W), lambda m: (m, 0)),
            pl.BlockSpec((OUT_TM, DIFF_W), lambda m: (m, 0)),
            pl.BlockSpec((1, D_MODEL, D_MODEL), lambda m: (l, 0, 0), pipeline_mode=pl.Buffered(1)),
            pl.BlockSpec((OUT_TM, D_MODEL), lambda m: (m, 0)),
            pl.BlockSpec((1, 1, 1, D_MODEL), lambda m: (l, m // per_batch, 0, 2)),
        ],
        out_specs=pl.BlockSpec((OUT_TM, D_MODEL), lambda m: (m, 0)),
        out_shape=jax.ShapeDtypeStruct((TOKENS, D_MODEL), F32),
        scratch_shapes=[pltpu.VMEM((D_MODEL, D_MODEL), BF16)],
        compiler_params=_params("arbitrary", vmem=52 * 1024 * 1024),
        name="out_proj",
    )(o_gla, o_gdn, o_diff, w_out, x2d, mods4)


def _lane_place(vecs, lane):
    n = vecs.shape[1]
    return jnp.pad(vecs.astype(F32), ((0, 0), (lane, LANES - lane - n))).reshape(DEPTH, 1, LANES)


def kernel(x, c, positions, norm_w, w_ada, b_ada, w_in, gla_w_lr, gla_b_lr, gla_norm_w, gdn_conv_w, gdn_a_log,
           gdn_dt_bias, gdn_norm_w, diff_q_norm_w, diff_k_norm_w, diff_lambda, diff_norm_w, w_out):
    c_pad = jnp.pad(c, ((0, 8 - BATCH), (0, 0)))
    mods4 = _ada_mod(c_pad, w_ada, b_ada).reshape(DEPTH, 8, 1, 3 * D_MODEL)
    cos2, sin2 = _rope_tables(positions)
    w_in_t = jnp.swapaxes(w_in, 1, 2)
    row3 = lambda p: p.reshape(DEPTH, 1, p.shape[-1])
    norm_w3 = row3(norm_w)
    wlr_pad = jnp.pad(gla_w_lr, ((0, 0), (GLR_LANE, LANES - GLR_LANE - GLA_RANK), (0, 0)))
    alog_pad = _lane_place(gdn_a_log, DA_LANE)
    dtb_pad = _lane_place(gdn_dt_bias, DA_LANE)
    x2d = x.reshape(TOKENS, D_MODEL)
    for l in range(DEPTH):
        proj_bf, proj_f32, proj_small = _in_proj(l, x2d, norm_w3, mods4, w_in_t)
        o_gla = _gla(l, proj_bf, proj_f32, proj_small, wlr_pad, row3(gla_b_lr), row3(gla_norm_w))
        o_gdn = _gdn(l, proj_bf, proj_f32, proj_small, gdn_conv_w, alog_pad, dtb_pad, row3(gdn_norm_w))
        q_d, k_d = _diff_prep(l, proj_bf, cos2, sin2, row3(diff_q_norm_w), row3(diff_k_norm_w))
        o_diff = _diff_attn(l, q_d, k_d, proj_bf, proj_f32, diff_lambda, row3(diff_norm_w))
        x2d = _out_proj(l, o_gla, o_gdn, o_diff, w_out, x2d, mods4)
    return x2d.reshape(BATCH, SEQ, D_MODEL)
```

```python
import functools
import math

import jax
import jax.numpy as jnp
from jax import lax
from jax.experimental import pallas as pl
from jax.experimental.pallas import tpu as pltpu

F32 = jnp.float32
BF16 = jnp.bfloat16

D_MODEL = 2048
BATCH = 4
SEQ = 2048
DEPTH = 2
TOKENS = BATCH * SEQ

GLA_HEADS = 4
GLA_DK = 64
GLA_DV = 128
GLA_W = GLA_HEADS * GLA_DV
GLA_RANK = 16
GLA_TAU = 16.0
GDN_HEADS = 4
GDN_D = 128
GDN_W = GDN_HEADS * GDN_D
CONV_K = 4
DIFF_HEADS = 4
DIFF_D = 128
DIFF_DV = 256
DIFF_W = DIFF_HEADS * DIFF_DV
CHUNK = 64
N_CHUNKS = SEQ // CHUNK
ROPE_THETA = 10000.0
EPS = 1e-6
LOG2E = math.log2(math.e)

LANES = 128

_IN_SPLITS = (
    ("gq", GLA_HEADS * GLA_DK), ("gk", GLA_HEADS * GLA_DK), ("gv", GLA_W), ("glr", GLA_RANK), ("gz", GLA_W),
    ("dq", GDN_W), ("dk", GDN_W), ("dv", GDN_W), ("da", GDN_HEADS), ("db", GDN_HEADS), ("dz", GDN_W),
    ("aq", DIFF_HEADS * 2 * DIFF_D), ("ak", DIFF_HEADS * 2 * DIFF_D), ("av", DIFF_W), ("az", DIFF_W),
)
_IN_OFFSETS = {}
_off = 0
for _name, _width in _IN_SPLITS:
    _IN_OFFSETS[_name] = (_off, _off + _width)
    _off += _width
D_IN = _off

_BF_ORDER = ("aq", "ak", "av", "gq", "gk", "gv", "dq", "dk", "dv")
_F32_ORDER = ("gz", "dz", "az")
N_BF = 5632
N_F32 = 2048
PROJ_TN = 512
NB_BF = N_BF // PROJ_TN
NB_F32 = N_F32 // PROJ_TN
GLR_SRC = (_IN_OFFSETS["glr"][0] // LANES) * LANES
DAB_SRC = (_IN_OFFSETS["da"][0] // LANES) * LANES
GLR_WINDOW = 0
DAB_WINDOW = 1
GLR_LANE = _IN_OFFSETS["glr"][0] - GLR_SRC
DA_LANE = _IN_OFFSETS["da"][0] - DAB_SRC
DB_LANE = _IN_OFFSETS["db"][0] - DAB_SRC
assert GLR_LANE + GLA_RANK <= LANES and DB_LANE + GDN_HEADS <= LANES

VMEM_LIMIT = 48 * 1024 * 1024


def _sigmoid(x):
    return 1.0 / (1.0 + jnp.exp(-x))


def _silu(x):
    return x * _sigmoid(x)


def _softplus(x):
    return jnp.maximum(x, 0.0) + jnp.log1p(jnp.exp(-jnp.abs(x)))


def _dot(a, b):
    return jnp.dot(a, b, preferred_element_type=F32)


def _dot_nt(a, b):
    return lax.dot_general(a, b, (((1,), (1,)), ((), ())), preferred_element_type=F32)


def _dot_tn(a, b):
    return lax.dot_general(a, b, (((0,), (0,)), ((), ())), preferred_element_type=F32)


def _dot_f32(a, b):
    return jnp.dot(a, b, precision=lax.Precision.HIGHEST, preferred_element_type=F32)


def _dot_bf(a, b):
    return _dot(a.astype(BF16), b.astype(BF16))


def _rms(x, w):
    return x * lax.rsqrt(jnp.mean(x * x, axis=-1, keepdims=True) + EPS) * w


def _params(*semantics, vmem=VMEM_LIMIT):
    return pltpu.CompilerParams(dimension_semantics=semantics, vmem_limit_bytes=vmem)


ADA_TN = 1024


def _ada_kernel(c_ref, w_ref, b_ref, o_ref):
    c_act = _silu(c_ref[...])
    o_ref[0] = _dot(c_act.astype(BF16), w_ref[0].astype(BF16)) + b_ref[0]


def _ada_mod(c_pad, w_ada, b_ada):
    n3 = 3 * D_MODEL
    return pl.pallas_call(
        _ada_kernel,
        grid=(DEPTH, n3 // ADA_TN),
        in_specs=[
            pl.BlockSpec((8, D_MODEL), lambda l, n: (0, 0)),
            pl.BlockSpec((1, D_MODEL, ADA_TN), lambda l, n: (l, 0, n)),
            pl.BlockSpec((1, 1, ADA_TN), lambda l, n: (l, 0, n)),
        ],
        out_specs=pl.BlockSpec((1, 8, ADA_TN), lambda l, n: (l, 0, n)),
        out_shape=jax.ShapeDtypeStruct((DEPTH, 8, n3), F32),
        compiler_params=_params("arbitrary", "arbitrary"),
        name="ada_mod",
    )(c_pad, w_ada, b_ada.reshape(DEPTH, 1, n3))


def _rope_kernel(pos_ref, freq_ref, sign_ref, cos_ref, sin_ref):
    ang = pos_ref[0].astype(F32) * freq_ref[...]
    cos_ref[0] = jnp.cos(ang)
    sin_ref[0] = jnp.sin(ang) * sign_ref[...]


def _rope_tables(positions):
    half = DIFF_D // 2
    inv_freq = ROPE_THETA ** (-jnp.arange(half, dtype=F32) / half)
    freq2 = jnp.concatenate([inv_freq, inv_freq]).reshape(1, DIFF_D)
    sign = jnp.concatenate([-jnp.ones((half,), F32), jnp.ones((half,), F32)]).reshape(1, DIFF_D)
    tab = jax.ShapeDtypeStruct((BATCH, SEQ, DIFF_D), F32)
    return pl.pallas_call(
        _rope_kernel,
        grid=(BATCH,),
        in_specs=[
            pl.BlockSpec((1, SEQ, 1), lambda b: (b, 0, 0)),
            pl.BlockSpec((1, DIFF_D), lambda b: (0, 0)),
            pl.BlockSpec((1, DIFF_D), lambda b: (0, 0)),
        ],
        out_specs=[pl.BlockSpec((1, SEQ, DIFF_D), lambda b: (b, 0, 0))] * 2,
        out_shape=[tab, tab],
        compiler_params=_params("arbitrary"),
        name="rope_tables",
    )(positions.reshape(BATCH, SEQ, 1), freq2, sign)


PROJ_TM = 1024
PROJ_ROWS = 256


def _inproj_kernel(src_ref, x_ref, nw_ref, shift_ref, scale_ref, w_ref, wg_ref, wd_ref,
                   ob_ref, of_ref, os_ref, h_ref):
    del src_ref
    n = pl.program_id(1)

    @pl.when(n == 0)
    def _():
        gain = nw_ref[0] * (1.0 + scale_ref[0, 0])
        shift = shift_ref[0, 0]
        w = w_ref[...].astype(BF16)
        w_small = jnp.concatenate([wg_ref[...], wd_ref[...]], axis=0).astype(BF16)
        for i in range(PROJ_TM // PROJ_ROWS):
            rows = slice(i * PROJ_ROWS, (i + 1) * PROJ_ROWS)
            x = x_ref[rows, :]
            h = (x * lax.rsqrt(jnp.mean(x * x, axis=-1, keepdims=True) + EPS) * gain + shift).astype(BF16)
            h_ref[rows, :] = h
            ob_ref[rows, :] = _dot_nt(h, w).astype(BF16)
            os_ref[rows, :] = _dot_nt(h, w_small)

    @pl.when(jnp.logical_and(n > 0, n < NB_BF))
    def _():
        ob_ref[...] = _dot_nt(h_ref[...], w_ref[...].astype(BF16)).astype(BF16)

    @pl.when(n >= NB_BF)
    def _():
        of_ref[...] = _dot_nt(h_ref[...], w_ref[...].astype(BF16))


def _w_in_sources():
    starts = []
    for name in _BF_ORDER + _F32_ORDER:
        lo, hi = _IN_OFFSETS[name]
        starts += list(range(lo, hi, 256))
    firsts = starts[0::2]
    assert all(b == a + 256 for a, b in zip(firsts, starts[1::2])), "each 512 block must be contiguous"
    assert len(firsts) * PROJ_TN == N_BF + N_F32
    return firsts


def _in_proj(l, x2d, norm_w3, mods4, w_in_t):
    per_batch = SEQ // PROJ_TM
    sublanes = 8
    assert all(s % sublanes == 0 for s in _w_in_sources())
    starts = jnp.asarray([s // sublanes for s in _w_in_sources()], jnp.int32)

    def window(first_row):
        return pl.BlockSpec((pl.Squeezed(), pl.Element(LANES), pl.Element(D_MODEL)),
                            lambda m, n, src: (l, first_row, 0))

    grid_spec = pltpu.PrefetchScalarGridSpec(
        num_scalar_prefetch=1,
        grid=(TOKENS // PROJ_TM, NB_BF + NB_F32),
        in_specs=[
            pl.BlockSpec((PROJ_TM, D_MODEL), lambda m, n, src: (m, 0)),
            pl.BlockSpec((1, 1, D_MODEL), lambda m, n, src: (l, 0, 0)),
            pl.BlockSpec((1, 1, 1, D_MODEL), lambda m, n, src: (l, m // per_batch, 0, 0)),
            pl.BlockSpec((1, 1, 1, D_MODEL), lambda m, n, src: (l, m // per_batch, 0, 1)),
            pl.BlockSpec((pl.Squeezed(), pl.Element(PROJ_TN), pl.Element(D_MODEL)),
                         lambda m, n, src: (l, src[n] * sublanes, 0)),
            window(GLR_SRC),
            window(DAB_SRC),
        ],
        out_specs=[
            pl.BlockSpec((PROJ_TM, PROJ_TN), lambda m, n, src: (m, jnp.minimum(n, NB_BF - 1))),
            pl.BlockSpec((PROJ_TM, PROJ_TN), lambda m, n, src: (m, jnp.maximum(n - NB_BF, 0))),
            pl.BlockSpec((PROJ_TM, 2 * LANES), lambda m, n, src: (m, 0)),
        ],
        scratch_shapes=[pltpu.VMEM((PROJ_TM, D_MODEL), BF16)],
    )
    return pl.pallas_call(
        _inproj_kernel,
        grid_spec=grid_spec,
        out_shape=[
            jax.ShapeDtypeStruct((TOKENS, N_BF), BF16),
            jax.ShapeDtypeStruct((TOKENS, N_F32), F32),
            jax.ShapeDtypeStruct((TOKENS, 2 * LANES), F32),
        ],
        compiler_params=_params("arbitrary", "arbitrary"),
        name="in_proj",
    )(starts, x2d, norm_w3, mods4, mods4, w_in_t, w_in_t, w_in_t)


GLA_GROUP = 4


def _gla_kernel(gq_ref, gk_ref, gv_ref, gz_ref, sm_ref, wlr_ref, blr_ref, nw_ref, o_ref, la_ref):
    z = _dot_f32(sm_ref[...], wlr_ref[0]) + blr_ref[0]
    la_ref[...] = -_softplus(-z) * (1.0 / GLA_TAU)

    row = lax.broadcasted_iota(jnp.int32, (CHUNK, CHUNK), 0)
    col = lax.broadcasted_iota(jnp.int32, (CHUNK, CHUNK), 1)
    incl = col <= row
    tril = jnp.where(incl, 1.0, 0.0)
    nw = nw_ref[0]
    heads = range(GLA_HEADS)
    group = range(GLA_GROUP)

    def kcols(x, h):
        return x[:, h * GLA_DK:(h + 1) * GLA_DK]

    def vcols(x, h):
        return x[:, h * GLA_DV:(h + 1) * GLA_DV]

    def body(gi, states):
        rs = [pl.ds(pl.multiple_of((gi * GLA_GROUP + c) * CHUNK, CHUNK), CHUNK) for c in group]
        bc = [_dot_f32(tril, la_ref[r, :]) for r in rs]
        qe, ke, kt, dec, v = [], [], [], [], []
        for c in group:
            b_last = bc[c][CHUNK - 1:CHUNK, :]
            q = gq_ref[rs[c], :].astype(F32) * GLA_DK ** -0.5
            k = gk_ref[rs[c], :].astype(F32)
            qe.append((q * jnp.exp(bc[c])).astype(BF16))
            ke.append((k * jnp.exp(-bc[c])).astype(BF16))
            kt.append((k * jnp.exp(b_last - bc[c])).astype(BF16))
            dec.append(jnp.exp(b_last))
            v.append(gv_ref[rs[c], :])
        att = [[jnp.where(incl, _dot_nt(kcols(qe[c], h), kcols(ke[c], h)), 0.0).astype(BF16) for h in heads]
               for c in group]
        o_intra = [[_dot(att[c][h], vcols(v[c], h)) for h in heads] for c in group]
        d_state = [[_dot_tn(vcols(v[c], h), kcols(kt[c], h)) for h in heads] for c in group]
        st = list(states)
        for c in group:
            gz = gz_ref[rs[c], :]
            outs = []
            for h in heads:
                o = o_intra[c][h] + _dot_nt(kcols(qe[c], h), st[h].astype(BF16))
                st[h] = st[h] * kcols(dec[c], h) + d_state[c][h]
                outs.append(_rms(o, nw) * _silu(vcols(gz, h)))
            o_ref[rs[c], :] = jnp.concatenate(outs, axis=-1).astype(BF16)
        return tuple(st)

    lax.fori_loop(0, N_CHUNKS // GLA_GROUP, body,
                  tuple(jnp.zeros((GLA_DV, GLA_DK), F32) for _ in heads))


def _gla(l, proj_bf, proj_f32, proj_small, wlr_pad, blr, nw):
    return pl.pallas_call(
        _gla_kernel,
        grid=(BATCH,),
        in_specs=[
            pl.BlockSpec((SEQ, 256), lambda b: (b, 12)),
            pl.BlockSpec((SEQ, 256), lambda b: (b, 13)),
            pl.BlockSpec((SEQ, GLA_W), lambda b: (b, 7)),
            pl.BlockSpec((SEQ, GLA_W), lambda b: (b, 0)),
            pl.BlockSpec((SEQ, LANES), lambda b: (b, GLR_WINDOW)),
            pl.BlockSpec((1, LANES, GLA_HEADS * GLA_DK), lambda b: (l, 0, 0)),
            pl.BlockSpec((1, 1, GLA_HEADS * GLA_DK), lambda b: (l, 0, 0)),
            pl.BlockSpec((1, 1, GLA_DV), lambda b: (l, 0, 0)),
        ],
        out_specs=pl.BlockSpec((SEQ, GLA_W), lambda b: (b, 0)),
        out_shape=jax.ShapeDtypeStruct((TOKENS, GLA_W), BF16),
        scratch_shapes=[pltpu.VMEM((SEQ, GLA_HEADS * GLA_DK), F32)],
        compiler_params=_params("arbitrary"),
        name="gla",
    )(proj_bf, proj_bf, proj_bf, proj_f32, proj_small, wlr_pad, blr, nw)


GDN_HALO = 16
GDN_GROUP = 2
GDN_ROWS = GDN_GROUP * CHUNK
N_GROUPS = N_CHUNKS // GDN_GROUP


def _heads(x, h):
    return x[:, h * GDN_D:(h + 1) * GDN_D]


def _gdn_kernel(dq_ref, dk_ref, dv_ref, dz_ref, sm_ref, cw_ref, alog_ref, dtb_ref, nw_ref, o_ref,
                pq_s, pk_s, pv_s, pg_s, pb_s, u_s, w_s, qe_s, kt_s, qk_s, cd_s, st_s):
    heads = range(GDN_HEADS)
    row = lax.broadcasted_iota(jnp.int32, (CHUNK, CHUNK), 0)
    col = lax.broadcasted_iota(jnp.int32, (CHUNK, CHUNK), 1)
    incl = col <= row
    strict = col < row
    tril = jnp.where(incl, 1.0, 0.0)
    eye = jnp.where(col == row, 1.0, 0.0)
    block_bits = jnp.bitwise_xor(row, col)
    nw = nw_ref[0]
    cw = cw_ref[0]
    neg_a = -jnp.exp(alog_ref[0])
    dtb = dtb_ref[0]

    def conv(x_ref, src, w, first_block):
        cur = x_ref[pl.ds(src, GDN_ROWS), :].astype(F32)
        if first_block:
            prev = jnp.zeros((GDN_HALO, GDN_W), F32)
        else:
            prev = x_ref[pl.ds(pl.multiple_of(src - GDN_HALO, GDN_HALO), GDN_HALO), :].astype(F32)
        win = jnp.concatenate([prev, cur], axis=0)
        first = GDN_HALO - (CONV_K - 1)
        acc = win[first:first + GDN_ROWS] * w[0:1]
        for j in range(1, CONV_K):
            acc = acc + win[first + j:first + j + GDN_ROWS] * w[j:j + 1]
        return _silu(acc)

    def l2n(x, scale):
        parts = []
        for h in heads:
            xh = _heads(x, h)
            parts.append(xh * (lax.rsqrt(jnp.sum(xh * xh, axis=-1, keepdims=True) + EPS) * scale))
        return jnp.concatenate(parts, axis=-1)

    def group_rows(g):
        return pl.multiple_of(g * GDN_ROWS, GDN_ROWS)

    def phase0_steps(g, slot, first_block=False):
        src = 0 if first_block else group_rows(g)

        def q_step():
            pq_s[slot] = l2n(conv(dq_ref, src, cw[:, 0:GDN_W], first_block), GDN_D ** -0.5)

        def k_step():
            pk_s[slot] = l2n(conv(dk_ref, src, cw[:, GDN_W:2 * GDN_W], first_block), 1.0)

        def v_step():
            pv_s[slot] = conv(dv_ref, src, cw[:, 2 * GDN_W:3 * GDN_W], first_block)

        def gate_step():
            sm = sm_ref[pl.ds(src, GDN_ROWS), :]
            pg_s[slot] = neg_a * _softplus(sm + dtb)
            pb_s[slot] = _sigmoid(sm)

        return [q_step, k_step, v_step, gate_step]

    for step in phase0_steps(0, 0, first_block=True):
        step()

    def wy_group(slot, fillers):
        fillers = list(fillers)

        def emit():
            if fillers:
                fillers.pop(0)()

        group = range(GDN_GROUP)
        units = [(c, h) for c in group for h in heads]
        crows = [slice(c * CHUNK, (c + 1) * CHUNK) for c in group]
        gcs = [_dot_f32(tril, pg_s[slot, cr, :]) for cr in crows]
        gcs_t = [x.T for x in gcs]
        beta_all = [pb_s[slot, cr, :] for cr in crows]
        qn = [pq_s[slot, cr, :] for cr in crows]
        kn = [pk_s[slot, cr, :] for cr in crows]
        v = [pv_s[slot, cr, :] for cr in crows]
        decay, eg, ekt, cd, k_beta, v_beta, kb = {}, {}, {}, {}, {}, {}, {}
        for c, h in units:
            gc_col = gcs[c][:, DA_LANE + h:DA_LANE + h + 1]
            gc_row = gcs_t[c][DA_LANE + h:DA_LANE + h + 1, :]
            decay[c, h] = jnp.exp(jnp.where(incl, gc_col - gc_row, -jnp.inf))
            gcb = jnp.broadcast_to(gc_col, (CHUNK, GDN_D))
            g_last = gcb[CHUNK - 1:CHUNK, :]
            eg[c, h] = jnp.exp(gcb)
            ekt[c, h] = jnp.exp(g_last - gcb)
            cd[c, h] = jnp.broadcast_to(jnp.exp(g_last), (8, GDN_D))
            b_col = beta_all[c][:, DB_LANE + h:DB_LANE + h + 1]
            k_beta[c, h] = _heads(kn[c], h) * b_col
            v_beta[c, h] = _heads(v[c], h) * b_col
            kb[c, h] = _heads(kn[c], h).astype(BF16)
        lower = {u: jnp.where(strict, _dot_nt(k_beta[u].astype(BF16), kb[u]) * decay[u], 0.0) for u in units}
        emit()
        base = 8
        in_base = lax.shift_right_logical(block_bits, int(math.log2(base))) == 0
        pw = {u: jnp.where(in_base, -lower[u], 0.0) for u in units}
        tq = dict(pw)
        pw = {u: _dot_bf(pw[u], pw[u]) for u in units}
        emit()
        both = {u: _dot_bf(jnp.concatenate([tq[u], pw[u]], axis=0), pw[u]) for u in units}
        emit()
        tq = {u: tq[u] + pw[u] + both[u][0:CHUNK] for u in units}
        pw = {u: both[u][CHUNK:2 * CHUNK] for u in units}
        inv = {u: eye + (tq[u] + pw[u] + _dot_bf(tq[u], pw[u])) for u in units}
        emit()
        for size in (base, 2 * base, 4 * base):
            inside = lax.shift_right_logical(block_bits, int(math.log2(size))) == 1
            nt = {u: _dot_bf(jnp.where(inside, lower[u], 0.0), inv[u]) for u in units}
            emit()
            inv = {u: inv[u] - _dot_bf(inv[u], nt[u]) for u in units}
            emit()
        rhs = {u: jnp.concatenate([v_beta[u], k_beta[u] * eg[u]], axis=-1) for u in units}
        sol = {u: rhs[u] + _dot((inv[u] - eye).astype(BF16), rhs[u].astype(BF16)) for u in units}
        emit()
        qk = {(c, h): _dot_nt(_heads(qn[c], h).astype(BF16), kb[c, h]) * decay[c, h] for c, h in units}
        while fillers:
            emit()
        for c in group:
            cr = crows[c]
            u_s[slot, cr, :] = jnp.concatenate([sol[c, h][:, 0:GDN_D] for h in heads], axis=-1)
            w_s[slot, cr, :] = jnp.concatenate([sol[c, h][:, GDN_D:2 * GDN_D] for h in heads],
                                               axis=-1).astype(BF16)
            qe_s[slot, cr, :] = jnp.concatenate([_heads(qn[c], h) * eg[c, h] for h in heads],
                                                axis=-1).astype(BF16)
            kt_s[slot, cr, :] = jnp.concatenate([_heads(kn[c], h) * ekt[c, h] for h in heads],
                                                axis=-1).astype(BF16)
            qk_s[slot, cr, :] = jnp.concatenate([qk[c, h] for h in heads], axis=-1).astype(BF16)
            cd_s[slot, c] = jnp.concatenate([cd[c, h] for h in heads], axis=-1)

    def scan_steps(g, slot):
        box = {}
        steps = []
        for c in range(GDN_GROUP):
            cr = slice(c * CHUNK, (c + 1) * CHUNK)

            def first(c=c, cr=cr):
                state = [st_s[h] for h in heads] if c == 0 else box["state"]
                sb = [s.astype(BF16) for s in state]
                w = w_s[slot, cr, :]
                qe = qe_s[slot, cr, :]
                box["state"] = state
                box["ws"] = [_dot(_heads(w, h), sb[h]) for h in heads]
                box["qs"] = [_dot(_heads(qe, h), sb[h]) for h in heads]

            def second(c=c, cr=cr):
                u = u_s[slot, cr, :]
                kt = kt_s[slot, cr, :]
                qk = qk_s[slot, cr, :]
                cd = cd_s[slot, c]
                v_new = [(_heads(u, h) - box["ws"][h]).astype(BF16) for h in heads]
                o = [box["qs"][h] + _dot(qk[:, h * CHUNK:(h + 1) * CHUNK], v_new[h]) for h in heads]
                state = [box["state"][h] * _heads(cd, h)[0:1] + _dot_tn(_heads(kt, h), v_new[h]) for h in heads]
                if c == GDN_GROUP - 1:
                    for h in heads:
                        st_s[h] = state[h]
                box["state"] = state
                r = pl.ds(pl.multiple_of((g * GDN_GROUP + c) * CHUNK, CHUNK), CHUNK)
                o = jnp.concatenate([_rms(o[h], nw) for h in heads], axis=-1)
                o_ref[r, :] = (o * _silu(dz_ref[r, :])).astype(BF16)

            steps += [first, second]
        return steps

    def interleave(a, b):
        out = []
        for i in range(max(len(a), len(b))):
            out += a[i:i + 1] + b[i:i + 1]
        return out

    st_s[...] = jnp.zeros((GDN_HEADS, GDN_D, GDN_D), F32)
    wy_group(0, phase0_steps(1, 1))

    def pair(i, carry):
        g = 2 * i
        wy_group(1, interleave(scan_steps(g, 0), phase0_steps(g + 2, 0)))
        wy_group(0, interleave(scan_steps(g + 1, 1), phase0_steps(g + 3, 1)))
        return carry

    lax.fori_loop(0, (N_GROUPS - 2) // 2, pair, 0)
    wy_group(1, scan_steps(N_GROUPS - 2, 0))
    for step in scan_steps(N_GROUPS - 1, 1):
        step()


def _gdn(l, proj_bf, proj_f32, proj_small, conv_w, alog_pad, dtb_pad, nw):
    slot_wide = pltpu.VMEM((2, GDN_ROWS, GDN_W), F32)
    slot_wide_bf = pltpu.VMEM((2, GDN_ROWS, GDN_W), BF16)
    slot_narrow = pltpu.VMEM((2, GDN_ROWS, LANES), F32)
    return pl.pallas_call(
        _gdn_kernel,
        grid=(BATCH,),
        in_specs=[
            pl.BlockSpec((SEQ, GDN_W), lambda b: (b, 8)),
            pl.BlockSpec((SEQ, GDN_W), lambda b: (b, 9)),
            pl.BlockSpec((SEQ, GDN_W), lambda b: (b, 10)),
            pl.BlockSpec((SEQ, GDN_W), lambda b: (b, 1)),
            pl.BlockSpec((SEQ, LANES), lambda b: (b, DAB_WINDOW)),
            pl.BlockSpec((1, CONV_K, 3 * GDN_W), lambda b: (l, 0, 0)),
            pl.BlockSpec((1, 1, LANES), lambda b: (l, 0, 0)),
            pl.BlockSpec((1, 1, LANES), lambda b: (l, 0, 0)),
            pl.BlockSpec((1, 1, GDN_D), lambda b: (l, 0, 0)),
        ],
        out_specs=pl.BlockSpec((SEQ, GDN_W), lambda b: (b, 0)),
        out_shape=jax.ShapeDtypeStruct((TOKENS, GDN_W), BF16),
        scratch_shapes=[slot_wide, slot_wide, slot_wide,
                        slot_narrow, slot_narrow,
                        slot_wide,
                        slot_wide_bf, slot_wide_bf, slot_wide_bf,
                        pltpu.VMEM((2, GDN_ROWS, GDN_HEADS * CHUNK), BF16),
                        pltpu.VMEM((2, GDN_GROUP, 8, GDN_W), F32),
                        pltpu.VMEM((GDN_HEADS, GDN_D, GDN_D), F32)],
        compiler_params=_params("arbitrary"),
        name="gdn",
    )(proj_bf, proj_bf, proj_bf, proj_f32, proj_small, conv_w, alog_pad, dtb_pad, nw)


PREP_TS = 512
ATT_TQ = 256


def _diff_prep_kernel(aq_ref, ak_ref, cos_ref, sin_ref, qw_ref, kw_ref, q_ref, k_ref):
    cos2 = cos_ref[0]
    sin2 = sin_ref[0]
    for src, w_ref, dst, scale in ((aq_ref, qw_ref, q_ref, DIFF_D ** -0.5 * LOG2E), (ak_ref, kw_ref, k_ref, 1.0)):
        w = w_ref[0]
        for g in range(2 * DIFF_HEADS):
            cols = slice(g * DIFF_D, (g + 1) * DIFF_D)
            y = _rms(src[:, cols].astype(F32), w)
            y = y * cos2 + pltpu.roll(y, DIFF_D // 2, 1) * sin2
            dst[:, cols] = (y * scale).astype(BF16)


def _diff_prep(l, proj_bf, cos2, sin2, qw, kw):
    per_batch = SEQ // PREP_TS
    width = 2 * DIFF_HEADS * DIFF_D
    tab_spec = pl.BlockSpec((1, PREP_TS, DIFF_D), lambda i: (i // per_batch, i % per_batch, 0))
    out = jax.ShapeDtypeStruct((TOKENS, width), BF16)
    return pl.pallas_call(
        _diff_prep_kernel,
        grid=(TOKENS // PREP_TS,),
        in_specs=[
            pl.BlockSpec((PREP_TS, width), lambda i: (i, 0)),
            pl.BlockSpec((PREP_TS, width), lambda i: (i, 1)),
            tab_spec, tab_spec,
            pl.BlockSpec((1, 1, DIFF_D), lambda i: (l, 0, 0)),
            pl.BlockSpec((1, 1, DIFF_D), lambda i: (l, 0, 0)),
        ],
        out_specs=[pl.BlockSpec((PREP_TS, width), lambda i: (i, 0))] * 2,
        out_shape=[out, out],
        compiler_params=_params("arbitrary"),
        name="diff_prep",
    )(proj_bf, proj_bf, cos2, sin2, qw, kw)


def _diff_attn_kernel(q1_ref, q2_ref, k1_ref, k2_ref, v_ref, az_ref, lam_ref, nw_ref, o_ref, *, lam_init):
    lv = lam_ref[0]
    lam = (jnp.exp(jnp.sum(lv[0:1] * lv[1:2], axis=-1, keepdims=True))
           - jnp.exp(jnp.sum(lv[2:3] * lv[3:4], axis=-1, keepdims=True)) + lam_init)
    nw = nw_ref[0]
    row = lax.broadcasted_iota(jnp.int32, (ATT_TQ, ATT_TQ), 0)
    col = lax.broadcasted_iota(jnp.int32, (ATT_TQ, ATT_TQ), 1)
    causal = col <= row
    q_refs = (q1_ref, q2_ref)
    k_refs = (k1_ref, k2_ref)
    n_blocks = SEQ // ATT_TQ

    def scores(i):
        start = i * ATT_TQ
        rows = slice(start, start + ATT_TQ)
        out = []
        for m in range(2):
            q = q_refs[m][rows, :]
            s_diag = jnp.where(causal, _dot_nt(q, k_refs[m][rows, :]), -jnp.inf)
            s_past = _dot_nt(q, k_refs[m][0:start, :]) if i > 0 else None
            out.append((s_diag, s_past))
        return out

    s_next = scores(0)
    for i in range(n_blocks):
        start = i * ATT_TQ
        rows = slice(start, start + ATT_TQ)
        s_cur = s_next
        if i + 1 < n_blocks:
            s_next = scores(i + 1)
        ps, inv_l = [], []
        for s_diag, s_past in s_cur:
            mx = jnp.max(s_diag, axis=-1, keepdims=True)
            if i > 0:
                mx = jnp.maximum(mx, jnp.max(s_past, axis=-1, keepdims=True))
            p_diag = jnp.exp2(s_diag - mx)
            l = jnp.sum(p_diag, axis=-1, keepdims=True)
            p_past = None
            if i > 0:
                p_past = jnp.exp2(s_past - mx)
                l = l + jnp.sum(p_past, axis=-1, keepdims=True)
                p_past = p_past.astype(BF16)
            ps.append((p_diag.astype(BF16), p_past))
            inv_l.append(1.0 / l)
        pv = []
        for p_diag, p_past in ps:
            acc = _dot(p_diag, v_ref[rows, :])
            if i > 0:
                acc = acc + _dot(p_past, v_ref[0:start, :])
            pv.append(acc)
        o = pv[0] * inv_l[0] - pv[1] * (lam * inv_l[1])
        o = _rms(o, nw) * (1.0 - lam_init)
        o_ref[rows, :] = (o * _silu(az_ref[rows, :])).astype(BF16)


def _diff_attn(l, q_d, k_d, proj_bf, proj_f32, lam, nw):
    lam_init = 0.8 - 0.6 * math.exp(-0.3 * l)
    return pl.pallas_call(
        functools.partial(_diff_attn_kernel, lam_init=lam_init),
        grid=(BATCH, DIFF_HEADS),
        in_specs=[
            pl.BlockSpec((SEQ, DIFF_D), lambda b, h: (b, 2 * h)),
            pl.BlockSpec((SEQ, DIFF_D), lambda b, h: (b, 2 * h + 1)),
            pl.BlockSpec((SEQ, DIFF_D), lambda b, h: (b, 2 * h)),
            pl.BlockSpec((SEQ, DIFF_D), lambda b, h: (b, 2 * h + 1)),
            pl.BlockSpec((SEQ, DIFF_DV), lambda b, h: (b, 8 + h)),
            pl.BlockSpec((SEQ, DIFF_DV), lambda b, h: (b, 4 + h)),
            pl.BlockSpec((1, 4, DIFF_D), lambda b, h: (l, 0, 0)),
            pl.BlockSpec((1, 1, DIFF_DV), lambda b, h: (l, 0, 0)),
        ],
        out_specs=pl.BlockSpec((SEQ, DIFF_DV), lambda b, h: (b, h)),
        out_shape=jax.ShapeDtypeStruct((TOKENS, DIFF_W), BF16),
        compiler_params=_params("arbitrary", "arbitrary"),
        name="diff_attn",
    )(q_d, q_d, k_d, k_d, proj_bf, proj_f32, lam, nw)


OUT_TM = 512


OUT_WROWS = 256


def _outproj_kernel(og_ref, od_ref, oa_ref, w_ref, x_ref, gate_ref, o_ref, wb_ref):
    @pl.when(pl.program_id(0) == 0)
    def _():
        def cast(i, carry):
            rows = pl.ds(pl.multiple_of(i * OUT_WROWS, OUT_WROWS), OUT_WROWS)
            wb_ref[rows, :] = w_ref[0, rows, :].astype(BF16)
            return carry

        lax.fori_loop(0, D_MODEL // OUT_WROWS, cast, 0)

    y = _dot(og_ref[...], wb_ref[0:GLA_W, :])
    y = y + _dot(od_ref[...], wb_ref[GLA_W:GLA_W + GDN_W, :])
    y = y + _dot(oa_ref[...], wb_ref[GLA_W + GDN_W:, :])
    o_ref[...] = x_ref[...] + gate_ref[0, 0] * y


def _out_proj(l, o_gla, o_gdn, o_diff, w_out, x2d, mods4):
    per_batch = SEQ // OUT_TM
    return pl.pallas_call(
        _outproj_kernel,
        grid=(TOKENS // OUT_TM,),
        in_specs=[
            pl.BlockSpec((OUT_TM, GLA_W), lambda m: (m, 0)),
            pl.BlockSpec((OUT_TM, GDN_W), lambda m: (m, 0)),
            pl.BlockSpec((OUT_TM, DIFF_W), lambda m: (m, 0)),
            pl.BlockSpec((1, D_MODEL, D_MODEL), lambda m: (l, 0, 0), pipeline_mode=pl.Buffered(1)),
            pl.BlockSpec((OUT_TM, D_MODEL), lambda m: (m, 0)),
            pl.BlockSpec((1, 1, 1, D_MODEL), lambda m: (l, m // per_batch, 0, 2)),
        ],
        out_specs=pl.BlockSpec((OUT_TM, D_MODEL), lambda m: (m, 0)),
        out_shape=jax.ShapeDtypeStruct((TOKENS, D_MODEL), F32),
        scratch_shapes=[pltpu.VMEM((D_MODEL, D_MODEL), BF16)],
        compiler_params=_params("arbitrary", vmem=52 * 1024 * 1024),
        name="out_proj",
    )(o_gla, o_gdn, o_diff, w_out, x2d, mods4)


def _lane_place(vecs, lane):
    n = vecs.shape[1]
    return jnp.pad(vecs.astype(F32), ((0, 0), (lane, LANES - lane - n))).reshape(DEPTH, 1, LANES)


def kernel(x, c, positions, norm_w, w_ada, b_ada, w_in, gla_w_lr, gla_b_lr, gla_norm_w, gdn_conv_w, gdn_a_log,
           gdn_dt_bias, gdn_norm_w, diff_q_norm_w, diff_k_norm_w, diff_lambda, diff_norm_w, w_out):
    c_pad = jnp.pad(c, ((0, 8 - BATCH), (0, 0)))
    mods4 = _ada_mod(c_pad, w_ada, b_ada).reshape(DEPTH, 8, 1, 3 * D_MODEL)
    cos2, sin2 = _rope_tables(positions)
    w_in_t = jnp.swapaxes(w_in, 1, 2)
    row3 = lambda p: p.reshape(DEPTH, 1, p.shape[-1])
    norm_w3 = row3(norm_w)
    wlr_pad = jnp.pad(gla_w_lr, ((0, 0), (GLR_LANE, LANES - GLR_LANE - GLA_RANK), (0, 0)))
    alog_pad = _lane_place(gdn_a_log, DA_LANE)
    dtb_pad = _lane_place(gdn_dt_bias, DA_LANE)
    x2d = x.reshape(TOKENS, D_MODEL)
    for l in range(DEPTH):
        proj_bf, proj_f32, proj_small = _in_proj(l, x2d, norm_w3, mods4, w_in_t)
        o_gla = _gla(l, proj_bf, proj_f32, proj_small, wlr_pad, row3(gla_b_lr), row3(gla_norm_w))
        o_gdn = _gdn(l, proj_bf, proj_f32, proj_small, gdn_conv_w, alog_pad, dtb_pad, row3(gdn_norm_w))
        q_d, k_d = _diff_prep(l, proj_bf, cos2, sin2, row3(diff_q_norm_w), row3(diff_k_norm_w))
        o_diff = _diff_attn(l, q_d, k_d, proj_bf, proj_f32, diff_lambda, row3(diff_norm_w))
        x2d = _out_proj(l, o_gla, o_gdn, o_diff, w_out, x2d, mods4)
    return x2d.reshape(BATCH, SEQ, D_MODEL)
```

```python
import functools
import math

import jax
import jax.numpy as jnp
from jax import lax
from jax.experimental import pallas as pl
from jax.experimental.pallas import tpu as pltpu

F32 = jnp.float32
BF16 = jnp.bfloat16

D_MODEL = 2048
BATCH = 4
SEQ = 2048
DEPTH = 2
TOKENS = BATCH * SEQ

GLA_HEADS = 4
GLA_DK = 64
GLA_DV = 128
GLA_W = GLA_HEADS * GLA_DV
GLA_RANK = 16
GLA_TAU = 16.0
GDN_HEADS = 4
GDN_D = 128
GDN_W = GDN_HEADS * GDN_D
CONV_K = 4
DIFF_HEADS = 4
DIFF_D = 128
DIFF_DV = 256
DIFF_W = DIFF_HEADS * DIFF_DV
CHUNK = 64
N_CHUNKS = SEQ // CHUNK
ROPE_THETA = 10000.0
EPS = 1e-6
LOG2E = math.log2(math.e)

LANES = 128

_IN_SPLITS = (
    ("gq", GLA_HEADS * GLA_DK), ("gk", GLA_HEADS * GLA_DK), ("gv", GLA_W), ("glr", GLA_RANK), ("gz", GLA_W),
    ("dq", GDN_W), ("dk", GDN_W), ("dv", GDN_W), ("da", GDN_HEADS), ("db", GDN_HEADS), ("dz", GDN_W),
    ("aq", DIFF_HEADS * 2 * DIFF_D), ("ak", DIFF_HEADS * 2 * DIFF_D), ("av", DIFF_W), ("az", DIFF_W),
)
_IN_OFFSETS = {}
_off = 0
for _name, _width in _IN_SPLITS:
    _IN_OFFSETS[_name] = (_off, _off + _width)
    _off += _width
D_IN = _off

_BF_ORDER = ("aq", "ak", "av", "gq", "gk", "gv", "dq", "dk", "dv")
_F32_ORDER = ("gz", "dz", "az")
N_BF = 5632
N_F32 = 2048
PROJ_TN = 512
NB_BF = N_BF // PROJ_TN
NB_F32 = N_F32 // PROJ_TN
GLR_SRC = (_IN_OFFSETS["glr"][0] // LANES) * LANES
DAB_SRC = (_IN_OFFSETS["da"][0] // LANES) * LANES
GLR_WINDOW = 0
DAB_WINDOW = 1
GLR_LANE = _IN_OFFSETS["glr"][0] - GLR_SRC
DA_LANE = _IN_OFFSETS["da"][0] - DAB_SRC
DB_LANE = _IN_OFFSETS["db"][0] - DAB_SRC
assert GLR_LANE + GLA_RANK <= LANES and DB_LANE + GDN_HEADS <= LANES

VMEM_LIMIT = 48 * 1024 * 1024


def _sigmoid(x):
    return 1.0 / (1.0 + jnp.exp(-x))


def _silu(x):
    return x * _sigmoid(x)


def _softplus(x):
    return jnp.maximum(x, 0.0) + jnp.log1p(jnp.exp(-jnp.abs(x)))


def _dot(a, b):
    return jnp.dot(a, b, preferred_element_type=F32)


def _dot_nt(a, b):
    return lax.dot_general(a, b, (((1,), (1,)), ((), ())), preferred_element_type=F32)


def _dot_tn(a, b):
    return lax.dot_general(a, b, (((0,), (0,)), ((), ())), preferred_element_type=F32)


def _dot_f32(a, b):
    return jnp.dot(a, b, precision=lax.Precision.HIGHEST, preferred_element_type=F32)


def _dot_bf(a, b):
    return _dot(a.astype(BF16), b.astype(BF16))


def _rms(x, w):
    return x * lax.rsqrt(jnp.mean(x * x, axis=-1, keepdims=True) + EPS) * w


def _params(*semantics, vmem=VMEM_LIMIT):
    return pltpu.CompilerParams(dimension_semantics=semantics, vmem_limit_bytes=vmem)


ADA_TN = 1024


def _ada_kernel(c_ref, w_ref, b_ref, o_ref):
    c_act = _silu(c_ref[...])
    o_ref[0] = _dot(c_act.astype(BF16), w_ref[0].astype(BF16)) + b_ref[0]


def _ada_mod(c_pad, w_ada, b_ada):
    n3 = 3 * D_MODEL
    return pl.pallas_call(
        _ada_kernel,
        grid=(DEPTH, n3 // ADA_TN),
        in_specs=[
            pl.BlockSpec((8, D_MODEL), lambda l, n: (0, 0)),
            pl.BlockSpec((1, D_MODEL, ADA_TN), lambda l, n: (l, 0, n)),
            pl.BlockSpec((1, 1, ADA_TN), lambda l, n: (l, 0, n)),
        ],
        out_specs=pl.BlockSpec((1, 8, ADA_TN), lambda l, n: (l, 0, n)),
        out_shape=jax.ShapeDtypeStruct((DEPTH, 8, n3), F32),
        compiler_params=_params("arbitrary", "arbitrary"),
        name="ada_mod",
    )(c_pad, w_ada, b_ada.reshape(DEPTH, 1, n3))


def _rope_kernel(pos_ref, freq_ref, sign_ref, cos_ref, sin_ref):
    ang = pos_ref[0].astype(F32) * freq_ref[...]
    cos_ref[0] = jnp.cos(ang)
    sin_ref[0] = jnp.sin(ang) * sign_ref[...]


def _rope_tables(positions):
    half = DIFF_D // 2
    inv_freq = ROPE_THETA ** (-jnp.arange(half, dtype=F32) / half)
    freq2 = jnp.concatenate([inv_freq, inv_freq]).reshape(1, DIFF_D)
    sign = jnp.concatenate([-jnp.ones((half,), F32), jnp.ones((half,), F32)]).reshape(1, DIFF_D)
    tab = jax.ShapeDtypeStruct((BATCH, SEQ, DIFF_D), F32)
    return pl.pallas_call(
        _rope_kernel,
        grid=(BATCH,),
        in_specs=[
            pl.BlockSpec((1, SEQ, 1), lambda b: (b, 0, 0)),
            pl.BlockSpec((1, DIFF_D), lambda b: (0, 0)),
            pl.BlockSpec((1, DIFF_D), lambda b: (0, 0)),
        ],
        out_specs=[pl.BlockSpec((1, SEQ, DIFF_D), lambda b: (b, 0, 0))] * 2,
        out_shape=[tab, tab],
        compiler_params=_params("arbitrary"),
        name="rope_tables",
    )(positions.reshape(BATCH, SEQ, 1), freq2, sign)


PROJ_TM = 1024
PROJ_ROWS = 256


def _inproj_kernel(src_ref, x_ref, nw_ref, shift_ref, scale_ref, w_ref, wg_ref, wd_ref,
                   ob_ref, of_ref, os_ref, h_ref):
    del src_ref
    n = pl.program_id(1)

    @pl.when(n == 0)
    def _():
        gain = nw_ref[0] * (1.0 + scale_ref[0, 0])
        shift = shift_ref[0, 0]
        w = w_ref[...].astype(BF16)
        w_small = jnp.concatenate([wg_ref[...], wd_ref[...]], axis=0).astype(BF16)
        for i in range(PROJ_TM // PROJ_ROWS):
            rows = slice(i * PROJ_ROWS, (i + 1) * PROJ_ROWS)
            x = x_ref[rows, :]
            h = (x * lax.rsqrt(jnp.mean(x * x, axis=-1, keepdims=True) + EPS) * gain + shift).astype(BF16)
            h_ref[rows, :] = h
            ob_ref[rows, :] = _dot_nt(h, w).astype(BF16)
            os_ref[rows, :] = _dot_nt(h, w_small)

    @pl.when(jnp.logical_and(n > 0, n < NB_BF))
    def _():
        ob_ref[...] = _dot_nt(h_ref[...], w_ref[...].astype(BF16)).astype(BF16)

    @pl.when(n >= NB_BF)
    def _():
        of_ref[...] = _dot_nt(h_ref[...], w_ref[...].astype(BF16))


def _w_in_sources():
    starts = []
    for name in _BF_ORDER + _F32_ORDER:
        lo, hi = _IN_OFFSETS[name]
        starts += list(range(lo, hi, 256))
    firsts = starts[0::2]
    assert all(b == a + 256 for a, b in zip(firsts, starts[1::2])), "each 512 block must be contiguous"
    assert len(firsts) * PROJ_TN == N_BF + N_F32
    return firsts


def _in_proj(l, x2d, norm_w3, mods4, w_in_t):
    per_batch = SEQ // PROJ_TM
    sublanes = 8
    assert all(s % sublanes == 0 for s in _w_in_sources())
    starts = jnp.asarray([s // sublanes for s in _w_in_sources()], jnp.int32)

    def window(first_row):
        return pl.BlockSpec((pl.Squeezed(), pl.Element(LANES), pl.Element(D_MODEL)),
                            lambda m, n, src: (l, first_row, 0))

    grid_spec = pltpu.PrefetchScalarGridSpec(
        num_scalar_prefetch=1,
        grid=(TOKENS // PROJ_TM, NB_BF + NB_F32),
        in_specs=[
            pl.BlockSpec((PROJ_TM, D_MODEL), lambda m, n, src: (m, 0)),
            pl.BlockSpec((1, 1, D_MODEL), lambda m, n, src: (l, 0, 0)),
            pl.BlockSpec((1, 1, 1, D_MODEL), lambda m, n, src: (l, m // per_batch, 0, 0)),
            pl.BlockSpec((1, 1, 1, D_MODEL), lambda m, n, src: (l, m // per_batch, 0, 1)),
            pl.BlockSpec((pl.Squeezed(), pl.Element(PROJ_TN), pl.Element(D_MODEL)),
                         lambda m, n, src: (l, src[n] * sublanes, 0)),
            window(GLR_SRC),
            window(DAB_SRC),
        ],
        out_specs=[
            pl.BlockSpec((PROJ_TM, PROJ_TN), lambda m, n, src: (m, jnp.minimum(n, NB_BF - 1))),
            pl.BlockSpec((PROJ_TM, PROJ_TN), lambda m, n, src: (m, jnp.maximum(n - NB_BF, 0))),
            pl.BlockSpec((PROJ_TM, 2 * LANES), lambda m, n, src: (m, 0)),
        ],
        scratch_shapes=[pltpu.VMEM((PROJ_TM, D_MODEL), BF16)],
    )
    return pl.pallas_call(
        _inproj_kernel,
        grid_spec=grid_spec,
        out_shape=[
            jax.ShapeDtypeStruct((TOKENS, N_BF), BF16),
            jax.ShapeDtypeStruct((TOKENS, N_F32), F32),
            jax.ShapeDtypeStruct((TOKENS, 2 * LANES), F32),
        ],
        compiler_params=_params("arbitrary", "arbitrary"),
        name="in_proj",
    )(starts, x2d, norm_w3, mods4, mods4, w_in_t, w_in_t, w_in_t)


GLA_GROUP = 4


def _gla_kernel(gq_ref, gk_ref, gv_ref, gz_ref, sm_ref, wlr_ref, blr_ref, nw_ref, o_ref, la_ref):
    z = _dot_f32(sm_ref[...], wlr_ref[0]) + blr_ref[0]
    la_ref[...] = -_softplus(-z) * (1.0 / GLA_TAU)

    row = lax.broadcasted_iota(jnp.int32, (CHUNK, CHUNK), 0)
    col = lax.broadcasted_iota(jnp.int32, (CHUNK, CHUNK), 1)
    incl = col <= row
    tril = jnp.where(incl, 1.0, 0.0)
    nw = nw_ref[0]
    heads = range(GLA_HEADS)
    group = range(GLA_GROUP)

    def kcols(x, h):
        return x[:, h * GLA_DK:(h + 1) * GLA_DK]

    def vcols(x, h):
        return x[:, h * GLA_DV:(h + 1) * GLA_DV]

    def body(gi, states):
        rs = [pl.ds(pl.multiple_of((gi * GLA_GROUP + c) * CHUNK, CHUNK), CHUNK) for c in group]
        bc = [_dot_f32(tril, la_ref[r, :]) for r in rs]
        qe, ke, kt, dec, v = [], [], [], [], []
        for c in group:
            b_last = bc[c][CHUNK - 1:CHUNK, :]
            q = gq_ref[rs[c], :].astype(F32) * GLA_DK ** -0.5
            k = gk_ref[rs[c], :].astype(F32)
            qe.append((q * jnp.exp(bc[c])).astype(BF16))
            ke.append((k * jnp.exp(-bc[c])).astype(BF16))
            kt.append((k * jnp.exp(b_last - bc[c])).astype(BF16))
            dec.append(jnp.exp(b_last))
            v.append(gv_ref[rs[c], :])
        att = [[jnp.where(incl, _dot_nt(kcols(qe[c], h), kcols(ke[c], h)), 0.0).astype(BF16) for h in heads]
               for c in group]
        o_intra = [[_dot(att[c][h], vcols(v[c], h)) for h in heads] for c in group]
        d_state = [[_dot_tn(vcols(v[c], h), kcols(kt[c], h)) for h in heads] for c in group]
        st = list(states)
        for c in group:
            gz = gz_ref[rs[c], :]
            outs = []
            for h in heads:
                o = o_intra[c][h] + _dot_nt(kcols(qe[c], h), st[h].astype(BF16))
                st[h] = st[h] * kcols(dec[c], h) + d_state[c][h]
                outs.append(_rms(o, nw) * _silu(vcols(gz, h)))
            o_ref[rs[c], :] = jnp.concatenate(outs, axis=-1).astype(BF16)
        return tuple(st)

    lax.fori_loop(0, N_CHUNKS // GLA_GROUP, body,
                  tuple(jnp.zeros((GLA_DV, GLA_DK), F32) for _ in heads))


def _gla(l, proj_bf, proj_f32, proj_small, wlr_pad, blr, nw):
    return pl.pallas_call(
        _gla_kernel,
        grid=(BATCH,),
        in_specs=[
            pl.BlockSpec((SEQ, 256), lambda b: (b, 12)),
            pl.BlockSpec((SEQ, 256), lambda b: (b, 13)),
            pl.BlockSpec((SEQ, GLA_W), lambda b: (b, 7)),
            pl.BlockSpec((SEQ, GLA_W), lambda b: (b, 0)),
            pl.BlockSpec((SEQ, LANES), lambda b: (b, GLR_WINDOW)),
            pl.BlockSpec((1, LANES, GLA_HEADS * GLA_DK), lambda b: (l, 0, 0)),
            pl.BlockSpec((1, 1, GLA_HEADS * GLA_DK), lambda b: (l, 0, 0)),
            pl.BlockSpec((1, 1, GLA_DV), lambda b: (l, 0, 0)),
        ],
        out_specs=pl.BlockSpec((SEQ, GLA_W), lambda b: (b, 0)),
        out_shape=jax.ShapeDtypeStruct((TOKENS, GLA_W), BF16),
        scratch_shapes=[pltpu.VMEM((SEQ, GLA_HEADS * GLA_DK), F32)],
        compiler_params=_params("arbitrary"),
        name="gla",
    )(proj_bf, proj_bf, proj_bf, proj_f32, proj_small, wlr_pad, blr, nw)


GDN_HALO = 16
GDN_GROUP = 2
GDN_ROWS = GDN_GROUP * CHUNK
N_GROUPS = N_CHUNKS // GDN_GROUP


def _heads(x, h):
    return x[:, h * GDN_D:(h + 1) * GDN_D]


def _gdn_kernel(dq_ref, dk_ref, dv_ref, dz_ref, sm_ref, cw_ref, alog_ref, dtb_ref, nw_ref, o_ref,
                pq_s, pk_s, pv_s, pg_s, pb_s, u_s, w_s, qe_s, kt_s, qk_s, cd_s, st_s):
    heads = range(GDN_HEADS)
    row = lax.broadcasted_iota(jnp.int32, (CHUNK, CHUNK), 0)
    col = lax.broadcasted_iota(jnp.int32, (CHUNK, CHUNK), 1)
    incl = col <= row
    strict = col < row
    tril = jnp.where(incl, 1.0, 0.0)
    eye = jnp.where(col == row, 1.0, 0.0)
    block_bits = jnp.bitwise_xor(row, col)
    nw = nw_ref[0]
    cw = cw_ref[0]
    neg_a = -jnp.exp(alog_ref[0])
    dtb = dtb_ref[0]

    def conv(x_ref, src, w, first_block):
        cur = x_ref[pl.ds(src, GDN_ROWS), :].astype(F32)
        if first_block:
            prev = jnp.zeros((GDN_HALO, GDN_W), F32)
        else:
            prev = x_ref[pl.ds(pl.multiple_of(src - GDN_HALO, GDN_HALO), GDN_HALO), :].astype(F32)
        win = jnp.concatenate([prev, cur], axis=0)
        first = GDN_HALO - (CONV_K - 1)
        acc = win[first:first + GDN_ROWS] * w[0:1]
        for j in range(1, CONV_K):
            acc = acc + win[first + j:first + j + GDN_ROWS] * w[j:j + 1]
        return _silu(acc)

    def l2n(x, scale):
        parts = []
        for h in heads:
            xh = _heads(x, h)
            parts.append(xh * (lax.rsqrt(jnp.sum(xh * xh, axis=-1, keepdims=True) + EPS) * scale))
        return jnp.concatenate(parts, axis=-1)

    def group_rows(g):
        return pl.multiple_of(g * GDN_ROWS, GDN_ROWS)

    def phase0_steps(g, slot, first_block=False):
        src = 0 if first_block else group_rows(g)

        def q_step():
            pq_s[slot] = l2n(conv(dq_ref, src, cw[:, 0:GDN_W], first_block), GDN_D ** -0.5)

        def k_step():
            pk_s[slot] = l2n(conv(dk_ref, src, cw[:, GDN_W:2 * GDN_W], first_block), 1.0)

        def v_step():
            pv_s[slot] = conv(dv_ref, src, cw[:, 2 * GDN_W:3 * GDN_W], first_block)

        def gate_step():
            sm = sm_ref[pl.ds(src, GDN_ROWS), :]
            pg_s[slot] = neg_a * _softplus(sm + dtb)
            pb_s[slot] = _sigmoid(sm)

        return [q_step, k_step, v_step, gate_step]

    for step in phase0_steps(0, 0, first_block=True):
        step()

    def wy_group(slot, fillers):
        fillers = list(fillers)

        def emit():
            if fillers:
                fillers.pop(0)()

        group = range(GDN_GROUP)
        units = [(c, h) for c in group for h in heads]
        crows = [slice(c * CHUNK, (c + 1) * CHUNK) for c in group]
        gcs = [_dot_f32(tril, pg_s[slot, cr, :]) for cr in crows]
        gcs_t = [x.T for x in gcs]
        beta_all = [pb_s[slot, cr, :] for cr in crows]
        qn = [pq_s[slot, cr, :] for cr in crows]
        kn = [pk_s[slot, cr, :] for cr in crows]
        v = [pv_s[slot, cr, :] for cr in crows]
        decay, eg, ekt, cd, k_beta, v_beta, kb = {}, {}, {}, {}, {}, {}, {}
        for c, h in units:
            gc_col = gcs[c][:, DA_LANE + h:DA_LANE + h + 1]
            gc_row = gcs_t[c][DA_LANE + h:DA_LANE + h + 1, :]
            decay[c, h] = jnp.exp(jnp.where(incl, gc_col - gc_row, -jnp.inf))
            gcb = jnp.broadcast_to(gc_col, (CHUNK, GDN_D))
            g_last = gcb[CHUNK - 1:CHUNK, :]
            eg[c, h] = jnp.exp(gcb)
            ekt[c, h] = jnp.exp(g_last - gcb)
            cd[c, h] = jnp.broadcast_to(jnp.exp(g_last), (8, GDN_D))
            b_col = beta_all[c][:, DB_LANE + h:DB_LANE + h + 1]
            k_beta[c, h] = _heads(kn[c], h) * b_col
            v_beta[c, h] = _heads(v[c], h) * b_col
            kb[c, h] = _heads(kn[c], h).astype(BF16)
        lower = {u: jnp.where(strict, _dot_nt(k_beta[u].astype(BF16), kb[u]) * decay[u], 0.0) for u in units}
        emit()
        base = 8
        in_base = lax.shift_right_logical(block_bits, int(math.log2(base))) == 0
        pw = {u: jnp.where(in_base, -lower[u], 0.0) for u in units}
        tq = dict(pw)
        pw = {u: _dot_bf(pw[u], pw[u]) for u in units}
        emit()
        both = {u: _dot_bf(jnp.concatenate([tq[u], pw[u]], axis=0), pw[u]) for u in units}
        emit()
        tq = {u: tq[u] + pw[u] + both[u][0:CHUNK] for u in units}
        pw = {u: both[u][CHUNK:2 * CHUNK] for u in units}
        inv = {u: eye + (tq[u] + pw[u] + _dot_bf(tq[u], pw[u])) for u in units}
        emit()
        for size in (base, 2 * base, 4 * base):
            inside = lax.shift_right_logical(block_bits, int(math.log2(size))) == 1
            nt = {u: _dot_bf(jnp.where(inside, lower[u], 0.0), inv[u]) for u in units}
            emit()
            inv = {u: inv[u] - _dot_bf(inv[u], nt[u]) for u in units}
            emit()
        rhs = {u: jnp.concatenate([v_beta[u], k_beta[u] * eg[u]], axis=-1) for u in units}
        sol = {u: rhs[u] + _dot((inv[u] - eye).astype(BF16), rhs[u].astype(BF16)) for u in units}
        emit()
        qk = {(c, h): _dot_nt(_heads(qn[c], h).astype(BF16), kb[c, h]) * decay[c, h] for c, h in units}
        while fillers:
            emit()
        for c in group:
            cr = crows[c]
            u_s[slot, cr, :] = jnp.concatenate([sol[c, h][:, 0:GDN_D] for h in heads], axis=-1)
            w_s[slot, cr, :] = jnp.concatenate([sol[c, h][:, GDN_D:2 * GDN_D] for h in heads],
                                               axis=-1).astype(BF16)
            qe_s[slot, cr, :] = jnp.concatenate([_heads(qn[c], h) * eg[c, h] for h in heads],
                                                axis=-1).astype(BF16)
            kt_s[slot, cr, :] = jnp.concatenate([_heads(kn[c], h) * ekt[c, h] for h in heads],
                                                axis=-1).astype(BF16)
            qk_s[slot, cr, :] = jnp.concatenate([qk[c, h] for h in heads], axis=-1).astype(BF16)
            cd_s[slot, c] = jnp.concatenate([cd[c, h] for h in heads], axis=-1)

    def scan_steps(g, slot):
        box = {}
        steps = []
        for c in range(GDN_GROUP):
            cr = slice(c * CHUNK, (c + 1) * CHUNK)

            def first(c=c, cr=cr):
                state = [st_s[h] for h in heads] if c == 0 else box["state"]
                sb = [s.astype(BF16) for s in state]
                w = w_s[slot, cr, :]
                qe = qe_s[slot, cr, :]
                box["state"] = state
                box["ws"] = [_dot(_heads(w, h), sb[h]) for h in heads]
                box["qs"] = [_dot(_heads(qe, h), sb[h]) for h in heads]

            def second(c=c, cr=cr):
                u = u_s[slot, cr, :]
                kt = kt_s[slot, cr, :]
                qk = qk_s[slot, cr, :]
                cd = cd_s[slot, c]
                v_new = [(_heads(u, h) - box["ws"][h]).astype(BF16) for h in heads]
                o = [box["qs"][h] + _dot(qk[:, h * CHUNK:(h + 1) * CHUNK], v_new[h]) for h in heads]
                state = [box["state"][h] * _heads(cd, h)[0:1] + _dot_tn(_heads(kt, h), v_new[h]) for h in heads]
                if c == GDN_GROUP - 1:
                    for h in heads:
                        st_s[h] = state[h]
                box["state"] = state
                r = pl.ds(pl.multiple_of((g * GDN_GROUP + c) * CHUNK, CHUNK), CHUNK)
                o = jnp.concatenate([_rms(o[h], nw) for h in heads], axis=-1)
                o_ref[r, :] = (o * _silu(dz_ref[r, :])).astype(BF16)

            steps += [first, second]
        return steps

    def interleave(a, b):
        out = []
        for i in range(max(len(a), len(b))):
            out += a[i:i + 1] + b[i:i + 1]
        return out

    st_s[...] = jnp.zeros((GDN_HEADS, GDN_D, GDN_D), F32)
    wy_group(0, phase0_steps(1, 1))

    def pair(i, carry):
        g = 2 * i
        wy_group(1, interleave(scan_steps(g, 0), phase0_steps(g + 2, 0)))
        wy_group(0, interleave(scan_steps(g + 1, 1), phase0_steps(g + 3, 1)))
        return carry

    lax.fori_loop(0, (N_GROUPS - 2) // 2, pair, 0)
    wy_group(1, scan_steps(N_GROUPS - 2, 0))
    for step in scan_steps(N_GROUPS - 1, 1):
        step()


def _gdn(l, proj_bf, proj_f32, proj_small, conv_w, alog_pad, dtb_pad, nw):
    slot_wide = pltpu.VMEM((2, GDN_ROWS, GDN_W), F32)
    slot_wide_bf = pltpu.VMEM((2, GDN_ROWS, GDN_W), BF16)
    slot_narrow = pltpu.VMEM((2, GDN_ROWS, LANES), F32)
    return pl.pallas_call(
        _gdn_kernel,
        grid=(BATCH,),
        in_specs=[
            pl.BlockSpec((SEQ, GDN_W), lambda b: (b, 8)),
            pl.BlockSpec((SEQ, GDN_W), lambda b: (b, 9)),
            pl.BlockSpec((SEQ, GDN_W), lambda b: (b, 10)),
            pl.BlockSpec((SEQ, GDN_W), lambda b: (b, 1)),
            pl.BlockSpec((SEQ, LANES), lambda b: (b, DAB_WINDOW)),
            pl.BlockSpec((1, CONV_K, 3 * GDN_W), lambda b: (l, 0, 0)),
            pl.BlockSpec((1, 1, LANES), lambda b: (l, 0, 0)),
            pl.BlockSpec((1, 1, LANES), lambda b: (l, 0, 0)),
            pl.BlockSpec((1, 1, GDN_D), lambda b: (l, 0, 0)),
        ],
        out_specs=pl.BlockSpec((SEQ, GDN_W), lambda b: (b, 0)),
        out_shape=jax.ShapeDtypeStruct((TOKENS, GDN_W), BF16),
        scratch_shapes=[slot_wide, slot_wide, slot_wide,
                        slot_narrow, slot_narrow,
                        slot_wide,
                        slot_wide_bf, slot_wide_bf, slot_wide_bf,
                        pltpu.VMEM((2, GDN_ROWS, GDN_HEADS * CHUNK), BF16),
                        pltpu.VMEM((2, GDN_GROUP, 8, GDN_W), F32),
                        pltpu.VMEM((GDN_HEADS, GDN_D, GDN_D), F32)],
        compiler_params=_params("arbitrary"),
        name="gdn",
    )(proj_bf, proj_bf, proj_bf, proj_f32, proj_small, conv_w, alog_pad, dtb_pad, nw)


PREP_TS = 512
ATT_TQ = 256


def _diff_prep_kernel(aq_ref, ak_ref, cos_ref, sin_ref, qw_ref, kw_ref, q_ref, k_ref):
    cos2 = cos_ref[0]
    sin2 = sin_ref[0]
    r_i = lax.broadcasted_iota(jnp.int32, (DIFF_D, DIFF_D), 0)
    c_i = lax.broadcasted_iota(jnp.int32, (DIFF_D, DIFF_D), 1)
    swap_halves = jnp.where(jnp.bitwise_xor(r_i, c_i) == DIFF_D // 2, 1.0, 0.0).astype(BF16)
    for src, w_ref, dst, scale in ((aq_ref, qw_ref, q_ref, DIFF_D ** -0.5 * LOG2E), (ak_ref, kw_ref, k_ref, 1.0)):
        w = w_ref[0]
        for g in range(2 * DIFF_HEADS):
            cols = slice(g * DIFF_D, (g + 1) * DIFF_D)
            y = _rms(src[:, cols].astype(F32), w)
            y = y * cos2 + _dot(y.astype(BF16), swap_halves) * sin2
            dst[:, cols] = (y * scale).astype(BF16)


def _diff_prep(l, proj_bf, cos2, sin2, qw, kw):
    per_batch = SEQ // PREP_TS
    width = 2 * DIFF_HEADS * DIFF_D
    tab_spec = pl.BlockSpec((1, PREP_TS, DIFF_D), lambda i: (i // per_batch, i % per_batch, 0))
    out = jax.ShapeDtypeStruct((TOKENS, width), BF16)
    return pl.pallas_call(
        _diff_prep_kernel,
        grid=(TOKENS // PREP_TS,),
        in_specs=[
            pl.BlockSpec((PREP_TS, width), lambda i: (i, 0)),
            pl.BlockSpec((PREP_TS, width), lambda i: (i, 1)),
            tab_spec, tab_spec,
            pl.BlockSpec((1, 1, DIFF_D), lambda i: (l, 0, 0)),
            pl.BlockSpec((1, 1, DIFF_D), lambda i: (l, 0, 0)),
        ],
        out_specs=[pl.BlockSpec((PREP_TS, width), lambda i: (i, 0))] * 2,
        out_shape=[out, out],
        compiler_params=_params("arbitrary"),
        name="diff_prep",
    )(proj_bf, proj_bf, cos2, sin2, qw, kw)


def _diff_attn_kernel(q1_ref, q2_ref, k1_ref, k2_ref, v_ref, az_ref, lam_ref, nw_ref, o_ref, *, lam_init):
    lv = lam_ref[0]
    lam = (jnp.exp(jnp.sum(lv[0:1] * lv[1:2], axis=-1, keepdims=True))
           - jnp.exp(jnp.sum(lv[2:3] * lv[3:4], axis=-1, keepdims=True)) + lam_init)
    nw = nw_ref[0]
    row = lax.broadcasted_iota(jnp.int32, (ATT_TQ, ATT_TQ), 0)
    col = lax.broadcasted_iota(jnp.int32, (ATT_TQ, ATT_TQ), 1)
    causal = col <= row
    q_refs = (q1_ref, q2_ref)
    k_refs = (k1_ref, k2_ref)
    n_blocks = SEQ // ATT_TQ

    def scores(i):
        start = i * ATT_TQ
        rows = slice(start, start + ATT_TQ)
        out = []
        for m in range(2):
            q = q_refs[m][rows, :]
            s_diag = jnp.where(causal, _dot_nt(q, k_refs[m][rows, :]), -jnp.inf)
            s_past = _dot_nt(q, k_refs[m][0:start, :]) if i > 0 else None
            out.append((s_diag, s_past))
        return out

    s_next = scores(0)
    for i in range(n_blocks):
        start = i * ATT_TQ
        rows = slice(start, start + ATT_TQ)
        s_cur = s_next
        if i + 1 < n_blocks:
            s_next = scores(i + 1)
        ps, inv_l = [], []
        for s_diag, s_past in s_cur:
            mx = jnp.max(s_diag, axis=-1, keepdims=True)
            if i > 0:
                mx = jnp.maximum(mx, jnp.max(s_past, axis=-1, keepdims=True))
            p_diag = jnp.exp2(s_diag - mx)
            l = jnp.sum(p_diag, axis=-1, keepdims=True)
            p_past = None
            if i > 0:
                p_past = jnp.exp2(s_past - mx)
                l = l + jnp.sum(p_past, axis=-1, keepdims=True)
                p_past = p_past.astype(BF16)
            ps.append((p_diag.astype(BF16), p_past))
            inv_l.append(1.0 / l)
        pv = []
        for p_diag, p_past in ps:
            acc = _dot(p_diag, v_ref[rows, :])
            if i > 0:
                acc = acc + _dot(p_past, v_ref[0:start, :])
            pv.append(acc)
        o = pv[0] * inv_l[0] - pv[1] * (lam * inv_l[1])
        o = _rms(o, nw) * (1.0 - lam_init)
        o_ref[rows, :] = (o * _silu(az_ref[rows, :])).astype(BF16)


def _diff_attn(l, q_d, k_d, proj_bf, proj_f32, lam, nw):
    lam_init = 0.8 - 0.6 * math.exp(-0.3 * l)
    return pl.pallas_call(
        functools.partial(_diff_attn_kernel, lam_init=lam_init),
        grid=(BATCH, DIFF_HEADS),
        in_specs=[
            pl.BlockSpec((SEQ, DIFF_D), lambda b, h: (b, 2 * h)),
            pl.BlockSpec((SEQ, DIFF_D), lambda b, h: (b, 2 * h + 1)),
            pl.BlockSpec((SEQ, DIFF_D), lambda b, h: (b, 2 * h)),
            pl.BlockSpec((SEQ, DIFF_D), lambda b, h: (b, 2 * h + 1)),
            pl.BlockSpec((SEQ, DIFF_DV), lambda b, h: (b, 8 + h)),
            pl.BlockSpec((SEQ, DIFF_DV), lambda b, h: (b, 4 + h)),
            pl.BlockSpec((1, 4, DIFF_D), lambda b, h: (l, 0, 0)),
            pl.BlockSpec((1, 1, DIFF_DV), lambda b, h: (l, 0, 0)),
        ],
        out_specs=pl.BlockSpec((SEQ, DIFF_DV), lambda b, h: (b, h)),
        out_shape=jax.ShapeDtypeStruct((TOKENS, DIFF_W), BF16),
        compiler_params=_params("arbitrary", "arbitrary"),
        name="diff_attn",
    )(q_d, q_d, k_d, k_d, proj_bf, proj_f32, lam, nw)


OUT_TM = 512


OUT_WROWS = 256


def _outproj_kernel(og_ref, od_ref, oa_ref, w_ref, x_ref, gate_ref, o_ref, wb_ref):
    @pl.when(pl.program_id(0) == 0)
    def _():
        def cast(i, carry):
            rows = pl.ds(pl.multiple_of(i * OUT_WROWS, OUT_WROWS), OUT_WROWS)
            wb_ref[rows, :] = w_ref[0, rows, :].astype(BF16)
            return carry

        lax.fori_loop(0, D_MODEL // OUT_WROWS, cast, 0)

    y = _dot(og_ref[...], wb_ref[0:GLA_W, :])
    y = y + _dot(od_ref[...], wb_ref[GLA_W:GLA_W + GDN_W, :])
    y = y + _dot(oa_ref[...], wb_ref[GLA_W + GDN_W:, :])
    o_ref[...] = x_ref[...] + gate_ref[0, 0] * y


def _out_proj(l, o_gla, o_gdn, o_diff, w_out, x2d, mods4):
    per_batch = SEQ // OUT_TM
    return pl.pallas_call(
        _outproj_kernel,
        grid=(TOKENS // OUT_TM,),
        in_specs=[
            pl.BlockSpec((OUT_TM, GLA_W), lambda m: (m, 0)),
            pl.BlockSpec((OUT_TM, GDN_W), lambda m: (m, 0)),
            pl.BlockSpec((OUT_TM, DIFF_W), lambda m: (m, 0)),
            pl.BlockSpec((1, D_MODEL, D_MODEL), lambda m: (l, 0, 0), pipeline_mode=pl.Buffered(1)),
            pl.BlockSpec((OUT_TM, D_MODEL), lambda m: (m, 0)),
            pl.BlockSpec((1, 1, 1, D_MODEL), lambda m: (l, m // per_batch, 0, 2)),
        ],
        out_specs=pl.BlockSpec((OUT_TM, D_MODEL), lambda m: (m, 0)),
        out_shape=jax.ShapeDtypeStruct((TOKENS, D_MODEL), F32),
        scratch_shapes=[pltpu.VMEM((D_MODEL, D_MODEL), BF16)],
        compiler_params=_params("arbitrary", vmem=52 * 1024 * 1024),
        name="out_proj",
    )(o_gla, o_gdn, o_diff, w_out, x2d, mods4)


def _lane_place(vecs, lane):
    n = vecs.shape[1]
    return jnp.pad(vecs.astype(F32), ((0, 0), (lane, LANES - lane - n))).reshape(DEPTH, 1, LANES)


def kernel(x, c, positions, norm_w, w_ada, b_ada, w_in, gla_w_lr, gla_b_lr, gla_norm_w, gdn_conv_w, gdn_a_log,
           gdn_dt_bias, gdn_norm_w, diff_q_norm_w, diff_k_norm_w, diff_lambda, diff_norm_w, w_out):
    c_pad = jnp.pad(c, ((0, 8 - BATCH), (0, 0)))
    mods4 = _ada_mod(c_pad, w_ada, b_ada).reshape(DEPTH, 8, 1, 3 * D_MODEL)
    cos2, sin2 = _rope_tables(positions)
    w_in_t = jnp.swapaxes(w_in, 1, 2)
    row3 = lambda p: p.reshape(DEPTH, 1, p.shape[-1])
    norm_w3 = row3(norm_w)
    wlr_pad = jnp.pad(gla_w_lr, ((0, 0), (GLR_LANE, LANES - GLR_LANE - GLA_RANK), (0, 0)))
    alog_pad = _lane_place(gdn_a_log, DA_LANE)
    dtb_pad = _lane_place(gdn_dt_bias, DA_LANE)
    x2d = x.reshape(TOKENS, D_MODEL)
    for l in range(DEPTH):
        proj_bf, proj_f32, proj_small = _in_proj(l, x2d, norm_w3, mods4, w_in_t)
        o_gla = _gla(l, proj_bf, proj_f32, proj_small, wlr_pad, row3(gla_b_lr), row3(gla_norm_w))
        o_gdn = _gdn(l, proj_bf, proj_f32, proj_small, gdn_conv_w, alog_pad, dtb_pad, row3(gdn_norm_w))
        q_d, k_d = _diff_prep(l, proj_bf, cos2, sin2, row3(diff_q_norm_w), row3(diff_k_norm_w))
        o_diff = _diff_attn(l, q_d, k_d, proj_bf, proj_f32, diff_lambda, row3(diff_norm_w))
        x2d = _out_proj(l, o_gla, o_gdn, o_diff, w_out, x2d, mods4)
    return x2d.reshape(BATCH, SEQ, D_MODEL)
```

```python
import functools
import math

import jax
import jax.numpy as jnp
from jax import lax
from jax.experimental import pallas as pl
from jax.experimental.pallas import tpu as pltpu

F32 = jnp.float32
BF16 = jnp.bfloat16

D_MODEL = 2048
BATCH = 4
SEQ = 2048
DEPTH = 2
TOKENS = BATCH * SEQ

GLA_HEADS = 4
GLA_DK = 64
GLA_DV = 128
GLA_W = GLA_HEADS * GLA_DV
GLA_RANK = 16
GLA_TAU = 16.0
GDN_HEADS = 4
GDN_D = 128
GDN_W = GDN_HEADS * GDN_D
CONV_K = 4
DIFF_HEADS = 4
DIFF_D = 128
DIFF_DV = 256
DIFF_W = DIFF_HEADS * DIFF_DV
CHUNK = 64
N_CHUNKS = SEQ // CHUNK
ROPE_THETA = 10000.0
EPS = 1e-6
LOG2E = math.log2(math.e)

LANES = 128

_IN_SPLITS = (
    ("gq", GLA_HEADS * GLA_DK), ("gk", GLA_HEADS * GLA_DK), ("gv", GLA_W), ("glr", GLA_RANK), ("gz", GLA_W),
    ("dq", GDN_W), ("dk", GDN_W), ("dv", GDN_W), ("da", GDN_HEADS), ("db", GDN_HEADS), ("dz", GDN_W),
    ("aq", DIFF_HEADS * 2 * DIFF_D), ("ak", DIFF_HEADS * 2 * DIFF_D), ("av", DIFF_W), ("az", DIFF_W),
)
_IN_OFFSETS = {}
_off = 0
for _name, _width in _IN_SPLITS:
    _IN_OFFSETS[_name] = (_off, _off + _width)
    _off += _width
D_IN = _off

_BF_ORDER = ("aq", "ak", "av", "gq", "gk", "gv", "dq", "dk", "dv")
_F32_ORDER = ("gz", "dz", "az")
N_BF = 5632
N_F32 = 2048
PROJ_TN = 512
NB_BF = N_BF // PROJ_TN
NB_F32 = N_F32 // PROJ_TN
GLR_SRC = (_IN_OFFSETS["glr"][0] // LANES) * LANES
DAB_SRC = (_IN_OFFSETS["da"][0] // LANES) * LANES
GLR_WINDOW = 0
DAB_WINDOW = 1
GLR_LANE = _IN_OFFSETS["glr"][0] - GLR_SRC
DA_LANE = _IN_OFFSETS["da"][0] - DAB_SRC
DB_LANE = _IN_OFFSETS["db"][0] - DAB_SRC
assert GLR_LANE + GLA_RANK <= LANES and DB_LANE + GDN_HEADS <= LANES

VMEM_LIMIT = 48 * 1024 * 1024


def _sigmoid(x):
    return 1.0 / (1.0 + jnp.exp(-x))


def _silu(x):
    return x * _sigmoid(x)


def _softplus(x):
    return jnp.maximum(x, 0.0) + jnp.log1p(jnp.exp(-jnp.abs(x)))


def _dot(a, b):
    return jnp.dot(a, b, preferred_element_type=F32)


def _dot_nt(a, b):
    return lax.dot_general(a, b, (((1,), (1,)), ((), ())), preferred_element_type=F32)


def _dot_tn(a, b):
    return lax.dot_general(a, b, (((0,), (0,)), ((), ())), preferred_element_type=F32)


def _dot_f32(a, b):
    return jnp.dot(a, b, precision=lax.Precision.HIGHEST, preferred_element_type=F32)


def _dot_bf(a, b):
    return _dot(a.astype(BF16), b.astype(BF16))


def _rms(x, w):
    return x * lax.rsqrt(jnp.mean(x * x, axis=-1, keepdims=True) + EPS) * w


def _params(*semantics, vmem=VMEM_LIMIT):
    return pltpu.CompilerParams(dimension_semantics=semantics, vmem_limit_bytes=vmem)


ADA_TN = 1024


def _ada_kernel(c_ref, w_ref, b_ref, o_ref):
    c_act = _silu(c_ref[...])
    o_ref[0] = _dot(c_act.astype(BF16), w_ref[0].astype(BF16)) + b_ref[0]


def _ada_mod(c_pad, w_ada, b_ada):
    n3 = 3 * D_MODEL
    return pl.pallas_call(
        _ada_kernel,
        grid=(DEPTH, n3 // ADA_TN),
        in_specs=[
            pl.BlockSpec((8, D_MODEL), lambda l, n: (0, 0)),
            pl.BlockSpec((1, D_MODEL, ADA_TN), lambda l, n: (l, 0, n)),
            pl.BlockSpec((1, 1, ADA_TN), lambda l, n: (l, 0, n)),
        ],
        out_specs=pl.BlockSpec((1, 8, ADA_TN), lambda l, n: (l, 0, n)),
        out_shape=jax.ShapeDtypeStruct((DEPTH, 8, n3), F32),
        compiler_params=_params("arbitrary", "arbitrary"),
        name="ada_mod",
    )(c_pad, w_ada, b_ada.reshape(DEPTH, 1, n3))


def _rope_kernel(pos_ref, freq_ref, sign_ref, cos_ref, sin_ref):
    ang = pos_ref[0].astype(F32) * freq_ref[...]
    cos_ref[0] = jnp.cos(ang)
    sin_ref[0] = jnp.sin(ang) * sign_ref[...]


def _rope_tables(positions):
    half = DIFF_D // 2
    inv_freq = ROPE_THETA ** (-jnp.arange(half, dtype=F32) / half)
    freq2 = jnp.concatenate([inv_freq, inv_freq]).reshape(1, DIFF_D)
    sign = jnp.concatenate([-jnp.ones((half,), F32), jnp.ones((half,), F32)]).reshape(1, DIFF_D)
    tab = jax.ShapeDtypeStruct((BATCH, SEQ, DIFF_D), F32)
    return pl.pallas_call(
        _rope_kernel,
        grid=(BATCH,),
        in_specs=[
            pl.BlockSpec((1, SEQ, 1), lambda b: (b, 0, 0)),
            pl.BlockSpec((1, DIFF_D), lambda b: (0, 0)),
            pl.BlockSpec((1, DIFF_D), lambda b: (0, 0)),
        ],
        out_specs=[pl.BlockSpec((1, SEQ, DIFF_D), lambda b: (b, 0, 0))] * 2,
        out_shape=[tab, tab],
        compiler_params=_params("arbitrary"),
        name="rope_tables",
    )(positions.reshape(BATCH, SEQ, 1), freq2, sign)


PROJ_TM = 1024
PROJ_ROWS = 256


def _inproj_kernel(src_ref, x_ref, nw_ref, shift_ref, scale_ref, w_ref, wg_ref, wd_ref,
                   ob_ref, of_ref, os_ref, h_ref):
    del src_ref
    n = pl.program_id(1)

    @pl.when(n == 0)
    def _():
        gain = nw_ref[0] * (1.0 + scale_ref[0, 0])
        shift = shift_ref[0, 0]
        w = w_ref[...].astype(BF16)
        w_small = jnp.concatenate([wg_ref[...], wd_ref[...]], axis=0).astype(BF16)
        for i in range(PROJ_TM // PROJ_ROWS):
            rows = slice(i * PROJ_ROWS, (i + 1) * PROJ_ROWS)
            x = x_ref[rows, :]
            h = (x * lax.rsqrt(jnp.mean(x * x, axis=-1, keepdims=True) + EPS) * gain + shift).astype(BF16)
            h_ref[rows, :] = h
            ob_ref[rows, :] = _dot_nt(h, w).astype(BF16)
            os_ref[rows, :] = _dot_nt(h, w_small)

    @pl.when(jnp.logical_and(n > 0, n < NB_BF))
    def _():
        ob_ref[...] = _dot_nt(h_ref[...], w_ref[...].astype(BF16)).astype(BF16)

    @pl.when(n >= NB_BF)
    def _():
        of_ref[...] = _dot_nt(h_ref[...], w_ref[...].astype(BF16))


def _w_in_sources():
    starts = []
    for name in _BF_ORDER + _F32_ORDER:
        lo, hi = _IN_OFFSETS[name]
        starts += list(range(lo, hi, 256))
    firsts = starts[0::2]
    assert all(b == a + 256 for a, b in zip(firsts, starts[1::2])), "each 512 block must be contiguous"
    assert len(firsts) * PROJ_TN == N_BF + N_F32
    return firsts


def _in_proj(l, x2d, norm_w3, mods4, w_in_t):
    per_batch = SEQ // PROJ_TM
    sublanes = 8
    assert all(s % sublanes == 0 for s in _w_in_sources())
    starts = jnp.asarray([s // sublanes for s in _w_in_sources()], jnp.int32)

    def window(first_row):
        return pl.BlockSpec((pl.Squeezed(), pl.Element(LANES), pl.Element(D_MODEL)),
                            lambda m, n, src: (l, first_row, 0))

    grid_spec = pltpu.PrefetchScalarGridSpec(
        num_scalar_prefetch=1,
        grid=(TOKENS // PROJ_TM, NB_BF + NB_F32),
        in_specs=[
            pl.BlockSpec((PROJ_TM, D_MODEL), lambda m, n, src: (m, 0)),
            pl.BlockSpec((1, 1, D_MODEL), lambda m, n, src: (l, 0, 0)),
            pl.BlockSpec((1, 1, 1, D_MODEL), lambda m, n, src: (l, m // per_batch, 0, 0)),
            pl.BlockSpec((1, 1, 1, D_MODEL), lambda m, n, src: (l, m // per_batch, 0, 1)),
            pl.BlockSpec((pl.Squeezed(), pl.Element(PROJ_TN), pl.Element(D_MODEL)),
                         lambda m, n, src: (l, src[n] * sublanes, 0)),
            window(GLR_SRC),
            window(DAB_SRC),
        ],
        out_specs=[
            pl.BlockSpec((PROJ_TM, PROJ_TN), lambda m, n, src: (m, jnp.minimum(n, NB_BF - 1))),
            pl.BlockSpec((PROJ_TM, PROJ_TN), lambda m, n, src: (m, jnp.maximum(n - NB_BF, 0))),
            pl.BlockSpec((PROJ_TM, 2 * LANES), lambda m, n, src: (m, 0)),
        ],
        scratch_shapes=[pltpu.VMEM((PROJ_TM, D_MODEL), BF16)],
    )
    return pl.pallas_call(
        _inproj_kernel,
        grid_spec=grid_spec,
        out_shape=[
            jax.ShapeDtypeStruct((TOKENS, N_BF), BF16),
            jax.ShapeDtypeStruct((TOKENS, N_F32), F32),
            jax.ShapeDtypeStruct((TOKENS, 2 * LANES), F32),
        ],
        compiler_params=_params("arbitrary", "arbitrary"),
        name="in_proj",
    )(starts, x2d, norm_w3, mods4, mods4, w_in_t, w_in_t, w_in_t)


GLA_GROUP = 4


def _gla_kernel(gq_ref, gk_ref, gv_ref, gz_ref, sm_ref, wlr_ref, blr_ref, nw_ref, o_ref, la_ref):
    z = _dot_f32(sm_ref[...], wlr_ref[0]) + blr_ref[0]
    la_ref[...] = -_softplus(-z) * (1.0 / GLA_TAU)

    row = lax.broadcasted_iota(jnp.int32, (CHUNK, CHUNK), 0)
    col = lax.broadcasted_iota(jnp.int32, (CHUNK, CHUNK), 1)
    incl = col <= row
    tril = jnp.where(incl, 1.0, 0.0)
    nw = nw_ref[0]
    heads = range(GLA_HEADS)
    group = range(GLA_GROUP)

    def kcols(x, h):
        return x[:, h * GLA_DK:(h + 1) * GLA_DK]

    def vcols(x, h):
        return x[:, h * GLA_DV:(h + 1) * GLA_DV]

    def body(gi, states):
        rs = [pl.ds(pl.multiple_of((gi * GLA_GROUP + c) * CHUNK, CHUNK), CHUNK) for c in group]
        bc = [_dot_f32(tril, la_ref[r, :]) for r in rs]
        qe, ke, kt, dec, v = [], [], [], [], []
        for c in group:
            b_last = bc[c][CHUNK - 1:CHUNK, :]
            q = gq_ref[rs[c], :].astype(F32) * GLA_DK ** -0.5
            k = gk_ref[rs[c], :].astype(F32)
            qe.append((q * jnp.exp(bc[c])).astype(BF16))
            ke.append((k * jnp.exp(-bc[c])).astype(BF16))
            kt.append((k * jnp.exp(b_last - bc[c])).astype(BF16))
            dec.append(jnp.exp(b_last))
            v.append(gv_ref[rs[c], :])
        att = [[jnp.where(incl, _dot_nt(kcols(qe[c], h), kcols(ke[c], h)), 0.0).astype(BF16) for h in heads]
               for c in group]
        o_intra = [[_dot(att[c][h], vcols(v[c], h)) for h in heads] for c in group]
        d_state = [[_dot_tn(vcols(v[c], h), kcols(kt[c], h)) for h in heads] for c in group]
        st = list(states)
        for c in group:
            gz = gz_ref[rs[c], :]
            outs = []
            for h in heads:
                o = o_intra[c][h] + _dot_nt(kcols(qe[c], h), st[h].astype(BF16))
                st[h] = st[h] * kcols(dec[c], h) + d_state[c][h]
                outs.append(_rms(o, nw) * _silu(vcols(gz, h)))
            o_ref[rs[c], :] = jnp.concatenate(outs, axis=-1).astype(BF16)
        return tuple(st)

    lax.fori_loop(0, N_CHUNKS // GLA_GROUP, body,
                  tuple(jnp.zeros((GLA_DV, GLA_DK), F32) for _ in heads))


def _gla(l, proj_bf, proj_f32, proj_small, wlr_pad, blr, nw):
    return pl.pallas_call(
        _gla_kernel,
        grid=(BATCH,),
        in_specs=[
            pl.BlockSpec((SEQ, 256), lambda b: (b, 12)),
            pl.BlockSpec((SEQ, 256), lambda b: (b, 13)),
            pl.BlockSpec((SEQ, GLA_W), lambda b: (b, 7)),
            pl.BlockSpec((SEQ, GLA_W), lambda b: (b, 0)),
            pl.BlockSpec((SEQ, LANES), lambda b: (b, GLR_WINDOW)),
            pl.BlockSpec((1, LANES, GLA_HEADS * GLA_DK), lambda b: (l, 0, 0)),
            pl.BlockSpec((1, 1, GLA_HEADS * GLA_DK), lambda b: (l, 0, 0)),
            pl.BlockSpec((1, 1, GLA_DV), lambda b: (l, 0, 0)),
        ],
        out_specs=pl.BlockSpec((SEQ, GLA_W), lambda b: (b, 0)),
        out_shape=jax.ShapeDtypeStruct((TOKENS, GLA_W), BF16),
        scratch_shapes=[pltpu.VMEM((SEQ, GLA_HEADS * GLA_DK), F32)],
        compiler_params=_params("arbitrary"),
        name="gla",
    )(proj_bf, proj_bf, proj_bf, proj_f32, proj_small, wlr_pad, blr, nw)


GDN_HALO = 128
GDN_GROUP = 2
GDN_ROWS = GDN_GROUP * CHUNK
N_GROUPS = N_CHUNKS // GDN_GROUP


def _heads(x, h):
    return x[:, h * GDN_D:(h + 1) * GDN_D]


def _gdn_kernel(dq_ref, dk_ref, dv_ref, dz_ref, sm_ref, cw_ref, alog_ref, dtb_ref, nw_ref, o_ref,
                pq_s, pk_s, pv_s, pg_s, pb_s, u_s, w_s, qe_s, kt_s, qk_s, cd_s, st_s):
    heads = range(GDN_HEADS)
    row = lax.broadcasted_iota(jnp.int32, (CHUNK, CHUNK), 0)
    col = lax.broadcasted_iota(jnp.int32, (CHUNK, CHUNK), 1)
    incl = col <= row
    strict = col < row
    tril = jnp.where(incl, 1.0, 0.0)
    eye = jnp.where(col == row, 1.0, 0.0)
    block_bits = jnp.bitwise_xor(row, col)
    nw = nw_ref[0]
    cw = cw_ref[0]
    neg_a = -jnp.exp(alog_ref[0])
    dtb = dtb_ref[0]

    s_row = lax.broadcasted_iota(jnp.int32, (GDN_ROWS, GDN_HALO + GDN_ROWS), 0)
    s_col = lax.broadcasted_iota(jnp.int32, (GDN_ROWS, GDN_HALO + GDN_ROWS), 1)
    shift_mat = jnp.concatenate(
        [jnp.where(s_col == s_row + (GDN_HALO - (CONV_K - 1) + j), 1.0, 0.0) for j in range(CONV_K - 1)],
        axis=0).astype(BF16)

    def conv(x_ref, src, w, first_block):
        cur = x_ref[pl.ds(src, GDN_ROWS), :]
        if first_block:
            prev = jnp.zeros((GDN_HALO, GDN_W), BF16)
        else:
            prev = x_ref[pl.ds(pl.multiple_of(src - GDN_HALO, GDN_HALO), GDN_HALO), :]
        taps = _dot(shift_mat, jnp.concatenate([prev, cur], axis=0))
        acc = cur.astype(F32) * w[CONV_K - 1:CONV_K]
        for j in range(CONV_K - 1):
            acc = acc + taps[j * GDN_ROWS:(j + 1) * GDN_ROWS] * w[j:j + 1]
        return _silu(acc)

    def l2n(x, scale):
        parts = []
        for h in heads:
            xh = _heads(x, h)
            parts.append(xh * (lax.rsqrt(jnp.sum(xh * xh, axis=-1, keepdims=True) + EPS) * scale))
        return jnp.concatenate(parts, axis=-1)

    def group_rows(g):
        return pl.multiple_of(g * GDN_ROWS, GDN_ROWS)

    def phase0_steps(g, slot, first_block=False):
        src = 0 if first_block else group_rows(g)

        def q_step():
            pq_s[slot] = l2n(conv(dq_ref, src, cw[:, 0:GDN_W], first_block), GDN_D ** -0.5)

        def k_step():
            pk_s[slot] = l2n(conv(dk_ref, src, cw[:, GDN_W:2 * GDN_W], first_block), 1.0)

        def v_step():
            pv_s[slot] = conv(dv_ref, src, cw[:, 2 * GDN_W:3 * GDN_W], first_block)

        def gate_step():
            sm = sm_ref[pl.ds(src, GDN_ROWS), :]
            pg_s[slot] = neg_a * _softplus(sm + dtb)
            pb_s[slot] = _sigmoid(sm)

        return [q_step, k_step, v_step, gate_step]

    for step in phase0_steps(0, 0, first_block=True):
        step()

    def wy_group(slot, fillers):
        fillers = list(fillers)

        def emit():
            if fillers:
                fillers.pop(0)()

        group = range(GDN_GROUP)
        units = [(c, h) for c in group for h in heads]
        crows = [slice(c * CHUNK, (c + 1) * CHUNK) for c in group]
        gcs = [_dot_f32(tril, pg_s[slot, cr, :]) for cr in crows]
        gcs_t = [x.T for x in gcs]
        beta_all = [pb_s[slot, cr, :] for cr in crows]
        qn = [pq_s[slot, cr, :] for cr in crows]
        kn = [pk_s[slot, cr, :] for cr in crows]
        v = [pv_s[slot, cr, :] for cr in crows]
        decay, eg, ekt, cd, k_beta, v_beta, kb = {}, {}, {}, {}, {}, {}, {}
        for c, h in units:
            gc_col = gcs[c][:, DA_LANE + h:DA_LANE + h + 1]
            gc_row = gcs_t[c][DA_LANE + h:DA_LANE + h + 1, :]
            decay[c, h] = jnp.exp(jnp.where(incl, gc_col - gc_row, -jnp.inf))
            gcb = jnp.broadcast_to(gc_col, (CHUNK, GDN_D))
            g_last = gcb[CHUNK - 1:CHUNK, :]
            eg[c, h] = jnp.exp(gcb)
            ekt[c, h] = jnp.exp(g_last - gcb)
            cd[c, h] = jnp.broadcast_to(jnp.exp(g_last), (8, GDN_D))
            b_col = beta_all[c][:, DB_LANE + h:DB_LANE + h + 1]
            k_beta[c, h] = _heads(kn[c], h) * b_col
            v_beta[c, h] = _heads(v[c], h) * b_col
            kb[c, h] = _heads(kn[c], h).astype(BF16)
        lower = {u: jnp.where(strict, _dot_nt(k_beta[u].astype(BF16), kb[u]) * decay[u], 0.0) for u in units}
        emit()
        base = 8
        in_base = lax.shift_right_logical(block_bits, int(math.log2(base))) == 0
        pw = {u: jnp.where(in_base, -lower[u], 0.0) for u in units}
        tq = dict(pw)
        pw = {u: _dot_bf(pw[u], pw[u]) for u in units}
        emit()
        both = {u: _dot_bf(jnp.concatenate([tq[u], pw[u]], axis=0), pw[u]) for u in units}
        emit()
        tq = {u: tq[u] + pw[u] + both[u][0:CHUNK] for u in units}
        pw = {u: both[u][CHUNK:2 * CHUNK] for u in units}
        inv = {u: eye + (tq[u] + pw[u] + _dot_bf(tq[u], pw[u])) for u in units}
        emit()
        for size in (base, 2 * base, 4 * base):
            inside = lax.shift_right_logical(block_bits, int(math.log2(size))) == 1
            nt = {u: _dot_bf(jnp.where(inside, lower[u], 0.0), inv[u]) for u in units}
            emit()
            inv = {u: inv[u] - _dot_bf(inv[u], nt[u]) for u in units}
            emit()
        rhs = {u: jnp.concatenate([v_beta[u], k_beta[u] * eg[u]], axis=-1) for u in units}
        sol = {u: rhs[u] + _dot((inv[u] - eye).astype(BF16), rhs[u].astype(BF16)) for u in units}
        emit()
        qk = {(c, h): _dot_nt(_heads(qn[c], h).astype(BF16), kb[c, h]) * decay[c, h] for c, h in units}
        while fillers:
            emit()
        for c in group:
            cr = crows[c]
            u_s[slot, cr, :] = jnp.concatenate([sol[c, h][:, 0:GDN_D] for h in heads], axis=-1)
            w_s[slot, cr, :] = jnp.concatenate([sol[c, h][:, GDN_D:2 * GDN_D] for h in heads],
                                               axis=-1).astype(BF16)
            qe_s[slot, cr, :] = jnp.concatenate([_heads(qn[c], h) * eg[c, h] for h in heads],
                                                axis=-1).astype(BF16)
            kt_s[slot, cr, :] = jnp.concatenate([_heads(kn[c], h) * ekt[c, h] for h in heads],
                                                axis=-1).astype(BF16)
            qk_s[slot, cr, :] = jnp.concatenate([qk[c, h] for h in heads], axis=-1).astype(BF16)
            cd_s[slot, c] = jnp.concatenate([cd[c, h] for h in heads], axis=-1)

    def scan_steps(g, slot):
        box = {}
        steps = []
        for c in range(GDN_GROUP):
            cr = slice(c * CHUNK, (c + 1) * CHUNK)

            def first(c=c, cr=cr):
                state = [st_s[h] for h in heads] if c == 0 else box["state"]
                sb = [s.astype(BF16) for s in state]
                w = w_s[slot, cr, :]
                qe = qe_s[slot, cr, :]
                box["state"] = state
                box["ws"] = [_dot(_heads(w, h), sb[h]) for h in heads]
                box["qs"] = [_dot(_heads(qe, h), sb[h]) for h in heads]

            def second(c=c, cr=cr):
                u = u_s[slot, cr, :]
                kt = kt_s[slot, cr, :]
                qk = qk_s[slot, cr, :]
                cd = cd_s[slot, c]
                v_new = [(_heads(u, h) - box["ws"][h]).astype(BF16) for h in heads]
                o = [box["qs"][h] + _dot(qk[:, h * CHUNK:(h + 1) * CHUNK], v_new[h]) for h in heads]
                state = [box["state"][h] * _heads(cd, h)[0:1] + _dot_tn(_heads(kt, h), v_new[h]) for h in heads]
                if c == GDN_GROUP - 1:
                    for h in heads:
                        st_s[h] = state[h]
                box["state"] = state
                r = pl.ds(pl.multiple_of((g * GDN_GROUP + c) * CHUNK, CHUNK), CHUNK)
                o = jnp.concatenate([_rms(o[h], nw) for h in heads], axis=-1)
                o_ref[r, :] = (o * _silu(dz_ref[r, :])).astype(BF16)

            steps += [first, second]
        return steps

    def interleave(a, b):
        out = []
        for i in range(max(len(a), len(b))):
            out += a[i:i + 1] + b[i:i + 1]
        return out

    st_s[...] = jnp.zeros((GDN_HEADS, GDN_D, GDN_D), F32)
    wy_group(0, phase0_steps(1, 1))

    def pair(i, carry):
        g = 2 * i
        wy_group(1, interleave(scan_steps(g, 0), phase0_steps(g + 2, 0)))
        wy_group(0, interleave(scan_steps(g + 1, 1), phase0_steps(g + 3, 1)))
        return carry

    lax.fori_loop(0, (N_GROUPS - 2) // 2, pair, 0)
    wy_group(1, scan_steps(N_GROUPS - 2, 0))
    for step in scan_steps(N_GROUPS - 1, 1):
        step()


def _gdn(l, proj_bf, proj_f32, proj_small, conv_w, alog_pad, dtb_pad, nw):
    slot_wide = pltpu.VMEM((2, GDN_ROWS, GDN_W), F32)
    slot_wide_bf = pltpu.VMEM((2, GDN_ROWS, GDN_W), BF16)
    slot_narrow = pltpu.VMEM((2, GDN_ROWS, LANES), F32)
    return pl.pallas_call(
        _gdn_kernel,
        grid=(BATCH,),
        in_specs=[
            pl.BlockSpec((SEQ, GDN_W), lambda b: (b, 8)),
            pl.BlockSpec((SEQ, GDN_W), lambda b: (b, 9)),
            pl.BlockSpec((SEQ, GDN_W), lambda b: (b, 10)),
            pl.BlockSpec((SEQ, GDN_W), lambda b: (b, 1)),
            pl.BlockSpec((SEQ, LANES), lambda b: (b, DAB_WINDOW)),
            pl.BlockSpec((1, CONV_K, 3 * GDN_W), lambda b: (l, 0, 0)),
            pl.BlockSpec((1, 1, LANES), lambda b: (l, 0, 0)),
            pl.BlockSpec((1, 1, LANES), lambda b: (l, 0, 0)),
            pl.BlockSpec((1, 1, GDN_D), lambda b: (l, 0, 0)),
        ],
        out_specs=pl.BlockSpec((SEQ, GDN_W), lambda b: (b, 0)),
        out_shape=jax.ShapeDtypeStruct((TOKENS, GDN_W), BF16),
        scratch_shapes=[slot_wide, slot_wide, slot_wide,
                        slot_narrow, slot_narrow,
                        slot_wide,
                        slot_wide_bf, slot_wide_bf, slot_wide_bf,
                        pltpu.VMEM((2, GDN_ROWS, GDN_HEADS * CHUNK), BF16),
                        pltpu.VMEM((2, GDN_GROUP, 8, GDN_W), F32),
                        pltpu.VMEM((GDN_HEADS, GDN_D, GDN_D), F32)],
        compiler_params=_params("arbitrary"),
        name="gdn",
    )(proj_bf, proj_bf, proj_bf, proj_f32, proj_small, conv_w, alog_pad, dtb_pad, nw)


PREP_TS = 512
ATT_TQ = 256


def _diff_prep_kernel(aq_ref, ak_ref, cos_ref, sin_ref, qw_ref, kw_ref, q_ref, k_ref):
    cos2 = cos_ref[0]
    sin2 = sin_ref[0]
    r_i = lax.broadcasted_iota(jnp.int32, (DIFF_D, DIFF_D), 0)
    c_i = lax.broadcasted_iota(jnp.int32, (DIFF_D, DIFF_D), 1)
    swap_halves = jnp.where(jnp.bitwise_xor(r_i, c_i) == DIFF_D // 2, 1.0, 0.0).astype(BF16)
    for src, w_ref, dst, scale in ((aq_ref, qw_ref, q_ref, DIFF_D ** -0.5 * LOG2E), (ak_ref, kw_ref, k_ref, 1.0)):
        w = w_ref[0]
        for g in range(2 * DIFF_HEADS):
            cols = slice(g * DIFF_D, (g + 1) * DIFF_D)
            y = _rms(src[:, cols].astype(F32), w)
            y = y * cos2 + _dot(y.astype(BF16), swap_halves) * sin2
            dst[:, cols] = (y * scale).astype(BF16)


def _diff_prep(l, proj_bf, cos2, sin2, qw, kw):
    per_batch = SEQ // PREP_TS
    width = 2 * DIFF_HEADS * DIFF_D
    tab_spec = pl.BlockSpec((1, PREP_TS, DIFF_D), lambda i: (i // per_batch, i % per_batch, 0))
    out = jax.ShapeDtypeStruct((TOKENS, width), BF16)
    return pl.pallas_call(
        _diff_prep_kernel,
        grid=(TOKENS // PREP_TS,),
        in_specs=[
            pl.BlockSpec((PREP_TS, width), lambda i: (i, 0)),
            pl.BlockSpec((PREP_TS, width), lambda i: (i, 1)),
            tab_spec, tab_spec,
            pl.BlockSpec((1, 1, DIFF_D), lambda i: (l, 0, 0)),
            pl.BlockSpec((1, 1, DIFF_D), lambda i: (l, 0, 0)),
        ],
        out_specs=[pl.BlockSpec((PREP_TS, width), lambda i: (i, 0))] * 2,
        out_shape=[out, out],
        compiler_params=_params("arbitrary"),
        name="diff_prep",
    )(proj_bf, proj_bf, cos2, sin2, qw, kw)


def _diff_attn_kernel(q1_ref, q2_ref, k1_ref, k2_ref, v_ref, az_ref, lam_ref, nw_ref, o_ref, *, lam_init):
    lv = lam_ref[0]
    lam = (jnp.exp(jnp.sum(lv[0:1] * lv[1:2], axis=-1, keepdims=True))
           - jnp.exp(jnp.sum(lv[2:3] * lv[3:4], axis=-1, keepdims=True)) + lam_init)
    nw = nw_ref[0]
    row = lax.broadcasted_iota(jnp.int32, (ATT_TQ, ATT_TQ), 0)
    col = lax.broadcasted_iota(jnp.int32, (ATT_TQ, ATT_TQ), 1)
    causal = col <= row
    q_refs = (q1_ref, q2_ref)
    k_refs = (k1_ref, k2_ref)
    n_blocks = SEQ // ATT_TQ

    def scores(i):
        start = i * ATT_TQ
        rows = slice(start, start + ATT_TQ)
        out = []
        for m in range(2):
            q = q_refs[m][rows, :]
            s_diag = jnp.where(causal, _dot_nt(q, k_refs[m][rows, :]), -jnp.inf)
            s_past = _dot_nt(q, k_refs[m][0:start, :]) if i > 0 else None
            out.append((s_diag, s_past))
        return out

    s_next = scores(0)
    for i in range(n_blocks):
        start = i * ATT_TQ
        rows = slice(start, start + ATT_TQ)
        s_cur = s_next
        if i + 1 < n_blocks:
            s_next = scores(i + 1)
        ps, inv_l = [], []
        for s_diag, s_past in s_cur:
            mx = jnp.max(s_diag, axis=-1, keepdims=True)
            if i > 0:
                mx = jnp.maximum(mx, jnp.max(s_past, axis=-1, keepdims=True))
            p_diag = jnp.exp2(s_diag - mx)
            l = jnp.sum(p_diag, axis=-1, keepdims=True)
            p_past = None
            if i > 0:
                p_past = jnp.exp2(s_past - mx)
                l = l + jnp.sum(p_past, axis=-1, keepdims=True)
                p_past = p_past.astype(BF16)
            ps.append((p_diag.astype(BF16), p_past))
            inv_l.append(1.0 / l)
        pv = []
        for p_diag, p_past in ps:
            acc = _dot(p_diag, v_ref[rows, :])
            if i > 0:
                acc = acc + _dot(p_past, v_ref[0:start, :])
            pv.append(acc)
        o = pv[0] * inv_l[0] - pv[1] * (lam * inv_l[1])
        o = _rms(o, nw) * (1.0 - lam_init)
        o_ref[rows, :] = (o * _silu(az_ref[rows, :])).astype(BF16)


def _diff_attn(l, q_d, k_d, proj_bf, proj_f32, lam, nw):
    lam_init = 0.8 - 0.6 * math.exp(-0.3 * l)
    return pl.pallas_call(
        functools.partial(_diff_attn_kernel, lam_init=lam_init),
        grid=(BATCH, DIFF_HEADS),
        in_specs=[
            pl.BlockSpec((SEQ, DIFF_D), lambda b, h: (b, 2 * h)),
            pl.BlockSpec((SEQ, DIFF_D), lambda b, h: (b, 2 * h + 1)),
            pl.BlockSpec((SEQ, DIFF_D), lambda b, h: (b, 2 * h)),
            pl.BlockSpec((SEQ, DIFF_D), lambda b, h: (b, 2 * h + 1)),
            pl.BlockSpec((SEQ, DIFF_DV), lambda b, h: (b, 8 + h)),
            pl.BlockSpec((SEQ, DIFF_DV), lambda b, h: (b, 4 + h)),
            pl.BlockSpec((1, 4, DIFF_D), lambda b, h: (l, 0, 0)),
            pl.BlockSpec((1, 1, DIFF_DV), lambda b, h: (l, 0, 0)),
        ],
        out_specs=pl.BlockSpec((SEQ, DIFF_DV), lambda b, h: (b, h)),
        out_shape=jax.ShapeDtypeStruct((TOKENS, DIFF_W), BF16),
        compiler_params=_params("arbitrary", "arbitrary"),
        name="diff_attn",
    )(q_d, q_d, k_d, k_d, proj_bf, proj_f32, lam, nw)


OUT_TM = 512


OUT_WROWS = 256


def _outproj_kernel(og_ref, od_ref, oa_ref, w_ref, x_ref, gate_ref, o_ref, wb_ref):
    @pl.when(pl.program_id(0) == 0)
    def _():
        def cast(i, carry):
            rows = pl.ds(pl.multiple_of(i * OUT_WROWS, OUT_WROWS), OUT_WROWS)
            wb_ref[rows, :] = w_ref[0, rows, :].astype(BF16)
            return carry

        lax.fori_loop(0, D_MODEL // OUT_WROWS, cast, 0)

    y = _dot(og_ref[...], wb_ref[0:GLA_W, :])
    y = y + _dot(od_ref[...], wb_ref[GLA_W:GLA_W + GDN_W, :])
    y = y + _dot(oa_ref[...], wb_ref[GLA_W + GDN_W:, :])
    o_ref[...] = x_ref[...] + gate_ref[0, 0] * y


def _out_proj(l, o_gla, o_gdn, o_diff, w_out, x2d, mods4):
    per_batch = SEQ // OUT_TM
    return pl.pallas_call(
        _outproj_kernel,
        grid=(TOKENS // OUT_TM,),
        in_specs=[
            pl.BlockSpec((OUT_TM, GLA_W), lambda m: (m, 0)),
            pl.BlockSpec((OUT_TM, GDN_W), lambda m: (m, 0)),
            pl.BlockSpec((OUT_TM, DIFF_W), lambda m: (m, 0)),
            pl.BlockSpec((1, D_MODEL, D_MODEL), lambda m: (l, 0, 0), pipeline_mode=pl.Buffered(1)),
            pl.BlockSpec((OUT_TM, D_MODEL), lambda m: (m, 0)),
            pl.BlockSpec((1, 1, 1, D_MODEL), lambda m: (l, m // per_batch, 0, 2)),
        ],
        out_specs=pl.BlockSpec((OUT_TM, D_MODEL), lambda m: (m, 0)),
        out_shape=jax.ShapeDtypeStruct((TOKENS, D_MODEL), F32),
        scratch_shapes=[pltpu.VMEM((D_MODEL, D_MODEL), BF16)],
        compiler_params=_params("arbitrary", vmem=52 * 1024 * 1024),
        name="out_proj",
    )(o_gla, o_gdn, o_diff, w_out, x2d, mods4)


def _lane_place(vecs, lane):
    n = vecs.shape[1]
    return jnp.pad(vecs.astype(F32), ((0, 0), (lane, LANES - lane - n))).reshape(DEPTH, 1, LANES)


def kernel(x, c, positions, norm_w, w_ada, b_ada, w_in, gla_w_lr, gla_b_lr, gla_norm_w, gdn_conv_w, gdn_a_log,
           gdn_dt_bias, gdn_norm_w, diff_q_norm_w, diff_k_norm_w, diff_lambda, diff_norm_w, w_out):
    c_pad = jnp.pad(c, ((0, 8 - BATCH), (0, 0)))
    mods4 = _ada_mod(c_pad, w_ada, b_ada).reshape(DEPTH, 8, 1, 3 * D_MODEL)
    cos2, sin2 = _rope_tables(positions)
    w_in_t = jnp.swapaxes(w_in, 1, 2)
    row3 = lambda p: p.reshape(DEPTH, 1, p.shape[-1])
    norm_w3 = row3(norm_w)
    wlr_pad = jnp.pad(gla_w_lr, ((0, 0), (GLR_LANE, LANES - GLR_LANE - GLA_RANK), (0, 0)))
    alog_pad = _lane_place(gdn_a_log, DA_LANE)
    dtb_pad = _lane_place(gdn_dt_bias, DA_LANE)
    x2d = x.reshape(TOKENS, D_MODEL)
    for l in range(DEPTH):
        proj_bf, proj_f32, proj_small = _in_proj(l, x2d, norm_w3, mods4, w_in_t)
        o_gla = _gla(l, proj_bf, proj_f32, proj_small, wlr_pad, row3(gla_b_lr), row3(gla_norm_w))
        o_gdn = _gdn(l, proj_bf, proj_f32, proj_small, gdn_conv_w, alog_pad, dtb_pad, row3(gdn_norm_w))
        q_d, k_d = _diff_prep(l, proj_bf, cos2, sin2, row3(diff_q_norm_w), row3(diff_k_norm_w))
        o_diff = _diff_attn(l, q_d, k_d, proj_bf, proj_f32, diff_lambda, row3(diff_norm_w))
        x2d = _out_proj(l, o_gla, o_gdn, o_diff, w_out, x2d, mods4)
    return x2d.reshape(BATCH, SEQ, D_MODEL)
```

```python
import functools
import math

import jax
import jax.numpy as jnp
from jax import lax
from jax.experimental import pallas as pl
from jax.experimental.pallas import tpu as pltpu

F32 = jnp.float32
BF16 = jnp.bfloat16

D_MODEL = 2048
BATCH = 4
SEQ = 2048
DEPTH = 2
TOKENS = BATCH * SEQ

GLA_HEADS = 4
GLA_DK = 64
GLA_DV = 128
GLA_W = GLA_HEADS * GLA_DV
GLA_RANK = 16
GLA_TAU = 16.0
GDN_HEADS = 4
GDN_D = 128
GDN_W = GDN_HEADS * GDN_D
CONV_K = 4
DIFF_HEADS = 4
DIFF_D = 128
DIFF_DV = 256
DIFF_W = DIFF_HEADS * DIFF_DV
CHUNK = 64
N_CHUNKS = SEQ // CHUNK
ROPE_THETA = 10000.0
EPS = 1e-6
LOG2E = math.log2(math.e)

LANES = 128

_IN_SPLITS = (
    ("gq", GLA_HEADS * GLA_DK), ("gk", GLA_HEADS * GLA_DK), ("gv", GLA_W), ("glr", GLA_RANK), ("gz", GLA_W),
    ("dq", GDN_W), ("dk", GDN_W), ("dv", GDN_W), ("da", GDN_HEADS), ("db", GDN_HEADS), ("dz", GDN_W),
    ("aq", DIFF_HEADS * 2 * DIFF_D), ("ak", DIFF_HEADS * 2 * DIFF_D), ("av", DIFF_W), ("az", DIFF_W),
)
_IN_OFFSETS = {}
_off = 0
for _name, _width in _IN_SPLITS:
    _IN_OFFSETS[_name] = (_off, _off + _width)
    _off += _width
D_IN = _off

_BF_ORDER = ("aq", "ak", "av", "gq", "gk", "gv", "dq", "dk", "dv")
_F32_ORDER = ("gz", "dz", "az")
N_BF = 5632
N_F32 = 2048
PROJ_TN = 512
NB_BF = N_BF // PROJ_TN
NB_F32 = N_F32 // PROJ_TN
GLR_SRC = (_IN_OFFSETS["glr"][0] // LANES) * LANES
DAB_SRC = (_IN_OFFSETS["da"][0] // LANES) * LANES
GLR_WINDOW = 0
DAB_WINDOW = 1
GLR_LANE = _IN_OFFSETS["glr"][0] - GLR_SRC
DA_LANE = _IN_OFFSETS["da"][0] - DAB_SRC
DB_LANE = _IN_OFFSETS["db"][0] - DAB_SRC
assert GLR_LANE + GLA_RANK <= LANES and DB_LANE + GDN_HEADS <= LANES

VMEM_LIMIT = 48 * 1024 * 1024


def _sigmoid(x):
    return 1.0 / (1.0 + jnp.exp(-x))


def _silu(x):
    return x * _sigmoid(x)


def _softplus(x):
    return jnp.maximum(x, 0.0) + jnp.log1p(jnp.exp(-jnp.abs(x)))


def _dot(a, b):
    return jnp.dot(a, b, preferred_element_type=F32)


def _dot_nt(a, b):
    return lax.dot_general(a, b, (((1,), (1,)), ((), ())), preferred_element_type=F32)


def _dot_tn(a, b):
    return lax.dot_general(a, b, (((0,), (0,)), ((), ())), preferred_element_type=F32)


def _dot_f32(a, b):
    return jnp.dot(a, b, precision=lax.Precision.HIGHEST, preferred_element_type=F32)


def _dot_bf(a, b):
    return _dot(a.astype(BF16), b.astype(BF16))


def _rms(x, w):
    return x * lax.rsqrt(jnp.mean(x * x, axis=-1, keepdims=True) + EPS) * w


def _params(*semantics, vmem=VMEM_LIMIT):
    return pltpu.CompilerParams(dimension_semantics=semantics, vmem_limit_bytes=vmem)


ADA_TN = 1024


def _ada_kernel(c_ref, w_ref, b_ref, o_ref):
    c_act = _silu(c_ref[...])
    o_ref[0] = _dot(c_act.astype(BF16), w_ref[0].astype(BF16)) + b_ref[0]


def _ada_mod(c_pad, w_ada, b_ada):
    n3 = 3 * D_MODEL
    return pl.pallas_call(
        _ada_kernel,
        grid=(DEPTH, n3 // ADA_TN),
        in_specs=[
            pl.BlockSpec((8, D_MODEL), lambda l, n: (0, 0)),
            pl.BlockSpec((1, D_MODEL, ADA_TN), lambda l, n: (l, 0, n)),
            pl.BlockSpec((1, 1, ADA_TN), lambda l, n: (l, 0, n)),
        ],
        out_specs=pl.BlockSpec((1, 8, ADA_TN), lambda l, n: (l, 0, n)),
        out_shape=jax.ShapeDtypeStruct((DEPTH, 8, n3), F32),
        compiler_params=_params("arbitrary", "arbitrary"),
        name="ada_mod",
    )(c_pad, w_ada, b_ada.reshape(DEPTH, 1, n3))


def _rope_kernel(pos_ref, freq_ref, sign_ref, cos_ref, sin_ref):
    ang = pos_ref[0].astype(F32) * freq_ref[...]
    cos_ref[0] = jnp.cos(ang)
    sin_ref[0] = jnp.sin(ang) * sign_ref[...]


def _rope_tables(positions):
    half = DIFF_D // 2
    inv_freq = ROPE_THETA ** (-jnp.arange(half, dtype=F32) / half)
    freq2 = jnp.concatenate([inv_freq, inv_freq]).reshape(1, DIFF_D)
    sign = jnp.concatenate([-jnp.ones((half,), F32), jnp.ones((half,), F32)]).reshape(1, DIFF_D)
    tab = jax.ShapeDtypeStruct((BATCH, SEQ, DIFF_D), F32)
    return pl.pallas_call(
        _rope_kernel,
        grid=(BATCH,),
        in_specs=[
            pl.BlockSpec((1, SEQ, 1), lambda b: (b, 0, 0)),
            pl.BlockSpec((1, DIFF_D), lambda b: (0, 0)),
            pl.BlockSpec((1, DIFF_D), lambda b: (0, 0)),
        ],
        out_specs=[pl.BlockSpec((1, SEQ, DIFF_D), lambda b: (b, 0, 0))] * 2,
        out_shape=[tab, tab],
        compiler_params=_params("arbitrary"),
        name="rope_tables",
    )(positions.reshape(BATCH, SEQ, 1), freq2, sign)


PROJ_TM = 1024
PROJ_ROWS = 256


def _inproj_kernel(src_ref, x_ref, nw_ref, shift_ref, scale_ref, w_ref, wg_ref, wd_ref,
                   ob_ref, of_ref, os_ref, h_ref):
    del src_ref
    n = pl.program_id(1)

    @pl.when(n == 0)
    def _():
        gain = nw_ref[0] * (1.0 + scale_ref[0, 0])
        shift = shift_ref[0, 0]
        w = w_ref[...].astype(BF16)
        w_small = jnp.concatenate([wg_ref[...], wd_ref[...]], axis=0).astype(BF16)
        for i in range(PROJ_TM // PROJ_ROWS):
            rows = slice(i * PROJ_ROWS, (i + 1) * PROJ_ROWS)
            x = x_ref[rows, :]
            h = (x * lax.rsqrt(jnp.mean(x * x, axis=-1, keepdims=True) + EPS) * gain + shift).astype(BF16)
            h_ref[rows, :] = h
            ob_ref[rows, :] = _dot_nt(h, w).astype(BF16)
            os_ref[rows, :] = _dot_nt(h, w_small)

    @pl.when(jnp.logical_and(n > 0, n < NB_BF))
    def _():
        ob_ref[...] = _dot_nt(h_ref[...], w_ref[...].astype(BF16)).astype(BF16)

    @pl.when(n >= NB_BF)
    def _():
        of_ref[...] = _dot_nt(h_ref[...], w_ref[...].astype(BF16))


def _w_in_sources():
    starts = []
    for name in _BF_ORDER + _F32_ORDER:
        lo, hi = _IN_OFFSETS[name]
        starts += list(range(lo, hi, 256))
    firsts = starts[0::2]
    assert all(b == a + 256 for a, b in zip(firsts, starts[1::2])), "each 512 block must be contiguous"
    assert len(firsts) * PROJ_TN == N_BF + N_F32
    return firsts


def _in_proj(l, x2d, norm_w3, mods4, w_in_t):
    per_batch = SEQ // PROJ_TM
    sublanes = 8
    assert all(s % sublanes == 0 for s in _w_in_sources())
    starts = jnp.asarray([s // sublanes for s in _w_in_sources()], jnp.int32)

    def window(first_row):
        return pl.BlockSpec((pl.Squeezed(), pl.Element(LANES), pl.Element(D_MODEL)),
                            lambda m, n, src: (l, first_row, 0))

    grid_spec = pltpu.PrefetchScalarGridSpec(
        num_scalar_prefetch=1,
        grid=(TOKENS // PROJ_TM, NB_BF + NB_F32),
        in_specs=[
            pl.BlockSpec((PROJ_TM, D_MODEL), lambda m, n, src: (m, 0)),
            pl.BlockSpec((1, 1, D_MODEL), lambda m, n, src: (l, 0, 0)),
            pl.BlockSpec((1, 1, 1, D_MODEL), lambda m, n, src: (l, m // per_batch, 0, 0)),
            pl.BlockSpec((1, 1, 1, D_MODEL), lambda m, n, src: (l, m // per_batch, 0, 1)),
            pl.BlockSpec((pl.Squeezed(), pl.Element(PROJ_TN), pl.Element(D_MODEL)),
                         lambda m, n, src: (l, src[n] * sublanes, 0)),
            window(GLR_SRC),
            window(DAB_SRC),
        ],
        out_specs=[
            pl.BlockSpec((PROJ_TM, PROJ_TN), lambda m, n, src: (m, jnp.minimum(n, NB_BF - 1))),
            pl.BlockSpec((PROJ_TM, PROJ_TN), lambda m, n, src: (m, jnp.maximum(n - NB_BF, 0))),
            pl.BlockSpec((PROJ_TM, 2 * LANES), lambda m, n, src: (m, 0)),
        ],
        scratch_shapes=[pltpu.VMEM((PROJ_TM, D_MODEL), BF16)],
    )
    return pl.pallas_call(
        _inproj_kernel,
        grid_spec=grid_spec,
        out_shape=[
            jax.ShapeDtypeStruct((TOKENS, N_BF), BF16),
            jax.ShapeDtypeStruct((TOKENS, N_F32), F32),
            jax.ShapeDtypeStruct((TOKENS, 2 * LANES), F32),
        ],
        compiler_params=_params("arbitrary", "arbitrary"),
        name="in_proj",
    )(starts, x2d, norm_w3, mods4, mods4, w_in_t, w_in_t, w_in_t)


GLA_GROUP = 8


def _gla_kernel(gq_ref, gk_ref, gv_ref, gz_ref, sm_ref, wlr_ref, blr_ref, nw_ref, o_ref, la_ref):
    z = _dot_f32(sm_ref[...], wlr_ref[0]) + blr_ref[0]
    la_ref[...] = -_softplus(-z) * (1.0 / GLA_TAU)

    row = lax.broadcasted_iota(jnp.int32, (CHUNK, CHUNK), 0)
    col = lax.broadcasted_iota(jnp.int32, (CHUNK, CHUNK), 1)
    incl = col <= row
    tril = jnp.where(incl, 1.0, 0.0)
    nw = nw_ref[0]
    heads = range(GLA_HEADS)
    group = range(GLA_GROUP)

    def kcols(x, h):
        return x[:, h * GLA_DK:(h + 1) * GLA_DK]

    def vcols(x, h):
        return x[:, h * GLA_DV:(h + 1) * GLA_DV]

    def body(gi, states):
        rs = [pl.ds(pl.multiple_of((gi * GLA_GROUP + c) * CHUNK, CHUNK), CHUNK) for c in group]
        bc = [_dot_f32(tril, la_ref[r, :]) for r in rs]
        qe, ke, kt, dec, v = [], [], [], [], []
        for c in group:
            b_last = bc[c][CHUNK - 1:CHUNK, :]
            q = gq_ref[rs[c], :].astype(F32) * GLA_DK ** -0.5
            k = gk_ref[rs[c], :].astype(F32)
            qe.append((q * jnp.exp(bc[c])).astype(BF16))
            ke.append((k * jnp.exp(-bc[c])).astype(BF16))
            kt.append((k * jnp.exp(b_last - bc[c])).astype(BF16))
            dec.append(jnp.exp(b_last))
            v.append(gv_ref[rs[c], :])
        att = [[jnp.where(incl, _dot_nt(kcols(qe[c], h), kcols(ke[c], h)), 0.0).astype(BF16) for h in heads]
               for c in group]
        o_intra = [[_dot(att[c][h], vcols(v[c], h)) for h in heads] for c in group]
        d_state = [[_dot_tn(vcols(v[c], h), kcols(kt[c], h)) for h in heads] for c in group]
        st = list(states)
        for c in group:
            gz = gz_ref[rs[c], :]
            outs = []
            for h in heads:
                o = o_intra[c][h] + _dot_nt(kcols(qe[c], h), st[h].astype(BF16))
                st[h] = st[h] * kcols(dec[c], h) + d_state[c][h]
                outs.append(_rms(o, nw) * _silu(vcols(gz, h)))
            o_ref[rs[c], :] = jnp.concatenate(outs, axis=-1).astype(BF16)
        return tuple(st)

    lax.fori_loop(0, N_CHUNKS // GLA_GROUP, body,
                  tuple(jnp.zeros((GLA_DV, GLA_DK), F32) for _ in heads))


def _gla(l, proj_bf, proj_f32, proj_small, wlr_pad, blr, nw):
    return pl.pallas_call(
        _gla_kernel,
        grid=(BATCH,),
        in_specs=[
            pl.BlockSpec((SEQ, 256), lambda b: (b, 12)),
            pl.BlockSpec((SEQ, 256), lambda b: (b, 13)),
            pl.BlockSpec((SEQ, GLA_W), lambda b: (b, 7)),
            pl.BlockSpec((SEQ, GLA_W), lambda b: (b, 0)),
            pl.BlockSpec((SEQ, LANES), lambda b: (b, GLR_WINDOW)),
            pl.BlockSpec((1, LANES, GLA_HEADS * GLA_DK), lambda b: (l, 0, 0)),
            pl.BlockSpec((1, 1, GLA_HEADS * GLA_DK), lambda b: (l, 0, 0)),
            pl.BlockSpec((1, 1, GLA_DV), lambda b: (l, 0, 0)),
        ],
        out_specs=pl.BlockSpec((SEQ, GLA_W), lambda b: (b, 0)),
        out_shape=jax.ShapeDtypeStruct((TOKENS, GLA_W), BF16),
        scratch_shapes=[pltpu.VMEM((SEQ, GLA_HEADS * GLA_DK), F32)],
        compiler_params=_params("arbitrary"),
        name="gla",
    )(proj_bf, proj_bf, proj_bf, proj_f32, proj_small, wlr_pad, blr, nw)


GDN_HALO = 128
GDN_GROUP = 2
GDN_ROWS = GDN_GROUP * CHUNK
N_GROUPS = N_CHUNKS // GDN_GROUP


def _heads(x, h):
    return x[:, h * GDN_D:(h + 1) * GDN_D]


def _gdn_kernel(dq_ref, dk_ref, dv_ref, dz_ref, sm_ref, cw_ref, alog_ref, dtb_ref, nw_ref, o_ref,
                pq_s, pk_s, pv_s, pg_s, pb_s, u_s, w_s, qe_s, kt_s, qk_s, cd_s, st_s):
    heads = range(GDN_HEADS)
    row = lax.broadcasted_iota(jnp.int32, (CHUNK, CHUNK), 0)
    col = lax.broadcasted_iota(jnp.int32, (CHUNK, CHUNK), 1)
    incl = col <= row
    strict = col < row
    tril = jnp.where(incl, 1.0, 0.0)
    eye = jnp.where(col == row, 1.0, 0.0)
    block_bits = jnp.bitwise_xor(row, col)
    nw = nw_ref[0]
    cw = cw_ref[0]
    neg_a = -jnp.exp(alog_ref[0])
    dtb = dtb_ref[0]

    s_row = lax.broadcasted_iota(jnp.int32, (GDN_ROWS, GDN_HALO + GDN_ROWS), 0)
    s_col = lax.broadcasted_iota(jnp.int32, (GDN_ROWS, GDN_HALO + GDN_ROWS), 1)
    shift_mat = jnp.concatenate(
        [jnp.where(s_col == s_row + (GDN_HALO - (CONV_K - 1) + j), 1.0, 0.0) for j in range(CONV_K - 1)],
        axis=0).astype(BF16)

    def conv(x_ref, src, w, first_block):
        cur = x_ref[pl.ds(src, GDN_ROWS), :]
        if first_block:
            prev = jnp.zeros((GDN_HALO, GDN_W), BF16)
        else:
            prev = x_ref[pl.ds(pl.multiple_of(src - GDN_HALO, GDN_HALO), GDN_HALO), :]
        taps = _dot(shift_mat, jnp.concatenate([prev, cur], axis=0))
        acc = cur.astype(F32) * w[CONV_K - 1:CONV_K]
        for j in range(CONV_K - 1):
            acc = acc + taps[j * GDN_ROWS:(j + 1) * GDN_ROWS] * w[j:j + 1]
        return _silu(acc)

    def l2n(x, scale):
        parts = []
        for h in heads:
            xh = _heads(x, h)
            parts.append(xh * (lax.rsqrt(jnp.sum(xh * xh, axis=-1, keepdims=True) + EPS) * scale))
        return jnp.concatenate(parts, axis=-1)

    def group_rows(g):
        return pl.multiple_of(g * GDN_ROWS, GDN_ROWS)

    def phase0_steps(g, slot, first_block=False):
        src = 0 if first_block else group_rows(g)

        def q_step():
            pq_s[slot] = l2n(conv(dq_ref, src, cw[:, 0:GDN_W], first_block), GDN_D ** -0.5)

        def k_step():
            pk_s[slot] = l2n(conv(dk_ref, src, cw[:, GDN_W:2 * GDN_W], first_block), 1.0)

        def v_step():
            pv_s[slot] = conv(dv_ref, src, cw[:, 2 * GDN_W:3 * GDN_W], first_block)

        def gate_step():
            sm = sm_ref[pl.ds(src, GDN_ROWS), :]
            pg_s[slot] = neg_a * _softplus(sm + dtb)
            pb_s[slot] = _sigmoid(sm)

        return [q_step, k_step, v_step, gate_step]

    for step in phase0_steps(0, 0, first_block=True):
        step()

    def wy_group(slot, fillers):
        fillers = list(fillers)

        def emit():
            if fillers:
                fillers.pop(0)()

        group = range(GDN_GROUP)
        units = [(c, h) for c in group for h in heads]
        crows = [slice(c * CHUNK, (c + 1) * CHUNK) for c in group]
        gcs = [_dot_f32(tril, pg_s[slot, cr, :]) for cr in crows]
        gcs_t = [x.T for x in gcs]
        beta_all = [pb_s[slot, cr, :] for cr in crows]
        qn = [pq_s[slot, cr, :] for cr in crows]
        kn = [pk_s[slot, cr, :] for cr in crows]
        v = [pv_s[slot, cr, :] for cr in crows]
        decay, eg, ekt, cd, k_beta, v_beta, kb = {}, {}, {}, {}, {}, {}, {}
        for c, h in units:
            gc_col = gcs[c][:, DA_LANE + h:DA_LANE + h + 1]
            gc_row = gcs_t[c][DA_LANE + h:DA_LANE + h + 1, :]
            decay[c, h] = jnp.exp(jnp.where(incl, gc_col - gc_row, -jnp.inf))
            gcb = jnp.broadcast_to(gc_col, (CHUNK, GDN_D))
            g_last = gcb[CHUNK - 1:CHUNK, :]
            eg[c, h] = jnp.exp(gcb)
            ekt[c, h] = jnp.exp(g_last - gcb)
            cd[c, h] = jnp.broadcast_to(jnp.exp(g_last), (8, GDN_D))
            b_col = beta_all[c][:, DB_LANE + h:DB_LANE + h + 1]
            k_beta[c, h] = _heads(kn[c], h) * b_col
            v_beta[c, h] = _heads(v[c], h) * b_col
            kb[c, h] = _heads(kn[c], h).astype(BF16)
        lower = {u: jnp.where(strict, _dot_nt(k_beta[u].astype(BF16), kb[u]) * decay[u], 0.0) for u in units}
        emit()
        base = 8
        in_base = lax.shift_right_logical(block_bits, int(math.log2(base))) == 0
        pw = {u: jnp.where(in_base, -lower[u], 0.0) for u in units}
        tq = dict(pw)
        pw = {u: _dot_bf(pw[u], pw[u]) for u in units}
        emit()
        both = {u: _dot_bf(jnp.concatenate([tq[u], pw[u]], axis=0), pw[u]) for u in units}
        emit()
        tq = {u: tq[u] + pw[u] + both[u][0:CHUNK] for u in units}
        pw = {u: both[u][CHUNK:2 * CHUNK] for u in units}
        inv = {u: eye + (tq[u] + pw[u] + _dot_bf(tq[u], pw[u])) for u in units}
        emit()
        for size in (base, 2 * base, 4 * base):
            inside = lax.shift_right_logical(block_bits, int(math.log2(size))) == 1
            nt = {u: _dot_bf(jnp.where(inside, lower[u], 0.0), inv[u]) for u in units}
            emit()
            inv = {u: inv[u] - _dot_bf(inv[u], nt[u]) for u in units}
            emit()
        rhs = {u: jnp.concatenate([v_beta[u], k_beta[u] * eg[u]], axis=-1) for u in units}
        sol = {u: rhs[u] + _dot((inv[u] - eye).astype(BF16), rhs[u].astype(BF16)) for u in units}
        emit()
        qk = {(c, h): _dot_nt(_heads(qn[c], h).astype(BF16), kb[c, h]) * decay[c, h] for c, h in units}
        while fillers:
            emit()
        for c in group:
            cr = crows[c]
            u_s[slot, cr, :] = jnp.concatenate([sol[c, h][:, 0:GDN_D] for h in heads], axis=-1)
            w_s[slot, cr, :] = jnp.concatenate([sol[c, h][:, GDN_D:2 * GDN_D] for h in heads],
                                               axis=-1).astype(BF16)
            qe_s[slot, cr, :] = jnp.concatenate([_heads(qn[c], h) * eg[c, h] for h in heads],
                                                axis=-1).astype(BF16)
            kt_s[slot, cr, :] = jnp.concatenate([_heads(kn[c], h) * ekt[c, h] for h in heads],
                                                axis=-1).astype(BF16)
            qk_s[slot, cr, :] = jnp.concatenate([qk[c, h] for h in heads], axis=-1).astype(BF16)
            cd_s[slot, c] = jnp.concatenate([cd[c, h] for h in heads], axis=-1)

    def scan_steps(g, slot):
        box = {}
        steps = []
        for c in range(GDN_GROUP):
            cr = slice(c * CHUNK, (c + 1) * CHUNK)

            def first(c=c, cr=cr):
                state = [st_s[h] for h in heads] if c == 0 else box["state"]
                sb = [s.astype(BF16) for s in state]
                w = w_s[slot, cr, :]
                qe = qe_s[slot, cr, :]
                box["state"] = state
                box["ws"] = [_dot(_heads(w, h), sb[h]) for h in heads]
                box["qs"] = [_dot(_heads(qe, h), sb[h]) for h in heads]

            def second(c=c, cr=cr):
                u = u_s[slot, cr, :]
                kt = kt_s[slot, cr, :]
                qk = qk_s[slot, cr, :]
                cd = cd_s[slot, c]
                v_new = [(_heads(u, h) - box["ws"][h]).astype(BF16) for h in heads]
                o = [box["qs"][h] + _dot(qk[:, h * CHUNK:(h + 1) * CHUNK], v_new[h]) for h in heads]
                state = [box["state"][h] * _heads(cd, h)[0:1] + _dot_tn(_heads(kt, h), v_new[h]) for h in heads]
                if c == GDN_GROUP - 1:
                    for h in heads:
                        st_s[h] = state[h]
                box["state"] = state
                r = pl.ds(pl.multiple_of((g * GDN_GROUP + c) * CHUNK, CHUNK), CHUNK)
                o = jnp.concatenate([_rms(o[h], nw) for h in heads], axis=-1)
                o_ref[r, :] = (o * _silu(dz_ref[r, :])).astype(BF16)

            steps += [first, second]
        return steps

    def interleave(a, b):
        out = []
        for i in range(max(len(a), len(b))):
            out += a[i:i + 1] + b[i:i + 1]
        return out

    st_s[...] = jnp.zeros((GDN_HEADS, GDN_D, GDN_D), F32)
    wy_group(0, phase0_steps(1, 1))

    def pair(i, carry):
        g = 2 * i
        wy_group(1, interleave(scan_steps(g, 0), phase0_steps(g + 2, 0)))
        wy_group(0, interleave(scan_steps(g + 1, 1), phase0_steps(g + 3, 1)))
        return carry

    lax.fori_loop(0, (N_GROUPS - 2) // 2, pair, 0)
    wy_group(1, scan_steps(N_GROUPS - 2, 0))
    for step in scan_steps(N_GROUPS - 1, 1):
        step()


def _gdn(l, proj_bf, proj_f32, proj_small, conv_w, alog_pad, dtb_pad, nw):
    slot_wide = pltpu.VMEM((2, GDN_ROWS, GDN_W), F32)
    slot_wide_bf = pltpu.VMEM((2, GDN_ROWS, GDN_W), BF16)
    slot_narrow = pltpu.VMEM((2, GDN_ROWS, LANES), F32)
    return pl.pallas_call(
        _gdn_kernel,
        grid=(BATCH,),
        in_specs=[
            pl.BlockSpec((SEQ, GDN_W), lambda b: (b, 8)),
            pl.BlockSpec((SEQ, GDN_W), lambda b: (b, 9)),
            pl.BlockSpec((SEQ, GDN_W), lambda b: (b, 10)),
            pl.BlockSpec((SEQ, GDN_W), lambda b: (b, 1)),
            pl.BlockSpec((SEQ, LANES), lambda b: (b, DAB_WINDOW)),
            pl.BlockSpec((1, CONV_K, 3 * GDN_W), lambda b: (l, 0, 0)),
            pl.BlockSpec((1, 1, LANES), lambda b: (l, 0, 0)),
            pl.BlockSpec((1, 1, LANES), lambda b: (l, 0, 0)),
            pl.BlockSpec((1, 1, GDN_D), lambda b: (l, 0, 0)),
        ],
        out_specs=pl.BlockSpec((SEQ, GDN_W), lambda b: (b, 0)),
        out_shape=jax.ShapeDtypeStruct((TOKENS, GDN_W), BF16),
        scratch_shapes=[slot_wide, slot_wide, slot_wide,
                        slot_narrow, slot_narrow,
                        slot_wide,
                        slot_wide_bf, slot_wide_bf, slot_wide_bf,
                        pltpu.VMEM((2, GDN_ROWS, GDN_HEADS * CHUNK), BF16),
                        pltpu.VMEM((2, GDN_GROUP, 8, GDN_W), F32),
                        pltpu.VMEM((GDN_HEADS, GDN_D, GDN_D), F32)],
        compiler_params=_params("arbitrary"),
        name="gdn",
    )(proj_bf, proj_bf, proj_bf, proj_f32, proj_small, conv_w, alog_pad, dtb_pad, nw)


PREP_TS = 512
ATT_TQ = 256


def _diff_prep_kernel(aq_ref, ak_ref, cos_ref, sin_ref, qw_ref, kw_ref, q_ref, k_ref):
    cos2 = cos_ref[0]
    sin2 = sin_ref[0]
    r_i = lax.broadcasted_iota(jnp.int32, (DIFF_D, DIFF_D), 0)
    c_i = lax.broadcasted_iota(jnp.int32, (DIFF_D, DIFF_D), 1)
    swap_halves = jnp.where(jnp.bitwise_xor(r_i, c_i) == DIFF_D // 2, 1.0, 0.0).astype(BF16)
    for src, w_ref, dst, scale in ((aq_ref, qw_ref, q_ref, DIFF_D ** -0.5 * LOG2E), (ak_ref, kw_ref, k_ref, 1.0)):
        w = w_ref[0]
        for g in range(2 * DIFF_HEADS):
            cols = slice(g * DIFF_D, (g + 1) * DIFF_D)
            y = _rms(src[:, cols].astype(F32), w)
            y = y * cos2 + _dot(y.astype(BF16), swap_halves) * sin2
            dst[:, cols] = (y * scale).astype(BF16)


def _diff_prep(l, proj_bf, cos2, sin2, qw, kw):
    per_batch = SEQ // PREP_TS
    width = 2 * DIFF_HEADS * DIFF_D
    tab_spec = pl.BlockSpec((1, PREP_TS, DIFF_D), lambda i: (i // per_batch, i % per_batch, 0))
    out = jax.ShapeDtypeStruct((TOKENS, width), BF16)
    return pl.pallas_call(
        _diff_prep_kernel,
        grid=(TOKENS // PREP_TS,),
        in_specs=[
            pl.BlockSpec((PREP_TS, width), lambda i: (i, 0)),
            pl.BlockSpec((PREP_TS, width), lambda i: (i, 1)),
            tab_spec, tab_spec,
            pl.BlockSpec((1, 1, DIFF_D), lambda i: (l, 0, 0)),
            pl.BlockSpec((1, 1, DIFF_D), lambda i: (l, 0, 0)),
        ],
        out_specs=[pl.BlockSpec((PREP_TS, width), lambda i: (i, 0))] * 2,
        out_shape=[out, out],
        compiler_params=_params("arbitrary"),
        name="diff_prep",
    )(proj_bf, proj_bf, cos2, sin2, qw, kw)


def _diff_attn_kernel(q1_ref, q2_ref, k1_ref, k2_ref, v_ref, az_ref, lam_ref, nw_ref, o_ref, *, lam_init):
    lv = lam_ref[0]
    lam = (jnp.exp(jnp.sum(lv[0:1] * lv[1:2], axis=-1, keepdims=True))
           - jnp.exp(jnp.sum(lv[2:3] * lv[3:4], axis=-1, keepdims=True)) + lam_init)
    nw = nw_ref[0]
    row = lax.broadcasted_iota(jnp.int32, (ATT_TQ, ATT_TQ), 0)
    col = lax.broadcasted_iota(jnp.int32, (ATT_TQ, ATT_TQ), 1)
    causal = col <= row
    q_refs = (q1_ref, q2_ref)
    k_refs = (k1_ref, k2_ref)
    n_blocks = SEQ // ATT_TQ

    def scores(i):
        start = i * ATT_TQ
        rows = slice(start, start + ATT_TQ)
        out = []
        for m in range(2):
            q = q_refs[m][rows, :]
            s_diag = jnp.where(causal, _dot_nt(q, k_refs[m][rows, :]), -jnp.inf)
            s_past = _dot_nt(q, k_refs[m][0:start, :]) if i > 0 else None
            out.append((s_diag, s_past))
        return out

    s_next = scores(0)
    for i in range(n_blocks):
        start = i * ATT_TQ
        rows = slice(start, start + ATT_TQ)
        s_cur = s_next
        if i + 1 < n_blocks:
            s_next = scores(i + 1)
        ps, inv_l = [], []
        for s_diag, s_past in s_cur:
            mx = jnp.max(s_diag, axis=-1, keepdims=True)
            if i > 0:
                mx = jnp.maximum(mx, jnp.max(s_past, axis=-1, keepdims=True))
            p_diag = jnp.exp2(s_diag - mx)
            l = jnp.sum(p_diag, axis=-1, keepdims=True)
            p_past = None
            if i > 0:
                p_past = jnp.exp2(s_past - mx)
                l = l + jnp.sum(p_past, axis=-1, keepdims=True)
                p_past = p_past.astype(BF16)
            ps.append((p_diag.astype(BF16), p_past))
            inv_l.append(1.0 / l)
        pv = []
        for p_diag, p_past in ps:
            acc = _dot(p_diag, v_ref[rows, :])
            if i > 0:
                acc = acc + _dot(p_past, v_ref[0:start, :])
            pv.append(acc)
        o = pv[0] * inv_l[0] - pv[1] * (lam * inv_l[1])
        o = _rms(o, nw) * (1.0 - lam_init)
        o_ref[rows, :] = (o * _silu(az_ref[rows, :])).astype(BF16)


def _diff_attn(l, q_d, k_d, proj_bf, proj_f32, lam, nw):
    lam_init = 0.8 - 0.6 * math.exp(-0.3 * l)
    return pl.pallas_call(
        functools.partial(_diff_attn_kernel, lam_init=lam_init),
        grid=(BATCH, DIFF_HEADS),
        in_specs=[
            pl.BlockSpec((SEQ, DIFF_D), lambda b, h: (b, 2 * h)),
            pl.BlockSpec((SEQ, DIFF_D), lambda b, h: (b, 2 * h + 1)),
            pl.BlockSpec((SEQ, DIFF_D), lambda b, h: (b, 2 * h)),
            pl.BlockSpec((SEQ, DIFF_D), lambda b, h: (b, 2 * h + 1)),
            pl.BlockSpec((SEQ, DIFF_DV), lambda b, h: (b, 8 + h)),
            pl.BlockSpec((SEQ, DIFF_DV), lambda b, h: (b, 4 + h)),
            pl.BlockSpec((1, 4, DIFF_D), lambda b, h: (l, 0, 0)),
            pl.BlockSpec((1, 1, DIFF_DV), lambda b, h: (l, 0, 0)),
        ],
        out_specs=pl.BlockSpec((SEQ, DIFF_DV), lambda b, h: (b, h)),
        out_shape=jax.ShapeDtypeStruct((TOKENS, DIFF_W), BF16),
        compiler_params=_params("arbitrary", "arbitrary"),
        name="diff_attn",
    )(q_d, q_d, k_d, k_d, proj_bf, proj_f32, lam, nw)


OUT_TM = 512


OUT_WROWS = 256


def _outproj_kernel(og_ref, od_ref, oa_ref, w_ref, x_ref, gate_ref, o_ref, wb_ref):
    @pl.when(pl.program_id(0) == 0)
    def _():
        def cast(i, carry):
            rows = pl.ds(pl.multiple_of(i * OUT_WROWS, OUT_WROWS), OUT_WROWS)
            wb_ref[rows, :] = w_ref[0, rows, :].astype(BF16)
            return carry

        lax.fori_loop(0, D_MODEL // OUT_WROWS, cast, 0)

    y = _dot(og_ref[...], wb_ref[0:GLA_W, :])
    y = y + _dot(od_ref[...], wb_ref[GLA_W:GLA_W + GDN_W, :])
    y = y + _dot(oa_ref[...], wb_ref[GLA_W + GDN_W:, :])
    o_ref[...] = x_ref[...] + gate_ref[0, 0] * y


def _out_proj(l, o_gla, o_gdn, o_diff, w_out, x2d, mods4):
    per_batch = SEQ // OUT_TM
    return pl.pallas_call(
        _outproj_kernel,
        grid=(TOKENS // OUT_TM,),
        in_specs=[
            pl.BlockSpec((OUT_TM, GLA_W), lambda m: (m, 0)),
            pl.BlockSpec((OUT_TM, GDN_W), lambda m: (m, 0)),
            pl.BlockSpec((OUT_TM, DIFF_W), lambda m: (m, 0)),
            pl.BlockSpec((1, D_MODEL, D_MODEL), lambda m: (l, 0, 0), pipeline_mode=pl.Buffered(1)),
            pl.BlockSpec((OUT_TM, D_MODEL), lambda m: (m, 0)),
            pl.BlockSpec((1, 1, 1, D_MODEL), lambda m: (l, m // per_batch, 0, 2)),
        ],
        out_specs=pl.BlockSpec((OUT_TM, D_MODEL), lambda m: (m, 0)),
        out_shape=jax.ShapeDtypeStruct((TOKENS, D_MODEL), F32),
        scratch_shapes=[pltpu.VMEM((D_MODEL, D_MODEL), BF16)],
        compiler_params=_params("arbitrary", vmem=52 * 1024 * 1024),
        name="out_proj",
    )(o_gla, o_gdn, o_diff, w_out, x2d, mods4)


def _lane_place(vecs, lane):
    n = vecs.shape[1]
    return jnp.pad(vecs.astype(F32), ((0, 0), (lane, LANES - lane - n))).reshape(DEPTH, 1, LANES)


def kernel(x, c, positions, norm_w, w_ada, b_ada, w_in, gla_w_lr, gla_b_lr, gla_norm_w, gdn_conv_w, gdn_a_log,
           gdn_dt_bias, gdn_norm_w, diff_q_norm_w, diff_k_norm_w, diff_lambda, diff_norm_w, w_out):
    c_pad = jnp.pad(c, ((0, 8 - BATCH), (0, 0)))
    mods4 = _ada_mod(c_pad, w_ada, b_ada).reshape(DEPTH, 8, 1, 3 * D_MODEL)
    cos2, sin2 = _rope_tables(positions)
    w_in_t = jnp.swapaxes(w_in, 1, 2)
    row3 = lambda p: p.reshape(DEPTH, 1, p.shape[-1])
    norm_w3 = row3(norm_w)
    wlr_pad = jnp.pad(gla_w_lr, ((0, 0), (GLR_LANE, LANES - GLR_LANE - GLA_RANK), (0, 0)))
    alog_pad = _lane_place(gdn_a_log, DA_LANE)
    dtb_pad = _lane_place(gdn_dt_bias, DA_LANE)
    x2d = x.reshape(TOKENS, D_MODEL)
    for l in range(DEPTH):
        proj_bf, proj_f32, proj_small = _in_proj(l, x2d, norm_w3, mods4, w_in_t)
        o_gla = _gla(l, proj_bf, proj_f32, proj_small, wlr_pad, row3(gla_b_lr), row3(gla_norm_w))
        o_gdn = _gdn(l, proj_bf, proj_f32, proj_small, gdn_conv_w, alog_pad, dtb_pad, row3(gdn_norm_w))
        q_d, k_d = _diff_prep(l, proj_bf, cos2, sin2, row3(diff_q_norm_w), row3(diff_k_norm_w))
        o_diff = _diff_attn(l, q_d, k_d, proj_bf, proj_f32, diff_lambda, row3(diff_norm_w))
        x2d = _out_proj(l, o_gla, o_gdn, o_diff, w_out, x2d, mods4)
    return x2d.reshape(BATCH, SEQ, D_MODEL)
```
